```python
import math
import jax, jax.numpy as jnp
from jax import lax
import numpy as np

D_MODEL = 1024
BATCH = 4
SEQ = 4096
DEPTH = 2

MIX_WIDTH = D_MODEL
A_WIDTH = MIX_WIDTH // 2
B_WIDTH = MIX_WIDTH - A_WIDTH
A_HEADS = 4
A_V_HEAD = A_WIDTH // A_HEADS
A_QK_HEAD = A_V_HEAD // 2
B_HEAD = 64
B_Q_HEADS = B_WIDTH // B_HEAD
B_KV_HEADS = 2
B_REP = B_Q_HEADS // B_KV_HEADS
GRID_W = 64
ROPE_THETA = 10000.0
NUM_BUCKETS = 32
MAX_DISTANCE = 128
Q_BLOCK = 128
EPS = 1e-6
SPLIT_SIZES = [A_WIDTH, A_WIDTH, A_WIDTH, A_WIDTH,
               B_WIDTH, B_KV_HEADS * B_HEAD, B_KV_HEADS * B_HEAD, B_WIDTH]
IN_COLS = sum(SPLIT_SIZES)
SPLIT_POINTS = [int(v) for v in np.cumsum(SPLIT_SIZES)[:-1]]

kernel_name = "hymba_diffattn_gqa_axialrope_encoder"


def rms_norm(x, g):
    xf = x.astype(jnp.float32)
    y = xf * lax.rsqrt(jnp.mean(xf * xf, axis=-1, keepdims=True) + EPS)
    return (y * g.astype(jnp.float32)).astype(x.dtype)


def t5_bucket(rel):
    nb = NUM_BUCKETS // 2
    max_exact = nb // 2
    n = jnp.abs(rel)
    large = max_exact + (jnp.log(jnp.maximum(n, 1).astype(jnp.float32) / max_exact)
                         / math.log(MAX_DISTANCE / max_exact) * (nb - max_exact)).astype(jnp.int32)
    large = jnp.minimum(large, nb - 1)
    return jnp.where(rel > 0, nb, 0) + jnp.where(n < max_exact, n, large)


def axial_rope_angles(n):
    rows = n // GRID_W
    row = jnp.repeat(jnp.arange(rows, dtype=jnp.float32), GRID_W)
    col = jnp.tile(jnp.arange(GRID_W, dtype=jnp.float32), rows)
    axis_dim = B_HEAD // 2
    inv = ROPE_THETA ** (-jnp.arange(0, axis_dim, 2, dtype=jnp.float32) / axis_dim)
    ang = jnp.concatenate([row[:, None] * inv, col[:, None] * inv], axis=-1)
    return jnp.cos(ang), jnp.sin(ang)


def apply_rope(x, cos, sin):
    xf = x.astype(jnp.float32).reshape(*x.shape[:-1], -1, 2)
    x0, x1 = xf[..., 0], xf[..., 1]
    c = cos[None, :, None, :]
    s = sin[None, :, None, :]
    out = jnp.stack([x0 * c - x1 * s, x0 * s + x1 * c], axis=-1).reshape(x.shape)
    return out.astype(x.dtype)


def diff_attention(q, k, v, lam, rel_bias):
    bn, s_len, h, _, d = q.shape
    dv = v.shape[-1]
    nblk = s_len // Q_BLOCK
    q = q * (d ** -0.5)
    qb = q.reshape(bn, nblk, Q_BLOCK, h, 2, d).transpose(1, 4, 0, 3, 2, 5)
    kt = k.transpose(3, 0, 2, 1, 4)
    vt = v.transpose(0, 2, 1, 3)
    kpos = jnp.arange(s_len, dtype=jnp.int32)
    starts = jnp.arange(nblk, dtype=jnp.int32) * Q_BLOCK

    def block(args):
        qblk, start = args
        qpos = start + jnp.arange(Q_BLOCK, dtype=jnp.int32)
        bias = rel_bias[t5_bucket(kpos[None, :] - qpos[:, None])]
        bias = bias.astype(jnp.float32).transpose(2, 0, 1)
        sc = jnp.einsum('cbhqd,cbhkd->cbhqk', qblk, kt).astype(jnp.float32) + bias
        p = jax.nn.softmax(sc, axis=-1)
        a = p[0] - lam * p[1]
        return jnp.einsum('bhqk,bhkv->bhqv', a.astype(vt.dtype), vt)

    o = lax.map(block, (qb, starts))
    return o.transpose(1, 0, 3, 2, 4).reshape(bn, s_len, h, dv)


def gqa_attention(q, k, v):
    bn, s_len, _, d = q.shape
    nblk = s_len // Q_BLOCK
    q = q * (d ** -0.5)
    qb = q.reshape(bn, nblk, Q_BLOCK, B_KV_HEADS, B_REP, d).transpose(1, 0, 3, 4, 2, 5)
    kt = k.transpose(0, 2, 1, 3)
    vt = v.transpose(0, 2, 1, 3)

    def block(qblk):
        sc = jnp.einsum('bgrqd,bgkd->bgrqk', qblk, kt).astype(jnp.float32)
        p = jax.nn.softmax(sc, axis=-1).astype(vt.dtype)
        return jnp.einsum('bgrqk,bgkd->bgrqd', p, vt)

    o = lax.map(block, qb)
    return o.transpose(1, 0, 4, 2, 3, 5).reshape(bn, s_len, B_Q_HEADS * d)


def setup_inputs(seed: int = 0) -> dict:
    key = jax.random.key(seed)
    ks = jax.random.split(key, 10)
    f32 = jnp.float32
    x = jax.random.normal(ks[0], (BATCH, SEQ, D_MODEL), f32)
    rel_bias = 0.5 * jax.random.normal(ks[1], (NUM_BUCKETS, A_HEADS), f32)
    pre_norm_g = 1.0 + 0.05 * jax.random.normal(ks[2], (DEPTH, D_MODEL), f32)
    w_in = jax.random.normal(ks[3], (DEPTH, D_MODEL, IN_COLS), f32) * D_MODEL ** -0.5
    diff_lambda = 0.1 * jax.random.normal(ks[4], (DEPTH, 4, A_QK_HEAD), f32)
    diff_subln_g = 1.0 + 0.05 * jax.random.normal(ks[5], (DEPTH, A_V_HEAD), f32)
    q_norm_g = 1.0 + 0.05 * jax.random.normal(ks[6], (DEPTH, B_HEAD), f32)
    k_norm_g = 1.0 + 0.05 * jax.random.normal(ks[7], (DEPTH, B_HEAD), f32)
    w_out = jax.random.normal(ks[8], (DEPTH, MIX_WIDTH, D_MODEL), f32) * MIX_WIDTH ** -0.5
    post_norm_g = 1.0 + 0.05 * jax.random.normal(ks[9], (DEPTH, D_MODEL), f32)
    return {"x": x, "rel_bias": rel_bias, "pre_norm_g": pre_norm_g, "w_in": w_in,
            "diff_lambda": diff_lambda, "diff_subln_g": diff_subln_g,
            "q_norm_g": q_norm_g, "k_norm_g": k_norm_g, "w_out": w_out,
            "post_norm_g": post_norm_g}


def reference(x, rel_bias, pre_norm_g, w_in, diff_lambda, diff_subln_g, q_norm_g, k_norm_g,
              w_out, post_norm_g):
    bn, s_len, _ = x.shape
    cos, sin = axial_rope_angles(s_len)
    for l in range(DEPTH):
        h = rms_norm(x, pre_norm_g[l])
        proj = h @ w_in[l]
        aq, ak, av, ag, bq, bk, bv, bg = jnp.split(proj, SPLIT_POINTS, axis=-1)

        lam_init = 0.8 - 0.6 * math.exp(-0.3 * l)
        lp = diff_lambda[l].astype(jnp.float32)
        lam = jnp.exp(jnp.sum(lp[0] * lp[1])) - jnp.exp(jnp.sum(lp[2] * lp[3])) + lam_init
        oa = diff_attention(aq.reshape(bn, s_len, A_HEADS, 2, A_QK_HEAD),
                            ak.reshape(bn, s_len, A_HEADS, 2, A_QK_HEAD),
                            av.reshape(bn, s_len, A_HEADS, A_V_HEAD), lam, rel_bias)
        oa = rms_norm(oa, diff_subln_g[l]) * (1.0 - lam_init)
        ya = oa.reshape(bn, s_len, A_WIDTH) * jax.nn.silu(ag)

        q = apply_rope(rms_norm(bq.reshape(bn, s_len, B_Q_HEADS, B_HEAD), q_norm_g[l]), cos, sin)
        k = apply_rope(rms_norm(bk.reshape(bn, s_len, B_KV_HEADS, B_HEAD), k_norm_g[l]), cos, sin)
        ob = gqa_attention(q, k, bv.reshape(bn, s_len, B_KV_HEADS, B_HEAD))
        yb = ob * jax.nn.silu(bg)

        y = jnp.concatenate([ya, yb], axis=-1) @ w_out[l]
        x = x + rms_norm(y, post_norm_g[l])
    return x
```

```python
import functools
import math

import jax
import jax.numpy as jnp
import numpy as np
from jax import lax
from jax.experimental import pallas as pl
from jax.experimental.pallas import tpu as pltpu

D_MODEL = 1024
DEPTH = 2
A_WIDTH = 512
B_WIDTH = 512
A_HEADS = 4
A_V_HEAD = 128
A_QK_HEAD = 64
B_HEAD = 64
B_Q_HEADS = 8
B_KV_HEADS = 2
GRID_W = 64
ROPE_THETA = 10000.0
NUM_BUCKETS = 32
MAX_DISTANCE = 128
EPS = 1e-6
LOG2E = math.log2(math.e)
Q_SCALE = (A_QK_HEAD ** -0.5) * LOG2E

LANES = 128
ROW_TILE = 512
Q_TILE = 256
K_CHUNK = 256
BAND_CHUNKS = 3
VMEM_LIMIT = 48 * 1024 * 1024

_AQ, _AK, _AV, _AG, _BQ, _BK, _BV, _BG = 0, 512, 1024, 1536, 2048, 2560, 2688, 2816
_HEAD_ORDER = (0, 4, 1, 5, 2, 6, 3, 7)
_EVEN_ODD = tuple(range(0, B_HEAD, 2)) + tuple(range(1, B_HEAD, 2))


def _t5_bucket(rel):
    nb = NUM_BUCKETS // 2
    max_exact = nb // 2
    n = jnp.abs(rel)
    large = max_exact + (jnp.log(jnp.maximum(n, 1).astype(jnp.float32) / max_exact)
                         / math.log(MAX_DISTANCE / max_exact) * (nb - max_exact)).astype(jnp.int32)
    large = jnp.minimum(large, nb - 1)
    return jnp.where(rel > 0, nb, 0) + jnp.where(n < max_exact, n, large)


def _rope_tables_t(n):
    rows = n // GRID_W
    row = jnp.repeat(jnp.arange(rows, dtype=jnp.float32), GRID_W)
    col = jnp.tile(jnp.arange(GRID_W, dtype=jnp.float32), rows)
    axis_dim = B_HEAD // 2
    inv = ROPE_THETA ** (-jnp.arange(0, axis_dim, 2, dtype=jnp.float32) / axis_dim)
    ang = jnp.concatenate([row[:, None] * inv, col[:, None] * inv], axis=-1)
    return jnp.cos(ang).T, jnp.sin(ang).T


def _cparams(n_axes):
    return pltpu.CompilerParams(dimension_semantics=("arbitrary",) * n_axes,
                                vmem_limit_bytes=VMEM_LIMIT)


def _band_kernel(rb_ref, idx_ref, out_ref):
    h = pl.program_id(0)
    rows_per_step = 8
    steps_per_chunk = Q_TILE // rows_per_step

    def body(t, carry):
        d = t // steps_per_chunk
        r0 = pl.multiple_of((t % steps_per_chunk) * rows_per_step, rows_per_step)
        idx = idx_ref[d, pl.ds(r0, rows_per_step), :]
        acc = jnp.zeros(idx.shape, jnp.float32)
        for b in range(NUM_BUCKETS):
            acc = jnp.where(idx == b, rb_ref[b, h] * LOG2E, acc)
        out_ref[0, d, pl.ds(r0, rows_per_step), :] = acc
        return carry

    lax.fori_loop(0, BAND_CHUNKS * steps_per_chunk, body, 0)


def _band_call(rel_bias):
    d = jnp.arange(BAND_CHUNKS, dtype=jnp.int32)[:, None, None]
    r = jnp.arange(Q_TILE, dtype=jnp.int32)[None, :, None]
    j = jnp.arange(K_CHUNK, dtype=jnp.int32)[None, None, :]
    idx = _t5_bucket((d - 1) * K_CHUNK + j - r)
    return pl.pallas_call(
        _band_kernel,
        grid=(A_HEADS,),
        in_specs=[pl.BlockSpec(memory_space=pltpu.SMEM),
                  pl.BlockSpec((BAND_CHUNKS, Q_TILE, K_CHUNK), lambda h: (0, 0, 0))],
        out_specs=pl.BlockSpec((1, BAND_CHUNKS, Q_TILE, K_CHUNK), lambda h: (h, 0, 0, 0)),
        out_shape=jax.ShapeDtypeStruct((A_HEADS, BAND_CHUNKS, Q_TILE, K_CHUNK), jnp.float32),
        compiler_params=_cparams(1),
        name="bias_band",
    )(rel_bias, idx)


def _proj_kernel(x_ref, g_ref, wn_ref, wt_ref, cos_ref, sin_ref, qg_ref, kg_ref,
                 qa_ref, va_ref, sa_ref, vb_ref, sb_ref, qb_ref, kta_ref, ktb_ref):
    f32, bf16 = jnp.float32, jnp.bfloat16
    x = x_ref[...]
    ms = jnp.mean(x * x, axis=-1, keepdims=True)
    h = (x * lax.rsqrt(ms + EPS) * g_ref[...]).astype(bf16)

    def nat(lo, hi):
        return jnp.dot(h, wn_ref[:, lo:hi], preferred_element_type=f32)

    def tr(lo, hi):
        return lax.dot_general(wt_ref[lo:hi, :], h, (((1,), (1,)), ((), ())),
                               preferred_element_type=f32)

    def silu(v):
        return v * jax.nn.sigmoid(v)

    qa_ref[...] = (nat(0, 512) * Q_SCALE).astype(bf16)
    va_ref[...] = nat(512, 1024).astype(bf16)
    sa_ref[...] = silu(nat(1024, 1536)).astype(bf16)
    vb_ref[...] = nat(1536, 1664).astype(bf16)
    sb_ref[...] = silu(nat(1664, 2176)).astype(bf16)

    n_chunks = ROW_TILE // K_CHUNK
    kta = tr(0, 512)
    for j in range(n_chunks):
        kta_ref[0, j] = kta[:, j * K_CHUNK:(j + 1) * K_CHUNK].astype(bf16)

    cos = cos_ref[...]
    sin = sin_ref[...]
    half = B_HEAD // 2

    def norm_rope(xt, g):
        msq = jnp.mean(xt * xt, axis=0, keepdims=True)
        y = xt * lax.rsqrt(msq + EPS) * g
        e, o = y[:half], y[half:]
        return jnp.concatenate([e * cos - o * sin, e * sin + o * cos], axis=0)

    qg = qg_ref[...]
    qt = tr(512, 1024)
    qn = jnp.concatenate([norm_rope(qt[i * B_HEAD:(i + 1) * B_HEAD], qg)
                          for i in range(B_Q_HEADS)], axis=0) * Q_SCALE
    qb_ref[...] = qn.T.astype(bf16)

    kg = kg_ref[...]
    kt = tr(1024, 1152)
    kn = jnp.concatenate([norm_rope(kt[i * B_HEAD:(i + 1) * B_HEAD], kg)
                          for i in range(B_KV_HEADS)], axis=0)
    for j in range(n_chunks):
        ktb_ref[0, j] = kn[:, j * K_CHUNK:(j + 1) * K_CHUNK].astype(bf16)


def _proj_call(xf, g_pre, wn, wt, cos_t, sin_t, qg, kg, batch, seq):
    m = xf.shape[0]
    tiles_per_seq = seq // ROW_TILE
    chunks_per_tile = ROW_TILE // K_CHUNK
    n_chunks = seq // K_CHUNK
    bf16 = jnp.bfloat16
    row = lambda i: (i, 0)
    const = lambda i: (0, 0)
    ktmap = lambda i: (i // tiles_per_seq, i % tiles_per_seq, 0, 0)
    return pl.pallas_call(
        _proj_kernel,
        grid=(m // ROW_TILE,),
        in_specs=[pl.BlockSpec((ROW_TILE, D_MODEL), row),
                  pl.BlockSpec((1, D_MODEL), const),
                  pl.BlockSpec(wn.shape, const),
                  pl.BlockSpec(wt.shape, const),
                  pl.BlockSpec((B_HEAD // 2, ROW_TILE), lambda i: (0, i % tiles_per_seq)),
                  pl.BlockSpec((B_HEAD // 2, ROW_TILE), lambda i: (0, i % tiles_per_seq)),
                  pl.BlockSpec((B_HEAD, 1), const),
                  pl.BlockSpec((B_HEAD, 1), const)],
        out_specs=[pl.BlockSpec((ROW_TILE, A_WIDTH), row),
                   pl.BlockSpec((ROW_TILE, A_WIDTH), row),
                   pl.BlockSpec((ROW_TILE, A_WIDTH), row),
                   pl.BlockSpec((ROW_TILE, B_KV_HEADS * B_HEAD), row),
                   pl.BlockSpec((ROW_TILE, B_WIDTH), row),
                   pl.BlockSpec((ROW_TILE, B_WIDTH), row),
                   pl.BlockSpec((1, chunks_per_tile, A_WIDTH, K_CHUNK), ktmap),
                   pl.BlockSpec((1, chunks_per_tile, B_KV_HEADS * B_HEAD, K_CHUNK), ktmap)],
        out_shape=[jax.ShapeDtypeStruct((m, A_WIDTH), bf16),
                   jax.ShapeDtypeStruct((m, A_WIDTH), bf16),
                   jax.ShapeDtypeStruct((m, A_WIDTH), bf16),
                   jax.ShapeDtypeStruct((m, B_KV_HEADS * B_HEAD), bf16),
                   jax.ShapeDtypeStruct((m, B_WIDTH), bf16),
                   jax.ShapeDtypeStruct((m, B_WIDTH), bf16),
                   jax.ShapeDtypeStruct((batch, n_chunks, A_WIDTH, K_CHUNK), bf16),
                   jax.ShapeDtypeStruct((batch, n_chunks, B_KV_HEADS * B_HEAD, K_CHUNK), bf16)],
        compiler_params=_cparams(1),
        name="in_proj",
    )(xf, g_pre, wn, wt, cos_t, sin_t, qg, kg)


def _attn_kernel(*refs, diff, lam_init, n_chunks):
    f32, bf16 = jnp.float32, jnp.bfloat16
    if diff:
        (rb_ref, lamp_ref, g_ref, q_ref, kt_ref, v_ref, gate_ref, band_ref,
         o_ref, lhs_ref, s_ref, m_ref, l_ref, acc_ref) = refs
    else:
        (q_ref, kt_ref, v_ref, gate_ref,
         o_ref, lhs_ref, s_ref, m_ref, l_ref, acc_ref) = refs
    tq = Q_TILE
    qi = pl.program_id(2)

    low = lax.broadcasted_iota(jnp.int32, (tq, LANES), 1) < (LANES // 2)
    q = q_ref[...].astype(f32)
    lhs_ref[0:tq, :] = jnp.where(low, q, 0.0).astype(bf16)
    lhs_ref[tq:2 * tq, :] = jnp.where(low, 0.0, q).astype(bf16)

    if diff:
        head = pl.program_id(1)
        c_left = rb_ref[NUM_BUCKETS // 2 - 1, head] * LOG2E
        c_right = rb_ref[NUM_BUCKETS - 1, head] * LOG2E

    m_ref[...] = jnp.full(m_ref.shape, -jnp.inf, f32)

    def scores(c, carry):
        s = jnp.dot(lhs_ref[...], kt_ref[0, c], preferred_element_type=f32)
        if diff:
            d = c - qi + 1
            in_band = jnp.logical_and(d >= 0, d < BAND_CHUNKS)
            band = band_ref[0, jnp.clip(d, 0, BAND_CHUNKS - 1)]
            bias = jnp.where(in_band, band, jnp.where(d < 0, c_left, c_right))
            s = s + jnp.concatenate([bias, bias], axis=0)
        s_ref[c] = s
        m_ref[...] = jnp.maximum(m_ref[...], jnp.maximum(s[:, :LANES], s[:, LANES:]))
        return carry

    lax.fori_loop(0, n_chunks, scores, 0)

    m = jnp.max(m_ref[...], axis=-1, keepdims=True)
    m_ref[...] = jnp.broadcast_to(m, m_ref.shape)
    l_ref[...] = jnp.zeros(l_ref.shape, f32)
    acc_ref[...] = jnp.zeros(acc_ref.shape, f32)

    def weighted(c, carry):
        s = s_ref[c]
        mb = m_ref[...]
        p_lo = jnp.exp2(s[:, :LANES] - mb)
        p_hi = jnp.exp2(s[:, LANES:] - mb)
        l_ref[...] += p_lo + p_hi
        p = jnp.concatenate([p_lo, p_hi], axis=1).astype(bf16)
        k0 = pl.multiple_of(c * K_CHUNK, K_CHUNK)
        acc_ref[...] += jnp.dot(p, v_ref[pl.ds(k0, K_CHUNK), :], preferred_element_type=f32)
        return carry

    lax.fori_loop(0, n_chunks, weighted, 0)

    l = jnp.sum(l_ref[...], axis=-1, keepdims=True)
    acc = acc_ref[...]
    o_lo = acc[:tq] / l[:tq]
    o_hi = acc[tq:] / l[tq:]
    gate = gate_ref[...].astype(f32)
    if diff:
        lp = lamp_ref[...]
        lam = (jnp.exp(jnp.sum(lp[0:1] * lp[1:2], axis=-1, keepdims=True))
               - jnp.exp(jnp.sum(lp[2:3] * lp[3:4], axis=-1, keepdims=True)) + lam_init)
        o = o_lo - lam * o_hi
        ms = jnp.mean(o * o, axis=-1, keepdims=True)
        o = (o * lax.rsqrt(ms + EPS) * g_ref[...]) * (1.0 - lam_init)
    else:
        o = jnp.where(low, o_lo, o_hi)
    o_ref[...] = (o * gate).astype(o_ref.dtype)


def _attn_scratch():
    return [pltpu.VMEM((2 * Q_TILE, LANES), jnp.bfloat16),
            None,
            pltpu.VMEM((2 * Q_TILE, LANES), jnp.float32),
            pltpu.VMEM((2 * Q_TILE, LANES), jnp.float32),
            pltpu.VMEM((2 * Q_TILE, LANES), jnp.float32)]


def _attn_diff_call(rel_bias, lam_params, subln_g, qa, kta, va, sa, band, lam_init, batch, seq):
    m = qa.shape[0]
    nq = seq // Q_TILE
    n_chunks = seq // K_CHUNK
    scratch = _attn_scratch()
    scratch[1] = pltpu.VMEM((n_chunks, 2 * Q_TILE, K_CHUNK), jnp.float32)
    qmap = lambda b, h, i: (b * nq + i, h)
    return pl.pallas_call(
        functools.partial(_attn_kernel, diff=True, lam_init=lam_init, n_chunks=n_chunks),
        grid=(batch, A_HEADS, nq),
        in_specs=[pl.BlockSpec(memory_space=pltpu.SMEM),
                  pl.BlockSpec((4, A_QK_HEAD), lambda b, h, i: (0, 0)),
                  pl.BlockSpec((1, A_V_HEAD), lambda b, h, i: (0, 0)),
                  pl.BlockSpec((Q_TILE, LANES), qmap),
                  pl.BlockSpec((1, n_chunks, LANES, K_CHUNK), lambda b, h, i: (b, 0, h, 0)),
                  pl.BlockSpec((seq, LANES), lambda b, h, i: (b, h)),
                  pl.BlockSpec((Q_TILE, LANES), qmap),
                  pl.BlockSpec((1, BAND_CHUNKS, Q_TILE, K_CHUNK), lambda b, h, i: (h, 0, 0, 0))],
        out_specs=pl.BlockSpec((Q_TILE, LANES), qmap),
        out_shape=jax.ShapeDtypeStruct((m, A_WIDTH), jnp.bfloat16),
        scratch_shapes=scratch,
        compiler_params=_cparams(3),
        name="attn_diff",
    )(rel_bias, lam_params, subln_g, qa, kta, va, sa, band)


def _attn_gqa_call(qb, ktb, vb, sb, batch, seq):
    m = qb.shape[0]
    nq = seq // Q_TILE
    n_chunks = seq // K_CHUNK
    n_pairs = B_Q_HEADS // 2
    scratch = _attn_scratch()
    scratch[1] = pltpu.VMEM((n_chunks, 2 * Q_TILE, K_CHUNK), jnp.float32)
    qmap = lambda b, j, i: (b * nq + i, j)
    return pl.pallas_call(
        functools.partial(_attn_kernel, diff=False, lam_init=0.0, n_chunks=n_chunks),
        grid=(batch, n_pairs, nq),
        in_specs=[pl.BlockSpec((Q_TILE, LANES), qmap),
                  pl.BlockSpec((1, n_chunks, LANES, K_CHUNK), lambda b, j, i: (b, 0, 0, 0)),
                  pl.BlockSpec((seq, LANES), lambda b, j, i: (b, 0)),
                  pl.BlockSpec((Q_TILE, LANES), qmap)],
        out_specs=pl.BlockSpec((Q_TILE, LANES), qmap),
        out_shape=jax.ShapeDtypeStruct((m, B_WIDTH), jnp.bfloat16),
        scratch_shapes=scratch,
        compiler_params=_cparams(3),
        name="attn_gqa",
    )(qb, ktb, vb, sb)


def _out_kernel(ya_ref, yb_ref, wa_ref, wb_ref, x_ref, g_ref, o_ref):
    f32 = jnp.float32
    y = (jnp.dot(ya_ref[...], wa_ref[...], preferred_element_type=f32)
         + jnp.dot(yb_ref[...], wb_ref[...], preferred_element_type=f32))
    ms = jnp.mean(y * y, axis=-1, keepdims=True)
    o_ref[...] = x_ref[...] + y * lax.rsqrt(ms + EPS) * g_ref[...]


def _out_call(ya, yb, wa, wb, xf, g_post):
    m = xf.shape[0]
    row = lambda i: (i, 0)
    const = lambda i: (0, 0)
    return pl.pallas_call(
        _out_kernel,
        grid=(m // ROW_TILE,),
        in_specs=[pl.BlockSpec((ROW_TILE, A_WIDTH), row),
                  pl.BlockSpec((ROW_TILE, B_WIDTH), row),
                  pl.BlockSpec((A_WIDTH, D_MODEL), const),
                  pl.BlockSpec((B_WIDTH, D_MODEL), const),
                  pl.BlockSpec((ROW_TILE, D_MODEL), row),
                  pl.BlockSpec((1, D_MODEL), const)],
        out_specs=pl.BlockSpec((ROW_TILE, D_MODEL), row),
        out_shape=jax.ShapeDtypeStruct((m, D_MODEL), jnp.float32),
        compiler_params=_cparams(1),
        name="out_proj",
    )(ya, yb, wa, wb, xf, g_post)


def _layer_weights(w_in_l, w_out_l, q_norm_g_l, k_norm_g_l):
    bf16 = jnp.bfloat16
    eo = np.asarray(_EVEN_ODD)
    bq_cols = np.concatenate([_BQ + h * B_HEAD + eo for h in _HEAD_ORDER])
    bk_cols = np.concatenate([_BK + g * B_HEAD + eo for g in range(B_KV_HEADS)])
    bg_cols = np.concatenate([_BG + h * B_HEAD + np.arange(B_HEAD) for h in _HEAD_ORDER])
    ob_rows = np.concatenate([A_WIDTH + h * B_HEAD + np.arange(B_HEAD) for h in _HEAD_ORDER])
    wn = jnp.concatenate([w_in_l[:, _AQ:_AK], w_in_l[:, _AV:_AG], w_in_l[:, _AG:_BQ],
                          w_in_l[:, _BV:_BG], w_in_l[:, bg_cols]], axis=1).astype(bf16)
    wt = jnp.concatenate([w_in_l[:, _AK:_AV], w_in_l[:, bq_cols], w_in_l[:, bk_cols]],
                         axis=1).T.astype(bf16)
    wa = w_out_l[:A_WIDTH].astype(bf16)
    wb = w_out_l[ob_rows].astype(bf16)
    qg = q_norm_g_l[eo].reshape(B_HEAD, 1)
    kg = k_norm_g_l[eo].reshape(B_HEAD, 1)
    return wn, wt, wa, wb, qg, kg


def kernel(x, rel_bias, pre_norm_g, w_in, diff_lambda, diff_subln_g, q_norm_g, k_norm_g,
           w_out, post_norm_g):
    batch, seq, d_model = x.shape
    xf = x.reshape(batch * seq, d_model)
    band = _band_call(rel_bias)
    cos_t, sin_t = _rope_tables_t(seq)
    for l in range(DEPTH):
        lam_init = 0.8 - 0.6 * math.exp(-0.3 * l)
        wn, wt, wa, wb, qg, kg = _layer_weights(w_in[l], w_out[l], q_norm_g[l], k_norm_g[l])
        qa, va, sa, vb, sb, qb, kta, ktb = _proj_call(
            xf, pre_norm_g[l].reshape(1, d_model), wn, wt, cos_t, sin_t, qg, kg, batch, seq)
        ya = _attn_diff_call(rel_bias, diff_lambda[l], diff_subln_g[l].reshape(1, A_V_HEAD),
                             qa, kta, va, sa, band, lam_init, batch, seq)
        yb = _attn_gqa_call(qb, ktb, vb, sb, batch, seq)
        xf = _out_call(ya, yb, wa, wb, xf, post_norm_g[l].reshape(1, d_model))
    return xf.reshape(batch, seq, d_model)
```

```python
import functools
import math

import jax
import jax.numpy as jnp
import numpy as np
from jax import lax
from jax.experimental import pallas as pl
from jax.experimental.pallas import tpu as pltpu

D_MODEL = 1024
DEPTH = 2
A_WIDTH = 512
B_WIDTH = 512
A_HEADS = 4
A_V_HEAD = 128
A_QK_HEAD = 64
B_HEAD = 64
B_Q_HEADS = 8
B_KV_HEADS = 2
GRID_W = 64
ROPE_THETA = 10000.0
NUM_BUCKETS = 32
MAX_DISTANCE = 128
EPS = 1e-6
LOG2E = math.log2(math.e)
Q_SCALE = (A_QK_HEAD ** -0.5) * LOG2E

LANES = 128
ROW_TILE = 512
Q_TILE = 256
K_CHUNK = 256
BAND_CHUNKS = 3
VMEM_LIMIT = 48 * 1024 * 1024

_AQ, _AK, _AV, _AG, _BQ, _BK, _BV, _BG = 0, 512, 1024, 1536, 2048, 2560, 2688, 2816
_HEAD_ORDER = (0, 4, 1, 5, 2, 6, 3, 7)
_EVEN_ODD = tuple(range(0, B_HEAD, 2)) + tuple(range(1, B_HEAD, 2))


def _t5_bucket(rel):
    nb = NUM_BUCKETS // 2
    max_exact = nb // 2
    n = jnp.abs(rel)
    large = max_exact + (jnp.log(jnp.maximum(n, 1).astype(jnp.float32) / max_exact)
                         / math.log(MAX_DISTANCE / max_exact) * (nb - max_exact)).astype(jnp.int32)
    large = jnp.minimum(large, nb - 1)
    return jnp.where(rel > 0, nb, 0) + jnp.where(n < max_exact, n, large)


def _rope_tables_t(n):
    rows = n // GRID_W
    row = jnp.repeat(jnp.arange(rows, dtype=jnp.float32), GRID_W)
    col = jnp.tile(jnp.arange(GRID_W, dtype=jnp.float32), rows)
    axis_dim = B_HEAD // 2
    inv = ROPE_THETA ** (-jnp.arange(0, axis_dim, 2, dtype=jnp.float32) / axis_dim)
    ang = jnp.concatenate([row[:, None] * inv, col[:, None] * inv], axis=-1)
    return jnp.cos(ang).T, jnp.sin(ang).T


def _cparams(n_axes):
    return pltpu.CompilerParams(dimension_semantics=("arbitrary",) * n_axes,
                                vmem_limit_bytes=VMEM_LIMIT)


def _band_kernel(rb_ref, idx_ref, out_ref):
    h = pl.program_id(0)
    rows_per_step = 8
    steps_per_chunk = Q_TILE // rows_per_step

    def body(t, carry):
        d = t // steps_per_chunk
        r0 = pl.multiple_of((t % steps_per_chunk) * rows_per_step, rows_per_step)
        idx = idx_ref[d, pl.ds(r0, rows_per_step), :]
        acc = jnp.zeros(idx.shape, jnp.float32)
        for b in range(NUM_BUCKETS):
            acc = jnp.where(idx == b, rb_ref[b, h] * LOG2E, acc)
        out_ref[0, d, pl.ds(r0, rows_per_step), :] = acc
        return carry

    lax.fori_loop(0, BAND_CHUNKS * steps_per_chunk, body, 0)


def _band_call(rel_bias):
    d = jnp.arange(BAND_CHUNKS, dtype=jnp.int32)[:, None, None]
    r = jnp.arange(Q_TILE, dtype=jnp.int32)[None, :, None]
    j = jnp.arange(K_CHUNK, dtype=jnp.int32)[None, None, :]
    idx = _t5_bucket((d - 1) * K_CHUNK + j - r)
    return pl.pallas_call(
        _band_kernel,
        grid=(A_HEADS,),
        in_specs=[pl.BlockSpec(memory_space=pltpu.SMEM),
                  pl.BlockSpec((BAND_CHUNKS, Q_TILE, K_CHUNK), lambda h: (0, 0, 0))],
        out_specs=pl.BlockSpec((1, BAND_CHUNKS, Q_TILE, K_CHUNK), lambda h: (h, 0, 0, 0)),
        out_shape=jax.ShapeDtypeStruct((A_HEADS, BAND_CHUNKS, Q_TILE, K_CHUNK), jnp.float32),
        compiler_params=_cparams(1),
        name="bias_band",
    )(rel_bias, idx)


def _proj_kernel(x_ref, g_ref, wn_ref, wt_ref, cos_ref, sin_ref, qg_ref, kg_ref,
                 qa_ref, va_ref, sa_ref, vb_ref, sb_ref, qb_ref, kta_ref, ktb_ref):
    f32, bf16 = jnp.float32, jnp.bfloat16
    x = x_ref[...]
    ms = jnp.mean(x * x, axis=-1, keepdims=True)
    h = (x * lax.rsqrt(ms + EPS) * g_ref[...]).astype(bf16)

    def nat(lo, hi):
        return jnp.dot(h, wn_ref[:, lo:hi], preferred_element_type=f32)

    def tr(lo, hi):
        return lax.dot_general(wt_ref[lo:hi, :], h, (((1,), (1,)), ((), ())),
                               preferred_element_type=f32)

    def silu(v):
        return v * jax.nn.sigmoid(v)

    qa_ref[...] = (nat(0, 512) * Q_SCALE).astype(bf16)
    va_ref[...] = nat(512, 1024).astype(bf16)
    sa_ref[...] = silu(nat(1024, 1536)).astype(bf16)
    vb_ref[...] = nat(1536, 1664).astype(bf16)
    sb_ref[...] = silu(nat(1664, 2176)).astype(bf16)

    n_chunks = ROW_TILE // K_CHUNK
    kta = tr(0, 512)
    for j in range(n_chunks):
        kta_ref[0, j] = kta[:, j * K_CHUNK:(j + 1) * K_CHUNK].astype(bf16)

    cos = cos_ref[...]
    sin = sin_ref[...]
    half = B_HEAD // 2

    def norm_rope(xt, g):
        msq = jnp.mean(xt * xt, axis=0, keepdims=True)
        y = xt * lax.rsqrt(msq + EPS) * g
        e, o = y[:half], y[half:]
        return jnp.concatenate([e * cos - o * sin, e * sin + o * cos], axis=0)

    qg = qg_ref[...]
    qt = tr(512, 1024)
    qn = jnp.concatenate([norm_rope(qt[i * B_HEAD:(i + 1) * B_HEAD], qg)
                          for i in range(B_Q_HEADS)], axis=0) * Q_SCALE
    qb_ref[...] = qn.T.astype(bf16)

    kg = kg_ref[...]
    kt = tr(1024, 1152)
    kn = jnp.concatenate([norm_rope(kt[i * B_HEAD:(i + 1) * B_HEAD], kg)
                          for i in range(B_KV_HEADS)], axis=0)
    for j in range(n_chunks):
        ktb_ref[0, j] = kn[:, j * K_CHUNK:(j + 1) * K_CHUNK].astype(bf16)


def _proj_call(xf, g_pre, wn, wt, cos_t, sin_t, qg, kg, batch, seq):
    m = xf.shape[0]
    tiles_per_seq = seq // ROW_TILE
    chunks_per_tile = ROW_TILE // K_CHUNK
    n_chunks = seq // K_CHUNK
    bf16 = jnp.bfloat16
    row = lambda i: (i, 0)
    const = lambda i: (0, 0)
    ktmap = lambda i: (i // tiles_per_seq, i % tiles_per_seq, 0, 0)
    return pl.pallas_call(
        _proj_kernel,
        grid=(m // ROW_TILE,),
        in_specs=[pl.BlockSpec((ROW_TILE, D_MODEL), row),
                  pl.BlockSpec((1, D_MODEL), const),
                  pl.BlockSpec(wn.shape, const),
                  pl.BlockSpec(wt.shape, const),
                  pl.BlockSpec((B_HEAD // 2, ROW_TILE), lambda i: (0, i % tiles_per_seq)),
                  pl.BlockSpec((B_HEAD // 2, ROW_TILE), lambda i: (0, i % tiles_per_seq)),
                  pl.BlockSpec((B_HEAD, 1), const),
                  pl.BlockSpec((B_HEAD, 1), const)],
        out_specs=[pl.BlockSpec((ROW_TILE, A_WIDTH), row),
                   pl.BlockSpec((ROW_TILE, A_WIDTH), row),
                   pl.BlockSpec((ROW_TILE, A_WIDTH), row),
                   pl.BlockSpec((ROW_TILE, B_KV_HEADS * B_HEAD), row),
                   pl.BlockSpec((ROW_TILE, B_WIDTH), row),
                   pl.BlockSpec((ROW_TILE, B_WIDTH), row),
                   pl.BlockSpec((1, chunks_per_tile, A_WIDTH, K_CHUNK), ktmap),
                   pl.BlockSpec((1, chunks_per_tile, B_KV_HEADS * B_HEAD, K_CHUNK), ktmap)],
        out_shape=[jax.ShapeDtypeStruct((m, A_WIDTH), bf16),
                   jax.ShapeDtypeStruct((m, A_WIDTH), bf16),
                   jax.ShapeDtypeStruct((m, A_WIDTH), bf16),
                   jax.ShapeDtypeStruct((m, B_KV_HEADS * B_HEAD), bf16),
                   jax.ShapeDtypeStruct((m, B_WIDTH), bf16),
                   jax.ShapeDtypeStruct((m, B_WIDTH), bf16),
                   jax.ShapeDtypeStruct((batch, n_chunks, A_WIDTH, K_CHUNK), bf16),
                   jax.ShapeDtypeStruct((batch, n_chunks, B_KV_HEADS * B_HEAD, K_CHUNK), bf16)],
        compiler_params=_cparams(1),
        name="in_proj",
    )(xf, g_pre, wn, wt, cos_t, sin_t, qg, kg)


def _attn_kernel(*refs, diff, lam_init, n_chunks):
    f32, bf16 = jnp.float32, jnp.bfloat16
    if diff:
        (rb_ref, lamp_ref, g_ref, q_ref, kt_ref, v_ref, gate_ref, band_ref,
         o_ref, lhs_ref, s_ref, m_ref, vext_ref) = refs
    else:
        (q_ref, kt_ref, v_ref, gate_ref,
         o_ref, lhs_ref, s_ref, m_ref, vext_ref) = refs
    tq = Q_TILE
    qi = pl.program_id(2)

    @pl.when(qi == 0)
    def _():
        vext_ref[:, :LANES] = v_ref[...]
        vext_ref[:, LANES:] = jnp.ones((vext_ref.shape[0], LANES), bf16)

    low = lax.broadcasted_iota(jnp.int32, (tq, LANES), 1) < (LANES // 2)
    q = q_ref[...].astype(f32)
    lhs_ref[0:tq, :] = jnp.where(low, q, 0.0).astype(bf16)
    lhs_ref[tq:2 * tq, :] = jnp.where(low, 0.0, q).astype(bf16)

    if diff:
        head = pl.program_id(1)
        c_left = rb_ref[NUM_BUCKETS // 2 - 1, head] * LOG2E
        c_right = rb_ref[NUM_BUCKETS - 1, head] * LOG2E

    halves = (slice(0, tq), slice(tq, 2 * tq))

    for c in range(n_chunks):
        if diff:
            d = c - qi + 1
            in_band = jnp.logical_and(d >= 0, d < BAND_CHUNKS)
            band = band_ref[0, jnp.clip(d, 0, BAND_CHUNKS - 1)]
            bias = jnp.where(in_band, band, jnp.where(d < 0, c_left, c_right))
        for rows in halves:
            s = jnp.dot(lhs_ref[rows, :], kt_ref[0, c], preferred_element_type=f32)
            if diff:
                s = s + bias
            s_ref[c, rows, :] = s
            cmax = jnp.maximum(s[:, :LANES], s[:, LANES:])
            m_ref[rows, :] = cmax if c == 0 else jnp.maximum(m_ref[rows, :], cmax)

    m = jnp.max(m_ref[...], axis=-1, keepdims=True)
    m_ref[...] = jnp.broadcast_to(m, m_ref.shape)

    accs = [None, None]
    for c in range(n_chunks):
        for i, rows in enumerate(halves):
            s = s_ref[c, rows, :]
            mb = m_ref[rows, :]
            p = jnp.concatenate([jnp.exp2(s[:, :LANES] - mb), jnp.exp2(s[:, LANES:] - mb)],
                                axis=1).astype(bf16)
            part = jnp.dot(p, vext_ref[c * K_CHUNK:(c + 1) * K_CHUNK, :],
                           preferred_element_type=f32)
            accs[i] = part if accs[i] is None else accs[i] + part

    o_lo = accs[0][:, :LANES] / accs[0][:, LANES:]
    o_hi = accs[1][:, :LANES] / accs[1][:, LANES:]
    gate = gate_ref[...].astype(f32)
    if diff:
        lp = lamp_ref[...]
        lam = (jnp.exp(jnp.sum(lp[0:1] * lp[1:2], axis=-1, keepdims=True))
               - jnp.exp(jnp.sum(lp[2:3] * lp[3:4], axis=-1, keepdims=True)) + lam_init)
        o = o_lo - lam * o_hi
        ms = jnp.mean(o * o, axis=-1, keepdims=True)
        o = (o * lax.rsqrt(ms + EPS) * g_ref[...]) * (1.0 - lam_init)
    else:
        o = jnp.where(low, o_lo, o_hi)
    o_ref[...] = (o * gate).astype(o_ref.dtype)


def _attn_scratch(seq):
    n_chunks = seq // K_CHUNK
    return [pltpu.VMEM((2 * Q_TILE, LANES), jnp.bfloat16),
            pltpu.VMEM((n_chunks, 2 * Q_TILE, K_CHUNK), jnp.float32),
            pltpu.VMEM((2 * Q_TILE, LANES), jnp.float32),
            pltpu.VMEM((seq, 2 * LANES), jnp.bfloat16)]


def _attn_diff_call(rel_bias, lam_params, subln_g, qa, kta, va, sa, band, lam_init, batch, seq):
    m = qa.shape[0]
    nq = seq // Q_TILE
    n_chunks = seq // K_CHUNK
    scratch = _attn_scratch(seq)
    qmap = lambda b, h, i: (b * nq + i, h)
    return pl.pallas_call(
        functools.partial(_attn_kernel, diff=True, lam_init=lam_init, n_chunks=n_chunks),
        grid=(batch, A_HEADS, nq),
        in_specs=[pl.BlockSpec(memory_space=pltpu.SMEM),
                  pl.BlockSpec((4, A_QK_HEAD), lambda b, h, i: (0, 0)),
                  pl.BlockSpec((1, A_V_HEAD), lambda b, h, i: (0, 0)),
                  pl.BlockSpec((Q_TILE, LANES), qmap),
                  pl.BlockSpec((1, n_chunks, LANES, K_CHUNK), lambda b, h, i: (b, 0, h, 0)),
                  pl.BlockSpec((seq, LANES), lambda b, h, i: (b, h)),
                  pl.BlockSpec((Q_TILE, LANES), qmap),
                  pl.BlockSpec((1, BAND_CHUNKS, Q_TILE, K_CHUNK), lambda b, h, i: (h, 0, 0, 0))],
        out_specs=pl.BlockSpec((Q_TILE, LANES), qmap),
        out_shape=jax.ShapeDtypeStruct((m, A_WIDTH), jnp.bfloat16),
        scratch_shapes=scratch,
        compiler_params=_cparams(3),
        name="attn_diff",
    )(rel_bias, lam_params, subln_g, qa, kta, va, sa, band)


def _attn_gqa_call(qb, ktb, vb, sb, batch, seq):
    m = qb.shape[0]
    nq = seq // Q_TILE
    n_chunks = seq // K_CHUNK
    n_pairs = B_Q_HEADS // 2
    scratch = _attn_scratch(seq)
    qmap = lambda b, j, i: (b * nq + i, j)
    return pl.pallas_call(
        functools.partial(_attn_kernel, diff=False, lam_init=0.0, n_chunks=n_chunks),
        grid=(batch, n_pairs, nq),
        in_specs=[pl.BlockSpec((Q_TILE, LANES), qmap),
                  pl.BlockSpec((1, n_chunks, LANES, K_CHUNK), lambda b, j, i: (b, 0, 0, 0)),
                  pl.BlockSpec((seq, LANES), lambda b, j, i: (b, 0)),
                  pl.BlockSpec((Q_TILE, LANES), qmap)],
        out_specs=pl.BlockSpec((Q_TILE, LANES), qmap),
        out_shape=jax.ShapeDtypeStruct((m, B_WIDTH), jnp.bfloat16),
        scratch_shapes=scratch,
        compiler_params=_cparams(3),
        name="attn_gqa",
    )(qb, ktb, vb, sb)


def _out_kernel(ya_ref, yb_ref, wa_ref, wb_ref, x_ref, g_ref, o_ref):
    f32 = jnp.float32
    y = (jnp.dot(ya_ref[...], wa_ref[...], preferred_element_type=f32)
         + jnp.dot(yb_ref[...], wb_ref[...], preferred_element_type=f32))
    ms = jnp.mean(y * y, axis=-1, keepdims=True)
    o_ref[...] = x_ref[...] + y * lax.rsqrt(ms + EPS) * g_ref[...]


def _out_call(ya, yb, wa, wb, xf, g_post):
    m = xf.shape[0]
    row = lambda i: (i, 0)
    const = lambda i: (0, 0)
    return pl.pallas_call(
        _out_kernel,
        grid=(m // ROW_TILE,),
        in_specs=[pl.BlockSpec((ROW_TILE, A_WIDTH), row),
                  pl.BlockSpec((ROW_TILE, B_WIDTH), row),
                  pl.BlockSpec((A_WIDTH, D_MODEL), const),
                  pl.BlockSpec((B_WIDTH, D_MODEL), const),
                  pl.BlockSpec((ROW_TILE, D_MODEL), row),
                  pl.BlockSpec((1, D_MODEL), const)],
        out_specs=pl.BlockSpec((ROW_TILE, D_MODEL), row),
        out_shape=jax.ShapeDtypeStruct((m, D_MODEL), jnp.float32),
        compiler_params=_cparams(1),
        name="out_proj",
    )(ya, yb, wa, wb, xf, g_post)


def _layer_weights(w_in_l, w_out_l, q_norm_g_l, k_norm_g_l):
    bf16 = jnp.bfloat16
    eo = np.asarray(_EVEN_ODD)
    bq_cols = np.concatenate([_BQ + h * B_HEAD + eo for h in _HEAD_ORDER])
    bk_cols = np.concatenate([_BK + g * B_HEAD + eo for g in range(B_KV_HEADS)])
    bg_cols = np.concatenate([_BG + h * B_HEAD + np.arange(B_HEAD) for h in _HEAD_ORDER])
    ob_rows = np.concatenate([A_WIDTH + h * B_HEAD + np.arange(B_HEAD) for h in _HEAD_ORDER])
    wn = jnp.concatenate([w_in_l[:, _AQ:_AK], w_in_l[:, _AV:_AG], w_in_l[:, _AG:_BQ],
                          w_in_l[:, _BV:_BG], w_in_l[:, bg_cols]], axis=1).astype(bf16)
    wt = jnp.concatenate([w_in_l[:, _AK:_AV], w_in_l[:, bq_cols], w_in_l[:, bk_cols]],
                         axis=1).T.astype(bf16)
    wa = w_out_l[:A_WIDTH].astype(bf16)
    wb = w_out_l[ob_rows].astype(bf16)
    qg = q_norm_g_l[eo].reshape(B_HEAD, 1)
    kg = k_norm_g_l[eo].reshape(B_HEAD, 1)
    return wn, wt, wa, wb, qg, kg


def kernel(x, rel_bias, pre_norm_g, w_in, diff_lambda, diff_subln_g, q_norm_g, k_norm_g,
           w_out, post_norm_g):
    batch, seq, d_model = x.shape
    xf = x.reshape(batch * seq, d_model)
    band = _band_call(rel_bias)
    cos_t, sin_t = _rope_tables_t(seq)
    for l in range(DEPTH):
        lam_init = 0.8 - 0.6 * math.exp(-0.3 * l)
        wn, wt, wa, wb, qg, kg = _layer_weights(w_in[l], w_out[l], q_norm_g[l], k_norm_g[l])
        qa, va, sa, vb, sb, qb, kta, ktb = _proj_call(
            xf, pre_norm_g[l].reshape(1, d_model), wn, wt, cos_t, sin_t, qg, kg, batch, seq)
        ya = _attn_diff_call(rel_bias, diff_lambda[l], diff_subln_g[l].reshape(1, A_V_HEAD),
                             qa, kta, va, sa, band, lam_init, batch, seq)
        yb = _attn_gqa_call(qb, ktb, vb, sb, batch, seq)
        xf = _out_call(ya, yb, wa, wb, xf, post_norm_g[l].reshape(1, d_model))
    return xf.reshape(batch, seq, d_model)
```

```python
import functools
import math

import jax
import jax.numpy as jnp
import numpy as np
from jax import lax
from jax.experimental import pallas as pl
from jax.experimental.pallas import tpu as pltpu

D_MODEL = 1024
DEPTH = 2
A_WIDTH = 512
B_WIDTH = 512
A_HEADS = 4
A_V_HEAD = 128
A_QK_HEAD = 64
B_HEAD = 64
B_Q_HEADS = 8
B_KV_HEADS = 2
GRID_W = 64
ROPE_THETA = 10000.0
NUM_BUCKETS = 32
MAX_DISTANCE = 128
EPS = 1e-6
LOG2E = math.log2(math.e)
Q_SCALE = (A_QK_HEAD ** -0.5) * LOG2E

LANES = 128
ROW_TILE = 512
Q_TILE = 256
K_CHUNK = 256
BAND_CHUNKS = 3
VMEM_LIMIT = 48 * 1024 * 1024

_AQ, _AK, _AV, _AG, _BQ, _BK, _BV, _BG = 0, 512, 1024, 1536, 2048, 2560, 2688, 2816
_HEAD_ORDER = (0, 4, 1, 5, 2, 6, 3, 7)
_EVEN_ODD = tuple(range(0, B_HEAD, 2)) + tuple(range(1, B_HEAD, 2))


def _t5_bucket(rel):
    nb = NUM_BUCKETS // 2
    max_exact = nb // 2
    n = jnp.abs(rel)
    large = max_exact + (jnp.log(jnp.maximum(n, 1).astype(jnp.float32) / max_exact)
                         / math.log(MAX_DISTANCE / max_exact) * (nb - max_exact)).astype(jnp.int32)
    large = jnp.minimum(large, nb - 1)
    return jnp.where(rel > 0, nb, 0) + jnp.where(n < max_exact, n, large)


def _rope_tables_t(n):
    rows = n // GRID_W
    row = jnp.repeat(jnp.arange(rows, dtype=jnp.float32), GRID_W)
    col = jnp.tile(jnp.arange(GRID_W, dtype=jnp.float32), rows)
    axis_dim = B_HEAD // 2
    inv = ROPE_THETA ** (-jnp.arange(0, axis_dim, 2, dtype=jnp.float32) / axis_dim)
    ang = jnp.concatenate([row[:, None] * inv, col[:, None] * inv], axis=-1)
    return jnp.cos(ang).T, jnp.sin(ang).T


def _cparams(n_axes):
    return pltpu.CompilerParams(dimension_semantics=("arbitrary",) * n_axes,
                                vmem_limit_bytes=VMEM_LIMIT)


def _band_kernel(rb_ref, idx_ref, out_ref):
    h = pl.program_id(0)
    rows_per_step = 8
    steps_per_chunk = Q_TILE // rows_per_step

    def body(t, carry):
        d = t // steps_per_chunk
        r0 = pl.multiple_of((t % steps_per_chunk) * rows_per_step, rows_per_step)
        idx = idx_ref[d, pl.ds(r0, rows_per_step), :]
        acc = jnp.zeros(idx.shape, jnp.float32)
        for b in range(NUM_BUCKETS):
            acc = jnp.where(idx == b, rb_ref[b, h] * LOG2E, acc)
        out_ref[0, d, pl.ds(r0, rows_per_step), :] = acc
        return carry

    lax.fori_loop(0, BAND_CHUNKS * steps_per_chunk, body, 0)


def _band_call(rel_bias):
    d = jnp.arange(BAND_CHUNKS, dtype=jnp.int32)[:, None, None]
    r = jnp.arange(Q_TILE, dtype=jnp.int32)[None, :, None]
    j = jnp.arange(K_CHUNK, dtype=jnp.int32)[None, None, :]
    idx = _t5_bucket((d - 1) * K_CHUNK + j - r)
    return pl.pallas_call(
        _band_kernel,
        grid=(A_HEADS,),
        in_specs=[pl.BlockSpec(memory_space=pltpu.SMEM),
                  pl.BlockSpec((BAND_CHUNKS, Q_TILE, K_CHUNK), lambda h: (0, 0, 0))],
        out_specs=pl.BlockSpec((1, BAND_CHUNKS, Q_TILE, K_CHUNK), lambda h: (h, 0, 0, 0)),
        out_shape=jax.ShapeDtypeStruct((A_HEADS, BAND_CHUNKS, Q_TILE, K_CHUNK), jnp.float32),
        compiler_params=_cparams(1),
        name="bias_band",
    )(rel_bias, idx)


def _proj_kernel(x_ref, g_ref, wn_ref, wt_ref, cos_ref, sin_ref, qg_ref, kg_ref,
                 qa_ref, va_ref, sa_ref, vb_ref, sb_ref, qb_ref, kta_ref, ktb_ref):
    f32, bf16 = jnp.float32, jnp.bfloat16
    x = x_ref[...]
    ms = jnp.mean(x * x, axis=-1, keepdims=True)
    h = (x * lax.rsqrt(ms + EPS) * g_ref[...]).astype(bf16)

    def nat(lo, hi):
        return jnp.dot(h, wn_ref[:, lo:hi], preferred_element_type=f32)

    def tr(lo, hi):
        return lax.dot_general(wt_ref[lo:hi, :], h, (((1,), (1,)), ((), ())),
                               preferred_element_type=f32)

    def silu(v):
        return v * jax.nn.sigmoid(v)

    qa_ref[...] = (nat(0, 512) * Q_SCALE).astype(bf16)
    ones = jnp.ones((ROW_TILE, LANES), bf16)
    va = nat(512, 1024).astype(bf16)
    for hd in range(A_HEADS):
        va_ref[:, 2 * hd * LANES:(2 * hd + 1) * LANES] = va[:, hd * LANES:(hd + 1) * LANES]
        va_ref[:, (2 * hd + 1) * LANES:(2 * hd + 2) * LANES] = ones
    sa_ref[...] = silu(nat(1024, 1536)).astype(bf16)
    vb_ref[:, :LANES] = nat(1536, 1664).astype(bf16)
    vb_ref[:, LANES:] = ones
    sb_ref[...] = silu(nat(1664, 2176)).astype(bf16)

    n_chunks = ROW_TILE // K_CHUNK
    kta = tr(0, 512)
    for j in range(n_chunks):
        kta_ref[0, j] = kta[:, j * K_CHUNK:(j + 1) * K_CHUNK].astype(bf16)

    cos = cos_ref[...]
    sin = sin_ref[...]
    half = B_HEAD // 2

    def norm_rope(xt, g):
        msq = jnp.mean(xt * xt, axis=0, keepdims=True)
        y = xt * lax.rsqrt(msq + EPS) * g
        e, o = y[:half], y[half:]
        return jnp.concatenate([e * cos - o * sin, e * sin + o * cos], axis=0)

    qg = qg_ref[...]
    qt = tr(512, 1024)
    qn = jnp.concatenate([norm_rope(qt[i * B_HEAD:(i + 1) * B_HEAD], qg)
                          for i in range(B_Q_HEADS)], axis=0) * Q_SCALE
    qb_ref[...] = qn.T.astype(bf16)

    kg = kg_ref[...]
    kt = tr(1024, 1152)
    kn = jnp.concatenate([norm_rope(kt[i * B_HEAD:(i + 1) * B_HEAD], kg)
                          for i in range(B_KV_HEADS)], axis=0)
    for j in range(n_chunks):
        ktb_ref[0, j] = kn[:, j * K_CHUNK:(j + 1) * K_CHUNK].astype(bf16)


def _proj_call(xf, g_pre, wn, wt, cos_t, sin_t, qg, kg, batch, seq):
    m = xf.shape[0]
    tiles_per_seq = seq // ROW_TILE
    chunks_per_tile = ROW_TILE // K_CHUNK
    n_chunks = seq // K_CHUNK
    bf16 = jnp.bfloat16
    row = lambda i: (i, 0)
    const = lambda i: (0, 0)
    ktmap = lambda i: (i // tiles_per_seq, i % tiles_per_seq, 0, 0)
    return pl.pallas_call(
        _proj_kernel,
        grid=(m // ROW_TILE,),
        in_specs=[pl.BlockSpec((ROW_TILE, D_MODEL), row),
                  pl.BlockSpec((1, D_MODEL), const),
                  pl.BlockSpec(wn.shape, const),
                  pl.BlockSpec(wt.shape, const),
                  pl.BlockSpec((B_HEAD // 2, ROW_TILE), lambda i: (0, i % tiles_per_seq)),
                  pl.BlockSpec((B_HEAD // 2, ROW_TILE), lambda i: (0, i % tiles_per_seq)),
                  pl.BlockSpec((B_HEAD, 1), const),
                  pl.BlockSpec((B_HEAD, 1), const)],
        out_specs=[pl.BlockSpec((ROW_TILE, A_WIDTH), row),
                   pl.BlockSpec((ROW_TILE, 2 * A_WIDTH), row),
                   pl.BlockSpec((ROW_TILE, A_WIDTH), row),
                   pl.BlockSpec((ROW_TILE, 2 * LANES), row),
                   pl.BlockSpec((ROW_TILE, B_WIDTH), row),
                   pl.BlockSpec((ROW_TILE, B_WIDTH), row),
                   pl.BlockSpec((1, chunks_per_tile, A_WIDTH, K_CHUNK), ktmap),
                   pl.BlockSpec((1, chunks_per_tile, B_KV_HEADS * B_HEAD, K_CHUNK), ktmap)],
        out_shape=[jax.ShapeDtypeStruct((m, A_WIDTH), bf16),
                   jax.ShapeDtypeStruct((m, 2 * A_WIDTH), bf16),
                   jax.ShapeDtypeStruct((m, A_WIDTH), bf16),
                   jax.ShapeDtypeStruct((m, 2 * LANES), bf16),
                   jax.ShapeDtypeStruct((m, B_WIDTH), bf16),
                   jax.ShapeDtypeStruct((m, B_WIDTH), bf16),
                   jax.ShapeDtypeStruct((batch, n_chunks, A_WIDTH, K_CHUNK), bf16),
                   jax.ShapeDtypeStruct((batch, n_chunks, B_KV_HEADS * B_HEAD, K_CHUNK), bf16)],
        compiler_params=_cparams(1),
        name="in_proj",
    )(xf, g_pre, wn, wt, cos_t, sin_t, qg, kg)


def _attn_kernel(*refs, diff, lam_init, n_chunks, nq, n_pairs, n_tiles):
    f32, bf16 = jnp.float32, jnp.bfloat16
    if diff:
        (rb_ref, lamp_ref, g_ref, q_ref, kt_ref, v_ref, gate_ref, band_ref,
         o_ref, lhs_ref, s_even, s_odd, m_even, m_odd) = refs
    else:
        (q_ref, kt_ref, v_ref, gate_ref,
         o_ref, lhs_ref, s_even, s_odd, m_even, m_odd) = refs
    tq = Q_TILE
    halves = (slice(0, tq), slice(tq, 2 * tq))
    t = pl.program_id(0)
    tile1 = jnp.minimum(t, n_tiles - 1)
    tile2 = jnp.maximum(t - 1, 0)
    qi1 = tile1 % nq
    qi2 = tile2 % nq

    @pl.when(t == 0)
    def _():
        s_odd[...] = jnp.zeros(s_odd.shape, f32)
        m_odd[...] = jnp.zeros(m_odd.shape, f32)

    def step(s1_ref, m1_ref, s2_ref, m2_ref):
        low = lax.broadcasted_iota(jnp.int32, (tq, LANES), 1) < (LANES // 2)
        q = q_ref[...].astype(f32)
        lhs_ref[0:tq, :] = jnp.where(low, q, 0.0).astype(bf16)
        lhs_ref[tq:2 * tq, :] = jnp.where(low, 0.0, q).astype(bf16)

        m = jnp.max(m2_ref[...], axis=-1, keepdims=True)
        m2_ref[...] = jnp.broadcast_to(m, m2_ref.shape)

        if diff:
            head1 = (tile1 // nq) % n_pairs
            head2 = (tile2 // nq) % n_pairs
            left1 = rb_ref[NUM_BUCKETS // 2 - 1, head1] * LOG2E
            right1 = rb_ref[NUM_BUCKETS - 1, head1] * LOG2E
            left2 = rb_ref[NUM_BUCKETS // 2 - 1, head2] * LOG2E
            right2 = rb_ref[NUM_BUCKETS - 1, head2] * LOG2E

        def chunk_of(qi, i):
            if not diff:
                return i, None
            c = qi - 1 + i
            return jnp.bitwise_and(c, n_chunks - 1), c

        accs = [None, None]
        for i in range(n_chunks):
            j1, c1 = chunk_of(qi1, i)
            if diff:
                in_range1 = jnp.logical_and(c1 >= 0, c1 < n_chunks)
                if i < BAND_CHUNKS:
                    bias1 = jnp.where(in_range1, band_ref[0, i],
                                      jnp.where(c1 < 0, right1, left1))
                else:
                    const1 = jnp.where(in_range1, right1, left1)
            for rows in halves:
                s = jnp.dot(lhs_ref[rows, :], kt_ref[0, j1], preferred_element_type=f32)
                if diff and i < BAND_CHUNKS:
                    s = s + bias1
                s1_ref[j1, rows, :] = s
                cmax = jnp.maximum(s[:, :LANES], s[:, LANES:])
                if diff and i >= BAND_CHUNKS:
                    cmax = cmax + const1
                m1_ref[rows, :] = cmax if i == 0 else jnp.maximum(m1_ref[rows, :], cmax)

            j2, c2 = chunk_of(qi2, i)
            if diff:
                k0 = pl.multiple_of(j2 * K_CHUNK, K_CHUNK)
                if i >= BAND_CHUNKS:
                    const2 = jnp.where(jnp.logical_and(c2 >= 0, c2 < n_chunks), right2, left2)
            else:
                k0 = j2 * K_CHUNK
            for h_idx, rows in enumerate(halves):
                s = s2_ref[j2, rows, :]
                mb = m2_ref[rows, :]
                if diff and i >= BAND_CHUNKS:
                    mb = mb - const2
                p = jnp.concatenate([jnp.exp2(s[:, :LANES] - mb), jnp.exp2(s[:, LANES:] - mb)],
                                    axis=1).astype(bf16)
                part = jnp.dot(p, v_ref[pl.ds(k0, K_CHUNK), :], preferred_element_type=f32)
                accs[h_idx] = part if accs[h_idx] is None else accs[h_idx] + part

        o_lo = accs[0][:, :LANES] / accs[0][:, LANES:]
        o_hi = accs[1][:, :LANES] / accs[1][:, LANES:]
        gate = gate_ref[...].astype(f32)
        if diff:
            lp = lamp_ref[...]
            lam = (jnp.exp(jnp.sum(lp[0:1] * lp[1:2], axis=-1, keepdims=True))
                   - jnp.exp(jnp.sum(lp[2:3] * lp[3:4], axis=-1, keepdims=True)) + lam_init)
            o = o_lo - lam * o_hi
            ms = jnp.mean(o * o, axis=-1, keepdims=True)
            o = (o * lax.rsqrt(ms + EPS) * g_ref[...]) * (1.0 - lam_init)
        else:
            o = jnp.where(low, o_lo, o_hi)
        o_ref[...] = (o * gate).astype(o_ref.dtype)

    @pl.when(t % 2 == 0)
    def _():
        step(s_even, m_even, s_odd, m_odd)

    @pl.when(t % 2 == 1)
    def _():
        step(s_odd, m_odd, s_even, m_even)


def _attn_call(operands, prefix_specs, q_arr, kt_arr, v_arr, gate_arr, band_arr, *, diff,
               lam_init, n_pairs, kt_per_pair, batch, seq, name):
    m = q_arr.shape[0]
    nq = seq // Q_TILE
    n_chunks = seq // K_CHUNK
    n_tiles = batch * n_pairs * nq

    def decode(tile):
        return tile // (n_pairs * nq), (tile // nq) % n_pairs, tile % nq

    def stage1(t):
        return decode(jnp.minimum(t, n_tiles - 1))

    def stage2(t):
        return decode(jnp.maximum(t - 1, 0))

    def q_map(t):
        b, p, i = stage1(t)
        return b * nq + i, p

    def kt_map(t):
        b, p, _ = stage1(t)
        return b, 0, p if kt_per_pair else 0, 0

    def v_map(t):
        b, p, _ = stage2(t)
        return b, p if kt_per_pair else 0

    def out_map(t):
        b, p, i = stage2(t)
        return b * nq + i, p

    in_specs = list(prefix_specs) + [
        pl.BlockSpec((Q_TILE, LANES), q_map),
        pl.BlockSpec((1, n_chunks, LANES, K_CHUNK), kt_map),
        pl.BlockSpec((seq, 2 * LANES), v_map),
        pl.BlockSpec((Q_TILE, LANES), out_map)]
    tail = ()
    if band_arr is not None:
        in_specs.append(pl.BlockSpec((1, BAND_CHUNKS, Q_TILE, K_CHUNK),
                                     lambda t: (stage1(t)[1], 0, 0, 0)))
        tail = (band_arr,)
    scratch = [pltpu.VMEM((2 * Q_TILE, LANES), jnp.bfloat16),
               pltpu.VMEM((n_chunks, 2 * Q_TILE, K_CHUNK), jnp.float32),
               pltpu.VMEM((n_chunks, 2 * Q_TILE, K_CHUNK), jnp.float32),
               pltpu.VMEM((2 * Q_TILE, LANES), jnp.float32),
               pltpu.VMEM((2 * Q_TILE, LANES), jnp.float32)]
    return pl.pallas_call(
        functools.partial(_attn_kernel, diff=diff, lam_init=lam_init, n_chunks=n_chunks,
                          nq=nq, n_pairs=n_pairs, n_tiles=n_tiles),
        grid=(n_tiles + 1,),
        in_specs=in_specs,
        out_specs=pl.BlockSpec((Q_TILE, LANES), out_map),
        out_shape=jax.ShapeDtypeStruct((m, n_pairs * LANES), jnp.bfloat16),
        scratch_shapes=scratch,
        compiler_params=_cparams(1),
        name=name,
    )(*operands, q_arr, kt_arr, v_arr, gate_arr, *tail)


def _attn_diff_call(rel_bias, lam_params, subln_g, qa, kta, va, sa, band, lam_init, batch, seq):
    prefix = [pl.BlockSpec(memory_space=pltpu.SMEM),
              pl.BlockSpec((4, A_QK_HEAD), lambda t: (0, 0)),
              pl.BlockSpec((1, A_V_HEAD), lambda t: (0, 0))]
    return _attn_call((rel_bias, lam_params, subln_g), prefix, qa, kta, va, sa, band,
                      diff=True, lam_init=lam_init, n_pairs=A_HEADS, kt_per_pair=True,
                      batch=batch, seq=seq, name="attn_diff")


def _attn_gqa_call(qb, ktb, vb, sb, batch, seq):
    return _attn_call((), [], qb, ktb, vb, sb, None, diff=False, lam_init=0.0,
                      n_pairs=B_Q_HEADS // 2, kt_per_pair=False, batch=batch, seq=seq,
                      name="attn_gqa")


def _out_kernel(ya_ref, yb_ref, wa_ref, wb_ref, x_ref, g_ref, o_ref):
    f32 = jnp.float32
    y = (jnp.dot(ya_ref[...], wa_ref[...], preferred_element_type=f32)
         + jnp.dot(yb_ref[...], wb_ref[...], preferred_element_type=f32))
    ms = jnp.mean(y * y, axis=-1, keepdims=True)
    o_ref[...] = x_ref[...] + y * lax.rsqrt(ms + EPS) * g_ref[...]


def _out_call(ya, yb, wa, wb, xf, g_post):
    m = xf.shape[0]
    row = lambda i: (i, 0)
    const = lambda i: (0, 0)
    return pl.pallas_call(
        _out_kernel,
        grid=(m // ROW_TILE,),
        in_specs=[pl.BlockSpec((ROW_TILE, A_WIDTH), row),
                  pl.BlockSpec((ROW_TILE, B_WIDTH), row),
                  pl.BlockSpec((A_WIDTH, D_MODEL), const),
                  pl.BlockSpec((B_WIDTH, D_MODEL), const),
                  pl.BlockSpec((ROW_TILE, D_MODEL), row),
                  pl.BlockSpec((1, D_MODEL), const)],
        out_specs=pl.BlockSpec((ROW_TILE, D_MODEL), row),
        out_shape=jax.ShapeDtypeStruct((m, D_MODEL), jnp.float32),
        compiler_params=_cparams(1),
        name="out_proj",
    )(ya, yb, wa, wb, xf, g_post)


def _layer_weights(w_in_l, w_out_l, q_norm_g_l, k_norm_g_l):
    bf16 = jnp.bfloat16
    eo = np.asarray(_EVEN_ODD)
    bq_cols = np.concatenate([_BQ + h * B_HEAD + eo for h in _HEAD_ORDER])
    bk_cols = np.concatenate([_BK + g * B_HEAD + eo for g in range(B_KV_HEADS)])
    bg_cols = np.concatenate([_BG + h * B_HEAD + np.arange(B_HEAD) for h in _HEAD_ORDER])
    ob_rows = np.concatenate([A_WIDTH + h * B_HEAD + np.arange(B_HEAD) for h in _HEAD_ORDER])
    wn = jnp.concatenate([w_in_l[:, _AQ:_AK], w_in_l[:, _AV:_AG], w_in_l[:, _AG:_BQ],
                          w_in_l[:, _BV:_BG], w_in_l[:, bg_cols]], axis=1).astype(bf16)
    wt = jnp.concatenate([w_in_l[:, _AK:_AV], w_in_l[:, bq_cols], w_in_l[:, bk_cols]],
                         axis=1).T.astype(bf16)
    wa = w_out_l[:A_WIDTH].astype(bf16)
    wb = w_out_l[ob_rows].astype(bf16)
    qg = q_norm_g_l[eo].reshape(B_HEAD, 1)
    kg = k_norm_g_l[eo].reshape(B_HEAD, 1)
    return wn, wt, wa, wb, qg, kg


def kernel(x, rel_bias, pre_norm_g, w_in, diff_lambda, diff_subln_g, q_norm_g, k_norm_g,
           w_out, post_norm_g):
    batch, seq, d_model = x.shape
    xf = x.reshape(batch * seq, d_model)
    band = _band_call(rel_bias)
    cos_t, sin_t = _rope_tables_t(seq)
    for l in range(DEPTH):
        lam_init = 0.8 - 0.6 * math.exp(-0.3 * l)
        wn, wt, wa, wb, qg, kg = _layer_weights(w_in[l], w_out[l], q_norm_g[l], k_norm_g[l])
        qa, va, sa, vb, sb, qb, kta, ktb = _proj_call(
            xf, pre_norm_g[l].reshape(1, d_model), wn, wt, cos_t, sin_t, qg, kg, batch, seq)
        ya = _attn_diff_call(rel_bias, diff_lambda[l], diff_subln_g[l].reshape(1, A_V_HEAD),
                             qa, kta, va, sa, band, lam_init, batch, seq)
        yb = _attn_gqa_call(qb, ktb, vb, sb, batch, seq)
        xf = _out_call(ya, yb, wa, wb, xf, post_norm_g[l].reshape(1, d_model))
    return xf.reshape(batch, seq, d_model)
```

```python
import functools
import math

import jax
import jax.numpy as jnp
from jax import lax
from jax.experimental import pallas as pl
from jax.experimental.pallas import tpu as pltpu

D_MODEL = 1024
DEPTH = 2
A_WIDTH = 512
B_WIDTH = 512
A_HEADS = 4
A_V_HEAD = 128
A_QK_HEAD = 64
B_HEAD = 64
B_Q_HEADS = 8
B_KV_HEADS = 2
GRID_W = 64
ROPE_THETA = 10000.0
NUM_BUCKETS = 32
MAX_DISTANCE = 128
EPS = 1e-6
LOG2E = math.log2(math.e)
Q_SCALE = (A_QK_HEAD ** -0.5) * LOG2E

LANES = 128
ROW_TILE = 512
Q_TILE = 256
K_CHUNK = 256
BAND_CHUNKS = 3
VMEM_LIMIT = 48 * 1024 * 1024

_AQ, _AK, _AV, _AG, _BQ, _BK, _BV, _BG = 0, 512, 1024, 1536, 2048, 2560, 2688, 2816


def _t5_bucket(rel):
    nb = NUM_BUCKETS // 2
    max_exact = nb // 2
    n = jnp.abs(rel)
    large = max_exact + (jnp.log(jnp.maximum(n, 1).astype(jnp.float32) / max_exact)
                         / math.log(MAX_DISTANCE / max_exact) * (nb - max_exact)).astype(jnp.int32)
    large = jnp.minimum(large, nb - 1)
    return jnp.where(rel > 0, nb, 0) + jnp.where(n < max_exact, n, large)


def _rope_tables_t(n):
    rows = n // GRID_W
    row = jnp.repeat(jnp.arange(rows, dtype=jnp.float32), GRID_W)
    col = jnp.tile(jnp.arange(GRID_W, dtype=jnp.float32), rows)
    axis_dim = B_HEAD // 2
    inv = ROPE_THETA ** (-jnp.arange(0, axis_dim, 2, dtype=jnp.float32) / axis_dim)
    ang = jnp.concatenate([row[:, None] * inv, col[:, None] * inv], axis=-1)
    return jnp.cos(ang).T, jnp.sin(ang).T


def _cparams(n_axes):
    return pltpu.CompilerParams(dimension_semantics=("arbitrary",) * n_axes,
                                vmem_limit_bytes=VMEM_LIMIT)


def _band_kernel(rb_ref, idx_ref, out_ref):
    h = pl.program_id(0)
    rows_per_step = 8
    steps_per_chunk = Q_TILE // rows_per_step

    def body(t, carry):
        d = t // steps_per_chunk
        r0 = pl.multiple_of((t % steps_per_chunk) * rows_per_step, rows_per_step)
        idx = idx_ref[d, pl.ds(r0, rows_per_step), :]
        acc = jnp.zeros(idx.shape, jnp.float32)
        for b in range(NUM_BUCKETS):
            acc = jnp.where(idx == b, rb_ref[b, h] * LOG2E, acc)
        out_ref[0, d, pl.ds(r0, rows_per_step), :] = acc
        return carry

    lax.fori_loop(0, BAND_CHUNKS * steps_per_chunk, body, 0)


def _band_call(rel_bias):
    d = jnp.arange(BAND_CHUNKS, dtype=jnp.int32)[:, None, None]
    r = jnp.arange(Q_TILE, dtype=jnp.int32)[None, :, None]
    j = jnp.arange(K_CHUNK, dtype=jnp.int32)[None, None, :]
    idx = _t5_bucket((d - 1) * K_CHUNK + j - r)
    return pl.pallas_call(
        _band_kernel,
        grid=(A_HEADS,),
        in_specs=[pl.BlockSpec(memory_space=pltpu.SMEM),
                  pl.BlockSpec((BAND_CHUNKS, Q_TILE, K_CHUNK), lambda h: (0, 0, 0))],
        out_specs=pl.BlockSpec((1, BAND_CHUNKS, Q_TILE, K_CHUNK), lambda h: (h, 0, 0, 0)),
        out_shape=jax.ShapeDtypeStruct((A_HEADS, BAND_CHUNKS, Q_TILE, K_CHUNK), jnp.float32),
        compiler_params=_cparams(1),
        name="bias_band",
    )(rel_bias, idx)


def _proj_kernel(x_ref, g_ref, wn_ref, wt_ref, cos_ref, sin_ref, qg_ref, kg_ref,
                 qa_ref, va_ref, sa_ref, vb_ref, sb_ref, qb_ref, kta_ref, ktb_ref):
    f32, bf16 = jnp.float32, jnp.bfloat16
    x = x_ref[...]
    ms = jnp.mean(x * x, axis=-1, keepdims=True)
    h = (x * lax.rsqrt(ms + EPS) * g_ref[...]).astype(bf16)

    def nat(lo, hi):
        return jnp.dot(h, wn_ref[:, lo:hi], preferred_element_type=f32)

    def tr(lo, hi):
        return lax.dot_general(wt_ref[lo:hi, :], h, (((1,), (1,)), ((), ())),
                               preferred_element_type=f32)

    def silu(v):
        return v * jax.nn.sigmoid(v)

    qa_ref[...] = (nat(0, 512) * Q_SCALE).astype(bf16)
    ones = jnp.ones((ROW_TILE, LANES), bf16)
    va = nat(512, 1024).astype(bf16)
    for hd in range(A_HEADS):
        va_ref[:, 2 * hd * LANES:(2 * hd + 1) * LANES] = va[:, hd * LANES:(hd + 1) * LANES]
        va_ref[:, (2 * hd + 1) * LANES:(2 * hd + 2) * LANES] = ones
    sa_ref[...] = silu(nat(1024, 1536)).astype(bf16)
    vb_ref[:, :LANES] = nat(1536, 1664).astype(bf16)
    vb_ref[:, LANES:] = ones
    sb_ref[...] = silu(nat(1664, 2176)).astype(bf16)

    n_chunks = ROW_TILE // K_CHUNK
    kta = tr(0, 512)
    for j in range(n_chunks):
        kta_ref[0, j] = kta[:, j * K_CHUNK:(j + 1) * K_CHUNK].astype(bf16)

    cos = cos_ref[...]
    sin = sin_ref[...]
    half = B_HEAD // 2

    def norm_rope(xt, g):
        msq = jnp.mean(xt * xt, axis=0, keepdims=True)
        y = xt * lax.rsqrt(msq + EPS) * g
        e, o = y[:half], y[half:]
        return jnp.concatenate([e * cos - o * sin, e * sin + o * cos], axis=0)

    qg = qg_ref[...]
    qt = tr(512, 1024)
    qn = jnp.concatenate([norm_rope(qt[i * B_HEAD:(i + 1) * B_HEAD], qg)
                          for i in range(B_Q_HEADS)], axis=0) * Q_SCALE
    qb_ref[...] = qn.T.astype(bf16)

    kg = kg_ref[...]
    kt = tr(1024, 1152)
    kn = jnp.concatenate([norm_rope(kt[i * B_HEAD:(i + 1) * B_HEAD], kg)
                          for i in range(B_KV_HEADS)], axis=0)
    for j in range(n_chunks):
        ktb_ref[0, j] = kn[:, j * K_CHUNK:(j + 1) * K_CHUNK].astype(bf16)


def _proj_call(xf, g_pre, wn, wt, cos_t, sin_t, qg, kg, batch, seq):
    m = xf.shape[0]
    tiles_per_seq = seq // ROW_TILE
    chunks_per_tile = ROW_TILE // K_CHUNK
    n_chunks = seq // K_CHUNK
    bf16 = jnp.bfloat16
    row = lambda i: (i, 0)
    const = lambda i: (0, 0)
    ktmap = lambda i: (i // tiles_per_seq, i % tiles_per_seq, 0, 0)
    return pl.pallas_call(
        _proj_kernel,
        grid=(m // ROW_TILE,),
        in_specs=[pl.BlockSpec((ROW_TILE, D_MODEL), row),
                  pl.BlockSpec((1, D_MODEL), const),
                  pl.BlockSpec(wn.shape, const),
                  pl.BlockSpec(wt.shape, const),
                  pl.BlockSpec((B_HEAD // 2, ROW_TILE), lambda i: (0, i % tiles_per_seq)),
                  pl.BlockSpec((B_HEAD // 2, ROW_TILE), lambda i: (0, i % tiles_per_seq)),
                  pl.BlockSpec((B_HEAD, 1), const),
                  pl.BlockSpec((B_HEAD, 1), const)],
        out_specs=[pl.BlockSpec((ROW_TILE, A_WIDTH), row),
                   pl.BlockSpec((ROW_TILE, 2 * A_WIDTH), row),
                   pl.BlockSpec((ROW_TILE, A_WIDTH), row),
                   pl.BlockSpec((ROW_TILE, 2 * LANES), row),
                   pl.BlockSpec((ROW_TILE, B_WIDTH), row),
                   pl.BlockSpec((ROW_TILE, B_WIDTH), row),
                   pl.BlockSpec((1, chunks_per_tile, A_WIDTH, K_CHUNK), ktmap),
                   pl.BlockSpec((1, chunks_per_tile, B_KV_HEADS * B_HEAD, K_CHUNK), ktmap)],
        out_shape=[jax.ShapeDtypeStruct((m, A_WIDTH), bf16),
                   jax.ShapeDtypeStruct((m, 2 * A_WIDTH), bf16),
                   jax.ShapeDtypeStruct((m, A_WIDTH), bf16),
                   jax.ShapeDtypeStruct((m, 2 * LANES), bf16),
                   jax.ShapeDtypeStruct((m, B_WIDTH), bf16),
                   jax.ShapeDtypeStruct((m, B_WIDTH), bf16),
                   jax.ShapeDtypeStruct((batch, n_chunks, A_WIDTH, K_CHUNK), bf16),
                   jax.ShapeDtypeStruct((batch, n_chunks, B_KV_HEADS * B_HEAD, K_CHUNK), bf16)],
        compiler_params=_cparams(1),
        name="in_proj",
    )(xf, g_pre, wn, wt, cos_t, sin_t, qg, kg)


def _attn_kernel(*refs, diff, lam_init, n_chunks, nq, n_pairs, n_tiles):
    f32, bf16 = jnp.float32, jnp.bfloat16
    if diff:
        (rb_ref, lamp_ref, g_ref, q_ref, kt_ref, v_ref, gate_ref, band_ref,
         o_ref, lhs_ref, s_even, s_odd, m_even, m_odd) = refs
    else:
        (q_ref, kt_ref, v_ref, gate_ref,
         o_ref, lhs_ref, s_even, s_odd, m_even, m_odd) = refs
    tq = Q_TILE
    halves = (slice(0, tq), slice(tq, 2 * tq))
    t = pl.program_id(0)
    tile1 = jnp.minimum(t, n_tiles - 1)
    tile2 = jnp.maximum(t - 1, 0)
    qi1 = tile1 % nq
    qi2 = tile2 % nq

    @pl.when(t == 0)
    def _():
        s_odd[...] = jnp.zeros(s_odd.shape, f32)
        m_odd[...] = jnp.zeros(m_odd.shape, f32)

    def step(s1_ref, m1_ref, s2_ref, m2_ref):
        low = lax.broadcasted_iota(jnp.int32, (tq, LANES), 1) < (LANES // 2)
        q = q_ref[...].astype(f32)
        lhs_ref[0:tq, :] = jnp.where(low, q, 0.0).astype(bf16)
        lhs_ref[tq:2 * tq, :] = jnp.where(low, 0.0, q).astype(bf16)

        m = jnp.max(m2_ref[...], axis=-1, keepdims=True)
        m2_ref[...] = jnp.broadcast_to(m, m2_ref.shape)

        if diff:
            head1 = (tile1 // nq) % n_pairs
            head2 = (tile2 // nq) % n_pairs
            left1 = rb_ref[NUM_BUCKETS // 2 - 1, head1] * LOG2E
            right1 = rb_ref[NUM_BUCKETS - 1, head1] * LOG2E
            left2 = rb_ref[NUM_BUCKETS // 2 - 1, head2] * LOG2E
            right2 = rb_ref[NUM_BUCKETS - 1, head2] * LOG2E

        def chunk_of(qi, i):
            if not diff:
                return i, None
            c = qi - 1 + i
            return jnp.bitwise_and(c, n_chunks - 1), c

        accs = [None, None]
        for i in range(n_chunks):
            j1, c1 = chunk_of(qi1, i)
            if diff:
                in_range1 = jnp.logical_and(c1 >= 0, c1 < n_chunks)
                if i < BAND_CHUNKS:
                    bias1 = jnp.where(in_range1, band_ref[0, i],
                                      jnp.where(c1 < 0, right1, left1))
                else:
                    const1 = jnp.where(in_range1, right1, left1)
            for rows in halves:
                s = jnp.dot(lhs_ref[rows, :], kt_ref[0, j1], preferred_element_type=f32)
                if diff and i < BAND_CHUNKS:
                    s = s + bias1
                s1_ref[j1, rows, :] = s
                cmax = jnp.maximum(s[:, :LANES], s[:, LANES:])
                if diff and i >= BAND_CHUNKS:
                    cmax = cmax + const1
                m1_ref[rows, :] = cmax if i == 0 else jnp.maximum(m1_ref[rows, :], cmax)

            j2, c2 = chunk_of(qi2, i)
            if diff:
                k0 = pl.multiple_of(j2 * K_CHUNK, K_CHUNK)
                if i >= BAND_CHUNKS:
                    const2 = jnp.where(jnp.logical_and(c2 >= 0, c2 < n_chunks), right2, left2)
            else:
                k0 = j2 * K_CHUNK
            for h_idx, rows in enumerate(halves):
                s = s2_ref[j2, rows, :]
                mb = m2_ref[rows, :]
                if diff and i >= BAND_CHUNKS:
                    mb = mb - const2
                p = jnp.concatenate([jnp.exp2(s[:, :LANES] - mb), jnp.exp2(s[:, LANES:] - mb)],
                                    axis=1).astype(bf16)
                part = jnp.dot(p, v_ref[pl.ds(k0, K_CHUNK), :], preferred_element_type=f32)
                accs[h_idx] = part if accs[h_idx] is None else accs[h_idx] + part

        o_lo = accs[0][:, :LANES] / accs[0][:, LANES:]
        o_hi = accs[1][:, :LANES] / accs[1][:, LANES:]
        gate = gate_ref[...].astype(f32)
        if diff:
            lp = lamp_ref[...]
            lam = (jnp.exp(jnp.sum(lp[0:1] * lp[1:2], axis=-1, keepdims=True))
                   - jnp.exp(jnp.sum(lp[2:3] * lp[3:4], axis=-1, keepdims=True)) + lam_init)
            o = o_lo - lam * o_hi
            ms = jnp.mean(o * o, axis=-1, keepdims=True)
            o = (o * lax.rsqrt(ms + EPS) * g_ref[...]) * (1.0 - lam_init)
        else:
            o = jnp.where(low, o_lo, o_hi)
        o_ref[...] = (o * gate).astype(o_ref.dtype)

    @pl.when(t % 2 == 0)
    def _():
        step(s_even, m_even, s_odd, m_odd)

    @pl.when(t % 2 == 1)
    def _():
        step(s_odd, m_odd, s_even, m_even)


def _attn_call(operands, prefix_specs, q_arr, kt_arr, v_arr, gate_arr, band_arr, *, diff,
               lam_init, n_pairs, kt_per_pair, batch, seq, name):
    m = q_arr.shape[0]
    nq = seq // Q_TILE
    n_chunks = seq // K_CHUNK
    n_tiles = batch * n_pairs * nq

    def decode(tile):
        return tile // (n_pairs * nq), (tile // nq) % n_pairs, tile % nq

    def stage1(t):
        return decode(jnp.minimum(t, n_tiles - 1))

    def stage2(t):
        return decode(jnp.maximum(t - 1, 0))

    def q_map(t):
        b, p, i = stage1(t)
        return b * nq + i, p

    def kt_map(t):
        b, p, _ = stage1(t)
        return b, 0, p if kt_per_pair else 0, 0

    def v_map(t):
        b, p, _ = stage2(t)
        return b, p if kt_per_pair else 0

    def out_map(t):
        b, p, i = stage2(t)
        return b * nq + i, p

    in_specs = list(prefix_specs) + [
        pl.BlockSpec((Q_TILE, LANES), q_map),
        pl.BlockSpec((1, n_chunks, LANES, K_CHUNK), kt_map),
        pl.BlockSpec((seq, 2 * LANES), v_map),
        pl.BlockSpec((Q_TILE, LANES), out_map)]
    tail = ()
    if band_arr is not None:
        in_specs.append(pl.BlockSpec((1, BAND_CHUNKS, Q_TILE, K_CHUNK),
                                     lambda t: (stage1(t)[1], 0, 0, 0)))
        tail = (band_arr,)
    scratch = [pltpu.VMEM((2 * Q_TILE, LANES), jnp.bfloat16),
               pltpu.VMEM((n_chunks, 2 * Q_TILE, K_CHUNK), jnp.float32),
               pltpu.VMEM((n_chunks, 2 * Q_TILE, K_CHUNK), jnp.float32),
               pltpu.VMEM((2 * Q_TILE, LANES), jnp.float32),
               pltpu.VMEM((2 * Q_TILE, LANES), jnp.float32)]
    return pl.pallas_call(
        functools.partial(_attn_kernel, diff=diff, lam_init=lam_init, n_chunks=n_chunks,
                          nq=nq, n_pairs=n_pairs, n_tiles=n_tiles),
        grid=(n_tiles + 1,),
        in_specs=in_specs,
        out_specs=pl.BlockSpec((Q_TILE, LANES), out_map),
        out_shape=jax.ShapeDtypeStruct((m, n_pairs * LANES), jnp.bfloat16),
        scratch_shapes=scratch,
        compiler_params=_cparams(1),
        name=name,
    )(*operands, q_arr, kt_arr, v_arr, gate_arr, *tail)


def _attn_diff_call(rel_bias, lam_params, subln_g, qa, kta, va, sa, band, lam_init, batch, seq):
    prefix = [pl.BlockSpec(memory_space=pltpu.SMEM),
              pl.BlockSpec((4, A_QK_HEAD), lambda t: (0, 0)),
              pl.BlockSpec((1, A_V_HEAD), lambda t: (0, 0))]
    return _attn_call((rel_bias, lam_params, subln_g), prefix, qa, kta, va, sa, band,
                      diff=True, lam_init=lam_init, n_pairs=A_HEADS, kt_per_pair=True,
                      batch=batch, seq=seq, name="attn_diff")


def _attn_gqa_call(qb, ktb, vb, sb, batch, seq):
    return _attn_call((), [], qb, ktb, vb, sb, None, diff=False, lam_init=0.0,
                      n_pairs=B_Q_HEADS // 2, kt_per_pair=False, batch=batch, seq=seq,
                      name="attn_gqa")


def _out_kernel(ya_ref, yb_ref, wa_ref, wb_ref, x_ref, g_ref, o_ref):
    f32 = jnp.float32
    y = (jnp.dot(ya_ref[...], wa_ref[...], preferred_element_type=f32)
         + jnp.dot(yb_ref[...], wb_ref[...], preferred_element_type=f32))
    ms = jnp.mean(y * y, axis=-1, keepdims=True)
    o_ref[...] = x_ref[...] + y * lax.rsqrt(ms + EPS) * g_ref[...]


def _out_call(ya, yb, wa, wb, xf, g_post):
    m = xf.shape[0]
    row = lambda i: (i, 0)
    const = lambda i: (0, 0)
    return pl.pallas_call(
        _out_kernel,
        grid=(m // ROW_TILE,),
        in_specs=[pl.BlockSpec((ROW_TILE, A_WIDTH), row),
                  pl.BlockSpec((ROW_TILE, B_WIDTH), row),
                  pl.BlockSpec((A_WIDTH, D_MODEL), const),
                  pl.BlockSpec((B_WIDTH, D_MODEL), const),
                  pl.BlockSpec((ROW_TILE, D_MODEL), row),
                  pl.BlockSpec((1, D_MODEL), const)],
        out_specs=pl.BlockSpec((ROW_TILE, D_MODEL), row),
        out_shape=jax.ShapeDtypeStruct((m, D_MODEL), jnp.float32),
        compiler_params=_cparams(1),
        name="out_proj",
    )(ya, yb, wa, wb, xf, g_post)


def _layer_weights(w_in_l, w_out_l, q_norm_g_l, k_norm_g_l):
    bf16 = jnp.bfloat16
    d = w_in_l.shape[0]
    per_group = B_Q_HEADS // B_KV_HEADS
    half = B_HEAD // 2
    bq = w_in_l[:, _BQ:_BK].reshape(d, B_KV_HEADS, per_group, half, 2)
    bq = bq.transpose(0, 2, 1, 4, 3).reshape(d, B_WIDTH)
    bk = w_in_l[:, _BK:_BV].reshape(d, B_KV_HEADS, half, 2)
    bk = bk.transpose(0, 1, 3, 2).reshape(d, B_KV_HEADS * B_HEAD)
    bg = w_in_l[:, _BG:].reshape(d, B_KV_HEADS, per_group, B_HEAD)
    bg = bg.transpose(0, 2, 1, 3).reshape(d, B_WIDTH)
    wn = jnp.concatenate([w_in_l[:, _AQ:_AK], w_in_l[:, _AV:_AG], w_in_l[:, _AG:_BQ],
                          w_in_l[:, _BV:_BG], bg], axis=1).astype(bf16)
    wt = jnp.concatenate([w_in_l[:, _AK:_AV], bq, bk], axis=1).astype(bf16).T
    wa = w_out_l[:A_WIDTH].astype(bf16)
    wb = w_out_l[A_WIDTH:].reshape(B_KV_HEADS, per_group, B_HEAD, -1)
    wb = wb.transpose(1, 0, 2, 3).reshape(B_WIDTH, -1).astype(bf16)
    qg = q_norm_g_l.reshape(half, 2).T.reshape(B_HEAD, 1)
    kg = k_norm_g_l.reshape(half, 2).T.reshape(B_HEAD, 1)
    return wn, wt, wa, wb, qg, kg


def kernel(x, rel_bias, pre_norm_g, w_in, diff_lambda, diff_subln_g, q_norm_g, k_norm_g,
           w_out, post_norm_g):
    batch, seq, d_model = x.shape
    xf = x.reshape(batch * seq, d_model)
    band = _band_call(rel_bias)
    cos_t, sin_t = _rope_tables_t(seq)
    for l in range(DEPTH):
        lam_init = 0.8 - 0.6 * math.exp(-0.3 * l)
        wn, wt, wa, wb, qg, kg = _layer_weights(w_in[l], w_out[l], q_norm_g[l], k_norm_g[l])
        qa, va, sa, vb, sb, qb, kta, ktb = _proj_call(
            xf, pre_norm_g[l].reshape(1, d_model), wn, wt, cos_t, sin_t, qg, kg, batch, seq)
        ya = _attn_diff_call(rel_bias, diff_lambda[l], diff_subln_g[l].reshape(1, A_V_HEAD),
                             qa, kta, va, sa, band, lam_init, batch, seq)
        yb = _attn_gqa_call(qb, ktb, vb, sb, batch, seq)
        xf = _out_call(ya, yb, wa, wb, xf, post_norm_g[l].reshape(1, d_model))
    return xf.reshape(batch, seq, d_model)
```

```python
import functools
import math

import jax
import jax.numpy as jnp
from jax import lax
from jax.experimental import pallas as pl
from jax.experimental.pallas import tpu as pltpu

D_MODEL = 1024
DEPTH = 2
A_WIDTH = 512
B_WIDTH = 512
A_HEADS = 4
A_V_HEAD = 128
A_QK_HEAD = 64
B_HEAD = 64
B_Q_HEADS = 8
B_KV_HEADS = 2
GRID_W = 64
ROPE_THETA = 10000.0
NUM_BUCKETS = 32
MAX_DISTANCE = 128
EPS = 1e-6
LOG2E = math.log2(math.e)
Q_SCALE = (A_QK_HEAD ** -0.5) * LOG2E

LANES = 128
SUBLANES = 8
ROW_TILE = 512
Q_TILE = 256
K_CHUNK = 256
BAND_CHUNKS = 3
BIAS_SLOTS = 16
BIAS_TERMS = 3
VMEM_LIMIT = 48 * 1024 * 1024

_AQ, _AK, _AV, _AG, _BQ, _BK, _BV, _BG = 0, 512, 1024, 1536, 2048, 2560, 2688, 2816
_T_AQ, _T_AV, _T_BQ, _T_BK, _T_BV, _T_END = 0, 512, 1024, 1536, 1664, 1792
_N_AK, _N_AG, _N_BG, _N_END = 0, 512, 1024, 1536


def _t5_bucket(rel):
    nb = NUM_BUCKETS // 2
    max_exact = nb // 2
    n = jnp.abs(rel)
    large = max_exact + (jnp.log(jnp.maximum(n, 1).astype(jnp.float32) / max_exact)
                         / math.log(MAX_DISTANCE / max_exact) * (nb - max_exact)).astype(jnp.int32)
    large = jnp.minimum(large, nb - 1)
    return jnp.where(rel > 0, nb, 0) + jnp.where(n < max_exact, n, large)


def _rope_tables_t(n):
    rows = n // GRID_W
    row = jnp.repeat(jnp.arange(rows, dtype=jnp.float32), GRID_W)
    col = jnp.tile(jnp.arange(GRID_W, dtype=jnp.float32), rows)
    axis_dim = B_HEAD // 2
    inv = ROPE_THETA ** (-jnp.arange(0, axis_dim, 2, dtype=jnp.float32) / axis_dim)
    ang = jnp.concatenate([row[:, None] * inv, col[:, None] * inv], axis=-1)
    return jnp.cos(ang).T, jnp.sin(ang).T


def _cparams(n_axes):
    return pltpu.CompilerParams(dimension_semantics=("arbitrary",) * n_axes,
                                vmem_limit_bytes=VMEM_LIMIT)


def _bias_slot_mask(slot_index, chunk):
    in_terms = jnp.right_shift(jnp.bitwise_and(slot_index, A_QK_HEAD - 1), 4) < BIAS_TERMS
    return jnp.logical_and(in_terms, jnp.bitwise_and(slot_index, BIAS_SLOTS - 1) == chunk)


def _band_kernel(rb_ref, idx_ref, out_ref):
    h = pl.program_id(0)
    steps_per_chunk = K_CHUNK // SUBLANES

    def body(t, carry):
        d = t // steps_per_chunk
        r0 = pl.multiple_of((t % steps_per_chunk) * SUBLANES, SUBLANES)
        idx = idx_ref[d, pl.ds(r0, SUBLANES), :]
        acc = jnp.zeros(idx.shape, jnp.float32)
        for b in range(NUM_BUCKETS):
            in_bucket = jnp.logical_and(idx >= b, idx < b + 1)
            acc = jnp.where(in_bucket, rb_ref[b, h] * LOG2E, acc)
        out_ref[0, d, pl.ds(r0, SUBLANES), :] = acc
        return carry

    lax.fori_loop(0, BAND_CHUNKS * steps_per_chunk, body, 0)


def _band_call(rel_bias):
    d = jnp.arange(BAND_CHUNKS, dtype=jnp.int32)[:, None, None]
    j = jnp.arange(K_CHUNK, dtype=jnp.int32)[None, :, None]
    r = jnp.arange(Q_TILE, dtype=jnp.int32)[None, None, :]
    idx = _t5_bucket((d - 1) * K_CHUNK + j - r)
    idx = jnp.take(jnp.arange(NUM_BUCKETS, dtype=jnp.int32), idx)
    return pl.pallas_call(
        _band_kernel,
        grid=(A_HEADS,),
        in_specs=[pl.BlockSpec(memory_space=pltpu.SMEM),
                  pl.BlockSpec((BAND_CHUNKS, K_CHUNK, Q_TILE), lambda h: (0, 0, 0))],
        out_specs=pl.BlockSpec((1, BAND_CHUNKS, K_CHUNK, Q_TILE), lambda h: (h, 0, 0, 0)),
        out_shape=jax.ShapeDtypeStruct((A_HEADS, BAND_CHUNKS, K_CHUNK, Q_TILE), jnp.float32),
        compiler_params=_cparams(1),
        name="bias_band",
    )(rel_bias, idx)


def _proj_kernel(x_ref, g_ref, wn_ref, wt_ref, cos_ref, sin_ref, qg_ref, kg_ref,
                 qa_ref, ka_ref, va_ref, sa_ref, qb_ref, kb_ref, vb_ref, sb_ref,
                 *, tiles_per_seq):
    f32, bf16 = jnp.float32, jnp.bfloat16
    x = x_ref[...]
    ms = jnp.mean(x * x, axis=-1, keepdims=True)
    h = (x * lax.rsqrt(ms + EPS) * g_ref[...]).astype(bf16)

    def nat(lo, hi):
        return jnp.dot(h, wn_ref[:, lo:hi], preferred_element_type=f32)

    def silu(v):
        return v * jax.nn.sigmoid(v)

    pt = lax.dot_general(wt_ref[...], h, (((1,), (1,)), ((), ())), preferred_element_type=f32)
    qa_ref[...] = (pt[_T_AQ:_T_AV] * Q_SCALE).astype(bf16)
    va_ref[...] = pt[_T_AV:_T_BQ].astype(bf16)
    vb_ref[...] = pt[_T_BV:_T_END].astype(bf16)

    cos = cos_ref[...]
    sin = sin_ref[...]
    half = B_HEAD // 2

    def norm_rope(xt, g):
        msq = jnp.mean(xt * xt, axis=0, keepdims=True)
        y = xt * lax.rsqrt(msq + EPS) * g
        e, o = y[:half], y[half:]
        return jnp.concatenate([e * cos - o * sin, e * sin + o * cos], axis=0)

    qg = qg_ref[...]
    qn = jnp.concatenate([norm_rope(pt[_T_BQ + i * B_HEAD:_T_BQ + (i + 1) * B_HEAD], qg)
                          for i in range(B_Q_HEADS)], axis=0) * Q_SCALE
    qb_ref[...] = qn.astype(bf16)
    kg = kg_ref[...]
    kn = jnp.concatenate([norm_rope(pt[_T_BK + i * B_HEAD:_T_BK + (i + 1) * B_HEAD], kg)
                          for i in range(B_KV_HEADS)], axis=0)
    kb_ref[...] = kn.T.astype(bf16)

    sa_ref[...] = silu(nat(_N_AG, _N_BG)).astype(bf16)
    sb_ref[...] = silu(nat(_N_BG, _N_END)).astype(bf16)

    ak = nat(_N_AK, _N_AG)
    lane = lax.broadcasted_iota(jnp.int32, (ROW_TILE, LANES), 1)
    row = lax.broadcasted_iota(jnp.int32, (ROW_TILE, LANES), 0)
    chunk = ((pl.program_id(0) % tiles_per_seq) * (ROW_TILE // K_CHUNK)
             + jnp.right_shift(row, K_CHUNK.bit_length() - 1))
    hot = jnp.where(_bias_slot_mask(lane, chunk), 1.0, 0.0)
    low = lane < A_QK_HEAD
    for hd in range(A_HEADS):
        akh = ak[:, hd * LANES:(hd + 1) * LANES]
        ka_ref[:, 2 * hd * LANES:(2 * hd + 1) * LANES] = jnp.where(low, akh, hot).astype(bf16)
        ka_ref[:, (2 * hd + 1) * LANES:(2 * hd + 2) * LANES] = jnp.where(low, hot, akh).astype(bf16)


def _proj_call(xf, g_pre, wn, wt, cos_t, sin_t, qg, kg, seq):
    m = xf.shape[0]
    tiles_per_seq = seq // ROW_TILE
    bf16 = jnp.bfloat16
    row = lambda i: (i, 0)
    col = lambda i: (0, i)
    const = lambda i: (0, 0)
    pos = lambda i: (0, i % tiles_per_seq)
    return pl.pallas_call(
        functools.partial(_proj_kernel, tiles_per_seq=tiles_per_seq),
        grid=(m // ROW_TILE,),
        in_specs=[pl.BlockSpec((ROW_TILE, D_MODEL), row),
                  pl.BlockSpec((1, D_MODEL), const),
                  pl.BlockSpec(wn.shape, const),
                  pl.BlockSpec(wt.shape, const),
                  pl.BlockSpec((B_HEAD // 2, ROW_TILE), pos),
                  pl.BlockSpec((B_HEAD // 2, ROW_TILE), pos),
                  pl.BlockSpec((B_HEAD, 1), const),
                  pl.BlockSpec((B_HEAD, 1), const)],
        out_specs=[pl.BlockSpec((A_WIDTH, ROW_TILE), col),
                   pl.BlockSpec((ROW_TILE, 2 * A_WIDTH), row),
                   pl.BlockSpec((A_WIDTH, ROW_TILE), col),
                   pl.BlockSpec((ROW_TILE, A_WIDTH), row),
                   pl.BlockSpec((B_WIDTH, ROW_TILE), col),
                   pl.BlockSpec((ROW_TILE, B_KV_HEADS * B_HEAD), row),
                   pl.BlockSpec((B_KV_HEADS * B_HEAD, ROW_TILE), col),
                   pl.BlockSpec((ROW_TILE, B_WIDTH), row)],
        out_shape=[jax.ShapeDtypeStruct((A_WIDTH, m), bf16),
                   jax.ShapeDtypeStruct((m, 2 * A_WIDTH), bf16),
                   jax.ShapeDtypeStruct((A_WIDTH, m), bf16),
                   jax.ShapeDtypeStruct((m, A_WIDTH), bf16),
                   jax.ShapeDtypeStruct((B_WIDTH, m), bf16),
                   jax.ShapeDtypeStruct((m, B_KV_HEADS * B_HEAD), bf16),
                   jax.ShapeDtypeStruct((B_KV_HEADS * B_HEAD, m), bf16),
                   jax.ShapeDtypeStruct((m, B_WIDTH), bf16)],
        compiler_params=_cparams(1),
        name="in_proj",
    )(xf, g_pre, wn, wt, cos_t, sin_t, qg, kg)


def _fold_rows(x, op):
    parts = [x[g * SUBLANES:(g + 1) * SUBLANES] for g in range(x.shape[0] // SUBLANES)]
    while len(parts) > 1:
        parts = [op(parts[i], parts[i + 1]) for i in range(0, len(parts), 2)]
    return parts[0]


def _pipeline_tiles(n_tiles, nq):
    t = pl.program_id(0)
    tile1 = jnp.minimum(t, n_tiles - 1)
    return t, tile1, tile1 % nq


def _zero_first_previous(t, s_odd, m_odd):
    @pl.when(t == 0)
    def _():
        s_odd[...] = jnp.zeros(s_odd.shape, jnp.float32)
        m_odd[...] = jnp.zeros(m_odd.shape, jnp.float32)


def _by_parity(t, step, s_even, m_even, s_odd, m_odd):
    @pl.when(t % 2 == 0)
    def _():
        step(s_even, m_even, s_odd, m_odd)

    @pl.when(t % 2 == 1)
    def _():
        step(s_odd, m_odd, s_even, m_even)


def _attn_gqa_kernel(q_ref, k_ref, v_ref, gate_ref, o_ref,
                     w_ref, s_even, s_odd, m_even, m_odd, *, n_chunks, nq, n_tiles):
    f32, bf16 = jnp.float32, jnp.bfloat16
    t, _, _ = _pipeline_tiles(n_tiles, nq)
    _zero_first_previous(t, s_odd, m_odd)

    def step(s1_ref, m1_ref, s2_ref, m2_ref):
        top = lax.broadcasted_iota(jnp.int32, (LANES, Q_TILE), 0) < B_HEAD
        qt = q_ref[...].astype(f32)
        w_ref[0] = jnp.where(top, qt, 0.0).astype(bf16)
        w_ref[1] = jnp.where(top, 0.0, qt).astype(bf16)

        m_row = [jnp.max(m2_ref[st], axis=0, keepdims=True) for st in range(2)]

        m_acc = [None, None]
        l_acc = [None, None]
        o_acc = [None, None]
        for c in range(n_chunks):
            keys = slice(c * K_CHUNK, (c + 1) * K_CHUNK)
            for st in range(2):
                s = jnp.dot(k_ref[keys, :], w_ref[st], preferred_element_type=f32)
                s1_ref[c, st] = s
                cm = _fold_rows(s, jnp.maximum)
                m_acc[st] = cm if c == 0 else jnp.maximum(m_acc[st], cm)
            for st in range(2):
                p = jnp.exp2(s2_ref[c, st] - m_row[st])
                ps = _fold_rows(p, jnp.add)
                l_acc[st] = ps if c == 0 else l_acc[st] + ps
                part = jnp.dot(v_ref[:, keys], p.astype(bf16), preferred_element_type=f32)
                o_acc[st] = part if c == 0 else o_acc[st] + part
        for st in range(2):
            m1_ref[st] = m_acc[st]

        l_row = [jnp.sum(l_acc[st], axis=0, keepdims=True) for st in range(2)]
        ot = jnp.concatenate([o_acc[0][:B_HEAD] / l_row[0], o_acc[1][B_HEAD:] / l_row[1]], axis=0)
        o_ref[...] = (ot.T * gate_ref[...].astype(f32)).astype(o_ref.dtype)

    _by_parity(t, step, s_even, m_even, s_odd, m_odd)


def _attn_diff_kernel(rb_ref, lamp_ref, g_ref, q_ref, k_ref, v_ref, gate_ref, band_ref, o_ref,
                      w_ref, s_even, s_odd, m_even, m_odd, *, lam_init, n_chunks, nq, n_tiles):
    f32, bf16 = jnp.float32, jnp.bfloat16
    t, tile1, qi1 = _pipeline_tiles(n_tiles, nq)
    head1 = (tile1 // nq) % A_HEADS
    _zero_first_previous(t, s_odd, m_odd)

    def step(s1_ref, m1_ref, s2_ref, m2_ref):
        row = lax.broadcasted_iota(jnp.int32, (LANES, Q_TILE), 0)
        top = row < A_QK_HEAD
        slot = jnp.bitwise_and(row, BIAS_SLOTS - 1)
        term = jnp.right_shift(jnp.bitwise_and(row, A_QK_HEAD - 1), 4)
        left = rb_ref[NUM_BUCKETS // 2 - 1, head1] * LOG2E
        right = rb_ref[NUM_BUCKETS - 1, head1] * LOG2E
        const = jnp.where(jnp.abs(slot - qi1) <= 1, 0.0, jnp.where(slot < qi1, left, right))
        t0 = const.astype(bf16).astype(f32)
        t1 = (const - t0).astype(bf16).astype(f32)
        t2 = (const - t0 - t1).astype(bf16).astype(f32)
        bias_rows = jnp.where(term == 0, t0, jnp.where(term == 1, t1,
                                                      jnp.where(term == 2, t2, 0.0)))
        qt = q_ref[...].astype(f32)
        w_ref[0] = jnp.where(top, qt, bias_rows).astype(bf16)
        w_ref[1] = jnp.where(top, bias_rows, qt).astype(bf16)

        m_row = [jnp.max(m2_ref[st], axis=0, keepdims=True) for st in range(2)]

        m_acc = [None, None]
        l_acc = [None, None]
        o_acc = [None, None]
        for i in range(n_chunks):
            c1 = qi1 - 1 + i
            j1 = jnp.bitwise_and(c1, n_chunks - 1)
            key0 = pl.multiple_of(j1 * K_CHUNK, K_CHUNK)
            if i < BAND_CHUNKS:
                in_range = jnp.logical_and(c1 >= 0, c1 < n_chunks)
                band = jnp.where(in_range, band_ref[0, i], 0.0)
            for st in range(2):
                s = jnp.dot(k_ref[pl.ds(key0, K_CHUNK), st * LANES:(st + 1) * LANES], w_ref[st],
                            preferred_element_type=f32)
                if i < BAND_CHUNKS:
                    s = s + band
                s1_ref[j1, st] = s
                cm = _fold_rows(s, jnp.maximum)
                m_acc[st] = cm if i == 0 else jnp.maximum(m_acc[st], cm)
            for st in range(2):
                p = jnp.exp2(s2_ref[i, st] - m_row[st])
                ps = _fold_rows(p, jnp.add)
                l_acc[st] = ps if i == 0 else l_acc[st] + ps
                part = jnp.dot(v_ref[:, i * K_CHUNK:(i + 1) * K_CHUNK], p.astype(bf16),
                               preferred_element_type=f32)
                o_acc[st] = part if i == 0 else o_acc[st] + part
        for st in range(2):
            m1_ref[st] = m_acc[st]

        l_row = [jnp.sum(l_acc[st], axis=0, keepdims=True) for st in range(2)]
        lp = lamp_ref[...]
        lam = (jnp.exp(jnp.sum(lp[0:1] * lp[1:2], axis=-1, keepdims=True))
               - jnp.exp(jnp.sum(lp[2:3] * lp[3:4], axis=-1, keepdims=True)) + lam_init)
        o = (o_acc[0] / l_row[0] - lam * (o_acc[1] / l_row[1])).T
        ms = jnp.mean(o * o, axis=-1, keepdims=True)
        o = (o * lax.rsqrt(ms + EPS) * g_ref[...]) * (1.0 - lam_init)
        o_ref[...] = (o * gate_ref[...].astype(f32)).astype(o_ref.dtype)

    _by_parity(t, step, s_even, m_even, s_odd, m_odd)


def _attn_call(kernel_fn, operands, prefix_specs, q_arr, k_arr, v_arr, gate_arr, band_arr,
               extra_scratch, *, n_pairs, per_pair, batch, seq, name):
    m = gate_arr.shape[0]
    nq = seq // Q_TILE
    n_chunks = seq // K_CHUNK
    n_tiles = batch * n_pairs * nq
    k_lanes = k_arr.shape[1] // (n_pairs if per_pair else 1)

    def decode(tile):
        return tile // (n_pairs * nq), (tile // nq) % n_pairs, tile % nq

    def stage1(t):
        return decode(jnp.minimum(t, n_tiles - 1))

    def stage2(t):
        return decode(jnp.maximum(t - 1, 0))

    def q_map(t):
        b, p, i = stage1(t)
        return p, b * nq + i

    def k_map(t):
        b, p, _ = stage1(t)
        return b, p if per_pair else 0

    def v_map(t):
        b, p, _ = stage2(t)
        return p if per_pair else 0, b

    def out_map(t):
        b, p, i = stage2(t)
        return b * nq + i, p

    in_specs = list(prefix_specs) + [
        pl.BlockSpec((LANES, Q_TILE), q_map),
        pl.BlockSpec((seq, k_lanes), k_map),
        pl.BlockSpec((LANES, seq), v_map),
        pl.BlockSpec((Q_TILE, LANES), out_map)]
    tail = ()
    if band_arr is not None:
        in_specs.append(pl.BlockSpec((1, BAND_CHUNKS, K_CHUNK, Q_TILE),
                                     lambda t: (stage1(t)[1], 0, 0, 0)))
        tail = (band_arr,)
    scores = pltpu.VMEM((n_chunks, 2, K_CHUNK, Q_TILE), jnp.float32)
    key_max = pltpu.VMEM((2, SUBLANES, Q_TILE), jnp.float32)
    scratch = [pltpu.VMEM((2, LANES, Q_TILE), jnp.bfloat16),
               scores, scores,
               key_max, key_max]
    return pl.pallas_call(
        functools.partial(kernel_fn, n_chunks=n_chunks, nq=nq, n_tiles=n_tiles),
        grid=(n_tiles + 1,),
        in_specs=in_specs,
        out_specs=pl.BlockSpec((Q_TILE, LANES), out_map),
        out_shape=jax.ShapeDtypeStruct((m, n_pairs * LANES), jnp.bfloat16),
        scratch_shapes=scratch + list(extra_scratch),
        compiler_params=_cparams(1),
        name=name,
    )(*operands, q_arr, k_arr, v_arr, gate_arr, *tail)


def _attn_diff_call(rel_bias, lam_params, subln_g, qa, ka, va, sa, band, lam_init, batch, seq):
    n_chunks = seq // K_CHUNK
    assert n_chunks <= BIAS_SLOTS and n_chunks & (n_chunks - 1) == 0
    prefix = [pl.BlockSpec(memory_space=pltpu.SMEM),
              pl.BlockSpec((4, A_QK_HEAD), lambda t: (0, 0)),
              pl.BlockSpec((1, A_V_HEAD), lambda t: (0, 0))]
    return _attn_call(functools.partial(_attn_diff_kernel, lam_init=lam_init),
                      (rel_bias, lam_params, subln_g), prefix, qa, ka, va, sa, band, [],
                      n_pairs=A_HEADS, per_pair=True, batch=batch, seq=seq, name="attn_diff")


def _attn_gqa_call(qb, kb, vb, sb, batch, seq):
    return _attn_call(_attn_gqa_kernel, (), [], qb, kb, vb, sb, None, [],
                      n_pairs=B_Q_HEADS // 2, per_pair=False, batch=batch, seq=seq,
                      name="attn_gqa")


def _out_kernel(ya_ref, yb_ref, wa_ref, wb_ref, x_ref, g_ref, o_ref):
    f32 = jnp.float32
    y = (jnp.dot(ya_ref[...], wa_ref[...], preferred_element_type=f32)
         + jnp.dot(yb_ref[...], wb_ref[...], preferred_element_type=f32))
    ms = jnp.mean(y * y, axis=-1, keepdims=True)
    o_ref[...] = x_ref[...] + y * lax.rsqrt(ms + EPS) * g_ref[...]


def _out_call(ya, yb, wa, wb, xf, g_post):
    m = xf.shape[0]
    row = lambda i: (i, 0)
    const = lambda i: (0, 0)
    return pl.pallas_call(
        _out_kernel,
        grid=(m // ROW_TILE,),
        in_specs=[pl.BlockSpec((ROW_TILE, A_WIDTH), row),
                  pl.BlockSpec((ROW_TILE, B_WIDTH), row),
                  pl.BlockSpec((A_WIDTH, D_MODEL), const),
                  pl.BlockSpec((B_WIDTH, D_MODEL), const),
                  pl.BlockSpec((ROW_TILE, D_MODEL), row),
                  pl.BlockSpec((1, D_MODEL), const)],
        out_specs=pl.BlockSpec((ROW_TILE, D_MODEL), row),
        out_shape=jax.ShapeDtypeStruct((m, D_MODEL), jnp.float32),
        compiler_params=_cparams(1),
        name="out_proj",
    )(ya, yb, wa, wb, xf, g_post)


def _layer_weights(w_in_l, w_out_l, q_norm_g_l, k_norm_g_l):
    bf16 = jnp.bfloat16
    d = w_in_l.shape[0]
    per_group = B_Q_HEADS // B_KV_HEADS
    half = B_HEAD // 2
    bq = w_in_l[:, _BQ:_BK].reshape(d, B_KV_HEADS, per_group, half, 2)
    bq = bq.transpose(0, 2, 1, 4, 3).reshape(d, B_WIDTH)
    bk = w_in_l[:, _BK:_BV].reshape(d, B_KV_HEADS, half, 2)
    bk = bk.transpose(0, 1, 3, 2).reshape(d, B_KV_HEADS * B_HEAD)
    bg = w_in_l[:, _BG:].reshape(d, B_KV_HEADS, per_group, B_HEAD)
    bg = bg.transpose(0, 2, 1, 3).reshape(d, B_WIDTH)
    wn = jnp.concatenate([w_in_l[:, _AK:_AV], w_in_l[:, _AG:_BQ], bg], axis=1).astype(bf16)
    wt = jnp.concatenate([w_in_l[:, _AQ:_AK], w_in_l[:, _AV:_AG], bq, bk, w_in_l[:, _BV:_BG]],
                         axis=1).astype(bf16).T
    wa = w_out_l[:A_WIDTH].astype(bf16)
    wb = w_out_l[A_WIDTH:].reshape(B_KV_HEADS, per_group, B_HEAD, -1)
    wb = wb.transpose(1, 0, 2, 3).reshape(B_WIDTH, -1).astype(bf16)
    qg = q_norm_g_l.reshape(half, 2).T.reshape(B_HEAD, 1)
    kg = k_norm_g_l.reshape(half, 2).T.reshape(B_HEAD, 1)
    return wn, wt, wa, wb, qg, kg


def kernel(x, rel_bias, pre_norm_g, w_in, diff_lambda, diff_subln_g, q_norm_g, k_norm_g,
           w_out, post_norm_g):
    batch, seq, d_model = x.shape
    xf = x.reshape(batch * seq, d_model)
    band = _band_call(rel_bias)
    cos_t, sin_t = _rope_tables_t(seq)
    for l in range(DEPTH):
        lam_init = 0.8 - 0.6 * math.exp(-0.3 * l)
        wn, wt, wa, wb, qg, kg = _layer_weights(w_in[l], w_out[l], q_norm_g[l], k_norm_g[l])
        qa, ka, va, sa, qb, kb, vb, sb = _proj_call(
            xf, pre_norm_g[l].reshape(1, d_model), wn, wt, cos_t, sin_t, qg, kg, seq)
        ya = _attn_diff_call(rel_bias, diff_lambda[l], diff_subln_g[l].reshape(1, A_V_HEAD),
                             qa, ka, va, sa, band, lam_init, batch, seq)
        yb = _attn_gqa_call(qb, kb, vb, sb, batch, seq)
        xf = _out_call(ya, yb, wa, wb, xf, post_norm_g[l].reshape(1, d_model))
    return xf.reshape(batch, seq, d_model)
```

```python
import functools
import math

import jax
import jax.numpy as jnp
from jax import lax
from jax.experimental import pallas as pl
from jax.experimental.pallas import tpu as pltpu

D_MODEL = 1024
DEPTH = 2
A_WIDTH = 512
B_WIDTH = 512
A_HEADS = 4
A_V_HEAD = 128
A_QK_HEAD = 64
B_HEAD = 64
B_Q_HEADS = 8
B_KV_HEADS = 2
GRID_W = 64
ROPE_THETA = 10000.0
NUM_BUCKETS = 32
MAX_DISTANCE = 128
EPS = 1e-6
LOG2E = math.log2(math.e)
Q_SCALE = (A_QK_HEAD ** -0.5) * LOG2E

LANES = 128
SUBLANES = 8
ROW_TILE = 512
Q_TILE = 256
K_CHUNK = 256
BAND_CHUNKS = 3
BIAS_SLOTS = 16
BIAS_TERMS = 3
VMEM_LIMIT = 48 * 1024 * 1024

_AQ, _AK, _AV, _AG, _BQ, _BK, _BV, _BG = 0, 512, 1024, 1536, 2048, 2560, 2688, 2816
_T_AQ, _T_AV, _T_BQ, _T_BK, _T_BV, _T_END = 0, 512, 1024, 1536, 1664, 1792
_N_AK, _N_AG, _N_BG, _N_END = 0, 512, 1024, 1536


def _t5_bucket(rel):
    nb = NUM_BUCKETS // 2
    max_exact = nb // 2
    n = jnp.abs(rel)
    large = max_exact + (jnp.log(jnp.maximum(n, 1).astype(jnp.float32) / max_exact)
                         / math.log(MAX_DISTANCE / max_exact) * (nb - max_exact)).astype(jnp.int32)
    large = jnp.minimum(large, nb - 1)
    return jnp.where(rel > 0, nb, 0) + jnp.where(n < max_exact, n, large)


def _rope_tables_t(n):
    rows = n // GRID_W
    row = jnp.repeat(jnp.arange(rows, dtype=jnp.float32), GRID_W)
    col = jnp.tile(jnp.arange(GRID_W, dtype=jnp.float32), rows)
    axis_dim = B_HEAD // 2
    inv = ROPE_THETA ** (-jnp.arange(0, axis_dim, 2, dtype=jnp.float32) / axis_dim)
    ang = jnp.concatenate([row[:, None] * inv, col[:, None] * inv], axis=-1)
    return jnp.cos(ang).T, jnp.sin(ang).T


def _cparams(n_axes):
    return pltpu.CompilerParams(dimension_semantics=("arbitrary",) * n_axes,
                                vmem_limit_bytes=VMEM_LIMIT)


def _bias_slot_mask(slot_index, chunk):
    in_terms = jnp.right_shift(jnp.bitwise_and(slot_index, A_QK_HEAD - 1), 4) < BIAS_TERMS
    return jnp.logical_and(in_terms, jnp.bitwise_and(slot_index, BIAS_SLOTS - 1) == chunk)


BAND_TABLE = 2 * Q_TILE


def _band_kernel(rb_ref, idx_ref, out_ref):
    h = pl.program_id(0)
    for d in range(BAND_CHUNKS):
        idx = jnp.broadcast_to(idx_ref[d], (SUBLANES, BAND_TABLE))
        vals = jnp.zeros(idx.shape, jnp.float32)
        for b in range(NUM_BUCKETS):
            vals = jnp.where(idx == b, rb_ref[b, h] * LOG2E, vals)

        def body(jb, carry, vals=vals, d=d):
            j0 = pl.multiple_of(jb * SUBLANES, SUBLANES)
            rolled = pltpu.roll(vals, j0 + (Q_TILE + 1), 1, stride=1, stride_axis=0)
            out_ref[0, d, pl.ds(j0, SUBLANES), :] = rolled[:, :Q_TILE]
            return carry

        lax.fori_loop(0, K_CHUNK // SUBLANES, body, 0)


def _band_call(rel_bias):
    d = jnp.arange(BAND_CHUNKS, dtype=jnp.int32)[:, None, None]
    c = jnp.arange(BAND_TABLE, dtype=jnp.int32)[None, None, :]
    idx = _t5_bucket((d - 1) * K_CHUNK + (Q_TILE - 1) - c)
    idx = jnp.take(jnp.arange(NUM_BUCKETS, dtype=jnp.int32), idx)
    return pl.pallas_call(
        _band_kernel,
        grid=(A_HEADS,),
        in_specs=[pl.BlockSpec(memory_space=pltpu.SMEM),
                  pl.BlockSpec((BAND_CHUNKS, 1, BAND_TABLE), lambda h: (0, 0, 0))],
        out_specs=pl.BlockSpec((1, BAND_CHUNKS, K_CHUNK, Q_TILE), lambda h: (h, 0, 0, 0)),
        out_shape=jax.ShapeDtypeStruct((A_HEADS, BAND_CHUNKS, K_CHUNK, Q_TILE), jnp.float32),
        compiler_params=_cparams(1),
        name="bias_band",
    )(rel_bias, idx)


def _proj_kernel(x_ref, g_ref, wn_ref, wt_ref, cos_ref, sin_ref, qg_ref, kg_ref,
                 qa_ref, ka_ref, va_ref, sa_ref, qb_ref, kb_ref, vb_ref, sb_ref,
                 *, tiles_per_seq):
    f32, bf16 = jnp.float32, jnp.bfloat16
    x = x_ref[...]
    ms = jnp.mean(x * x, axis=-1, keepdims=True)
    h = (x * lax.rsqrt(ms + EPS) * g_ref[...]).astype(bf16)

    def nat(lo, hi):
        return jnp.dot(h, wn_ref[:, lo:hi], preferred_element_type=f32)

    def silu(v):
        return v * jax.nn.sigmoid(v)

    pt = lax.dot_general(wt_ref[...], h, (((1,), (1,)), ((), ())), preferred_element_type=f32)
    qa_ref[...] = (pt[_T_AQ:_T_AV] * Q_SCALE).astype(bf16)
    va_ref[...] = pt[_T_AV:_T_BQ].astype(bf16)
    vb_ref[...] = pt[_T_BV:_T_END].astype(bf16)

    cos = cos_ref[...]
    sin = sin_ref[...]
    half = B_HEAD // 2

    def norm_rope(xt, g):
        msq = jnp.mean(xt * xt, axis=0, keepdims=True)
        y = xt * lax.rsqrt(msq + EPS) * g
        e, o = y[:half], y[half:]
        return jnp.concatenate([e * cos - o * sin, e * sin + o * cos], axis=0)

    qg = qg_ref[...]
    qn = jnp.concatenate([norm_rope(pt[_T_BQ + i * B_HEAD:_T_BQ + (i + 1) * B_HEAD], qg)
                          for i in range(B_Q_HEADS)], axis=0) * Q_SCALE
    qb_ref[...] = qn.astype(bf16)
    kg = kg_ref[...]
    kn = jnp.concatenate([norm_rope(pt[_T_BK + i * B_HEAD:_T_BK + (i + 1) * B_HEAD], kg)
                          for i in range(B_KV_HEADS)], axis=0)
    kb_ref[...] = kn.T.astype(bf16)

    sa_ref[...] = silu(nat(_N_AG, _N_BG)).astype(bf16)
    sb_ref[...] = silu(nat(_N_BG, _N_END)).astype(bf16)

    ak = nat(_N_AK, _N_AG)
    lane = lax.broadcasted_iota(jnp.int32, (ROW_TILE, LANES), 1)
    row = lax.broadcasted_iota(jnp.int32, (ROW_TILE, LANES), 0)
    chunk = ((pl.program_id(0) % tiles_per_seq) * (ROW_TILE // K_CHUNK)
             + jnp.right_shift(row, K_CHUNK.bit_length() - 1))
    hot = jnp.where(_bias_slot_mask(lane, chunk), 1.0, 0.0)
    low = lane < A_QK_HEAD
    for hd in range(A_HEADS):
        akh = ak[:, hd * LANES:(hd + 1) * LANES]
        ka_ref[:, 2 * hd * LANES:(2 * hd + 1) * LANES] = jnp.where(low, akh, hot).astype(bf16)
        ka_ref[:, (2 * hd + 1) * LANES:(2 * hd + 2) * LANES] = jnp.where(low, hot, akh).astype(bf16)


def _proj_call(xf, g_pre, wn, wt, cos_t, sin_t, qg, kg, seq):
    m = xf.shape[0]
    tiles_per_seq = seq // ROW_TILE
    bf16 = jnp.bfloat16
    row = lambda i: (i, 0)
    col = lambda i: (0, i)
    const = lambda i: (0, 0)
    pos = lambda i: (0, i % tiles_per_seq)
    return pl.pallas_call(
        functools.partial(_proj_kernel, tiles_per_seq=tiles_per_seq),
        grid=(m // ROW_TILE,),
        in_specs=[pl.BlockSpec((ROW_TILE, D_MODEL), row),
                  pl.BlockSpec((1, D_MODEL), const),
                  pl.BlockSpec(wn.shape, const),
                  pl.BlockSpec(wt.shape, const),
                  pl.BlockSpec((B_HEAD // 2, ROW_TILE), pos),
                  pl.BlockSpec((B_HEAD // 2, ROW_TILE), pos),
                  pl.BlockSpec((B_HEAD, 1), const),
                  pl.BlockSpec((B_HEAD, 1), const)],
        out_specs=[pl.BlockSpec((A_WIDTH, ROW_TILE), col),
                   pl.BlockSpec((ROW_TILE, 2 * A_WIDTH), row),
                   pl.BlockSpec((A_WIDTH, ROW_TILE), col),
                   pl.BlockSpec((ROW_TILE, A_WIDTH), row),
                   pl.BlockSpec((B_WIDTH, ROW_TILE), col),
                   pl.BlockSpec((ROW_TILE, B_KV_HEADS * B_HEAD), row),
                   pl.BlockSpec((B_KV_HEADS * B_HEAD, ROW_TILE), col),
                   pl.BlockSpec((ROW_TILE, B_WIDTH), row)],
        out_shape=[jax.ShapeDtypeStruct((A_WIDTH, m), bf16),
                   jax.ShapeDtypeStruct((m, 2 * A_WIDTH), bf16),
                   jax.ShapeDtypeStruct((A_WIDTH, m), bf16),
                   jax.ShapeDtypeStruct((m, A_WIDTH), bf16),
                   jax.ShapeDtypeStruct((B_WIDTH, m), bf16),
                   jax.ShapeDtypeStruct((m, B_KV_HEADS * B_HEAD), bf16),
                   jax.ShapeDtypeStruct((B_KV_HEADS * B_HEAD, m), bf16),
                   jax.ShapeDtypeStruct((m, B_WIDTH), bf16)],
        compiler_params=_cparams(1),
        name="in_proj",
    )(xf, g_pre, wn, wt, cos_t, sin_t, qg, kg)


def _fold_rows(x, op):
    parts = [x[g * SUBLANES:(g + 1) * SUBLANES] for g in range(x.shape[0] // SUBLANES)]
    while len(parts) > 1:
        parts = [op(parts[i], parts[i + 1]) for i in range(0, len(parts), 2)]
    return parts[0]


def _pipeline_tiles(n_tiles, nq):
    t = pl.program_id(0)
    tile1 = jnp.minimum(t, n_tiles - 1)
    return t, tile1, tile1 % nq


def _zero_first_previous(t, s_odd, m_odd):
    @pl.when(t == 0)
    def _():
        s_odd[...] = jnp.zeros(s_odd.shape, jnp.float32)
        m_odd[...] = jnp.zeros(m_odd.shape, jnp.float32)


def _by_parity(t, step, s_even, m_even, s_odd, m_odd):
    @pl.when(t % 2 == 0)
    def _():
        step(s_even, m_even, s_odd, m_odd)

    @pl.when(t % 2 == 1)
    def _():
        step(s_odd, m_odd, s_even, m_even)


def _attn_gqa_kernel(q_ref, k_ref, v_ref, gate_ref, o_ref,
                     w_ref, s_even, s_odd, m_even, m_odd, *, n_chunks, nq, n_tiles):
    f32, bf16 = jnp.float32, jnp.bfloat16
    t, _, _ = _pipeline_tiles(n_tiles, nq)
    _zero_first_previous(t, s_odd, m_odd)

    def step(s1_ref, m1_ref, s2_ref, m2_ref):
        top = lax.broadcasted_iota(jnp.int32, (LANES, Q_TILE), 0) < B_HEAD
        qt = q_ref[...].astype(f32)
        w_ref[0] = jnp.where(top, qt, 0.0).astype(bf16)
        w_ref[1] = jnp.where(top, 0.0, qt).astype(bf16)

        m_row = [jnp.max(m2_ref[st], axis=0, keepdims=True) for st in range(2)]

        m_acc = [None, None]
        l_acc = [None, None]
        o_acc = [None, None]
        for c in range(n_chunks):
            keys = slice(c * K_CHUNK, (c + 1) * K_CHUNK)
            for st in range(2):
                s = jnp.dot(k_ref[keys, :], w_ref[st], preferred_element_type=f32)
                s1_ref[c, st] = s
                cm = _fold_rows(s, jnp.maximum)
                m_acc[st] = cm if c == 0 else jnp.maximum(m_acc[st], cm)
            for st in range(2):
                p = jnp.exp2(s2_ref[c, st] - m_row[st])
                ps = _fold_rows(p, jnp.add)
                l_acc[st] = ps if c == 0 else l_acc[st] + ps
                part = jnp.dot(v_ref[:, keys], p.astype(bf16), preferred_element_type=f32)
                o_acc[st] = part if c == 0 else o_acc[st] + part
        for st in range(2):
            m1_ref[st] = m_acc[st]

        l_row = [jnp.sum(l_acc[st], axis=0, keepdims=True) for st in range(2)]
        ot = jnp.concatenate([o_acc[0][:B_HEAD] / l_row[0], o_acc[1][B_HEAD:] / l_row[1]], axis=0)
        o_ref[...] = (ot.T * gate_ref[...].astype(f32)).astype(o_ref.dtype)

    _by_parity(t, step, s_even, m_even, s_odd, m_odd)


def _attn_diff_kernel(rb_ref, lamp_ref, g_ref, q_ref, k_ref, v_ref, gate_ref, band_ref, o_ref,
                      w_ref, s_even, s_odd, m_even, m_odd, *, lam_init, n_chunks, nq, n_tiles):
    f32, bf16 = jnp.float32, jnp.bfloat16
    t, tile1, qi1 = _pipeline_tiles(n_tiles, nq)
    head1 = (tile1 // nq) % A_HEADS
    _zero_first_previous(t, s_odd, m_odd)

    def step(s1_ref, m1_ref, s2_ref, m2_ref):
        row = lax.broadcasted_iota(jnp.int32, (LANES, Q_TILE), 0)
        top = row < A_QK_HEAD
        slot = jnp.bitwise_and(row, BIAS_SLOTS - 1)
        term = jnp.right_shift(jnp.bitwise_and(row, A_QK_HEAD - 1), 4)
        left = rb_ref[NUM_BUCKETS // 2 - 1, head1] * LOG2E
        right = rb_ref[NUM_BUCKETS - 1, head1] * LOG2E
        const = jnp.where(jnp.abs(slot - qi1) <= 1, 0.0, jnp.where(slot < qi1, left, right))
        t0 = const.astype(bf16).astype(f32)
        t1 = (const - t0).astype(bf16).astype(f32)
        t2 = (const - t0 - t1).astype(bf16).astype(f32)
        bias_rows = jnp.where(term == 0, t0, jnp.where(term == 1, t1,
                                                      jnp.where(term == 2, t2, 0.0)))
        qt = q_ref[...].astype(f32)
        w_ref[0] = jnp.where(top, qt, bias_rows).astype(bf16)
        w_ref[1] = jnp.where(top, bias_rows, qt).astype(bf16)

        m_row = [jnp.max(m2_ref[st], axis=0, keepdims=True) for st in range(2)]

        m_acc = [None, None]
        l_acc = [None, None]
        o_acc = [None, None]
        for i in range(n_chunks):
            c1 = qi1 - 1 + i
            j1 = jnp.bitwise_and(c1, n_chunks - 1)
            key0 = pl.multiple_of(j1 * K_CHUNK, K_CHUNK)
            if i < BAND_CHUNKS:
                in_range = jnp.logical_and(c1 >= 0, c1 < n_chunks)
                band = jnp.where(in_range, band_ref[0, i], 0.0)
            for st in range(2):
                s = jnp.dot(k_ref[pl.ds(key0, K_CHUNK), st * LANES:(st + 1) * LANES], w_ref[st],
                            preferred_element_type=f32)
                if i < BAND_CHUNKS:
                    s = s + band
                s1_ref[j1, st] = s
                cm = _fold_rows(s, jnp.maximum)
                m_acc[st] = cm if i == 0 else jnp.maximum(m_acc[st], cm)
            for st in range(2):
                p = jnp.exp2(s2_ref[i, st] - m_row[st])
                ps = _fold_rows(p, jnp.add)
                l_acc[st] = ps if i == 0 else l_acc[st] + ps
                part = jnp.dot(v_ref[:, i * K_CHUNK:(i + 1) * K_CHUNK], p.astype(bf16),
                               preferred_element_type=f32)
                o_acc[st] = part if i == 0 else o_acc[st] + part
        for st in range(2):
            m1_ref[st] = m_acc[st]

        l_row = [jnp.sum(l_acc[st], axis=0, keepdims=True) for st in range(2)]
        lp = lamp_ref[...]
        lam = (jnp.exp(jnp.sum(lp[0:1] * lp[1:2], axis=-1, keepdims=True))
               - jnp.exp(jnp.sum(lp[2:3] * lp[3:4], axis=-1, keepdims=True)) + lam_init)
        o = (o_acc[0] / l_row[0] - lam * (o_acc[1] / l_row[1])).T
        ms = jnp.mean(o * o, axis=-1, keepdims=True)
        o = (o * lax.rsqrt(ms + EPS) * g_ref[...]) * (1.0 - lam_init)
        o_ref[...] = (o * gate_ref[...].astype(f32)).astype(o_ref.dtype)

    _by_parity(t, step, s_even, m_even, s_odd, m_odd)


def _attn_call(kernel_fn, operands, prefix_specs, q_arr, k_arr, v_arr, gate_arr, band_arr,
               *, n_pairs, per_pair, batch, seq, name):
    m = gate_arr.shape[0]
    nq = seq // Q_TILE
    n_chunks = seq // K_CHUNK
    n_tiles = batch * n_pairs * nq
    k_lanes = k_arr.shape[1] // (n_pairs if per_pair else 1)

    def decode(tile):
        return tile // (n_pairs * nq), (tile // nq) % n_pairs, tile % nq

    def stage1(t):
        return decode(jnp.minimum(t, n_tiles - 1))

    def stage2(t):
        return decode(jnp.maximum(t - 1, 0))

    def q_map(t):
        b, p, i = stage1(t)
        return p, b * nq + i

    def k_map(t):
        b, p, _ = stage1(t)
        return b, p if per_pair else 0

    def v_map(t):
        b, p, _ = stage2(t)
        return p if per_pair else 0, b

    def out_map(t):
        b, p, i = stage2(t)
        return b * nq + i, p

    in_specs = list(prefix_specs) + [
        pl.BlockSpec((LANES, Q_TILE), q_map),
        pl.BlockSpec((seq, k_lanes), k_map),
        pl.BlockSpec((LANES, seq), v_map),
        pl.BlockSpec((Q_TILE, LANES), out_map)]
    tail = ()
    if band_arr is not None:
        in_specs.append(pl.BlockSpec((1, BAND_CHUNKS, K_CHUNK, Q_TILE),
                                     lambda t: (stage1(t)[1], 0, 0, 0)))
        tail = (band_arr,)
    scores = pltpu.VMEM((n_chunks, 2, K_CHUNK, Q_TILE), jnp.float32)
    key_max = pltpu.VMEM((2, SUBLANES, Q_TILE), jnp.float32)
    scratch = [pltpu.VMEM((2, LANES, Q_TILE), jnp.bfloat16),
               scores, scores,
               key_max, key_max]
    return pl.pallas_call(
        functools.partial(kernel_fn, n_chunks=n_chunks, nq=nq, n_tiles=n_tiles),
        grid=(n_tiles + 1,),
        in_specs=in_specs,
        out_specs=pl.BlockSpec((Q_TILE, LANES), out_map),
        out_shape=jax.ShapeDtypeStruct((m, n_pairs * LANES), jnp.bfloat16),
        scratch_shapes=scratch,
        compiler_params=_cparams(1),
        name=name,
    )(*operands, q_arr, k_arr, v_arr, gate_arr, *tail)


def _attn_diff_call(rel_bias, lam_params, subln_g, qa, ka, va, sa, band, lam_init, batch, seq):
    n_chunks = seq // K_CHUNK
    assert n_chunks <= BIAS_SLOTS and n_chunks & (n_chunks - 1) == 0
    prefix = [pl.BlockSpec(memory_space=pltpu.SMEM),
              pl.BlockSpec((4, A_QK_HEAD), lambda t: (0, 0)),
              pl.BlockSpec((1, A_V_HEAD), lambda t: (0, 0))]
    return _attn_call(functools.partial(_attn_diff_kernel, lam_init=lam_init),
                      (rel_bias, lam_params, subln_g), prefix, qa, ka, va, sa, band,
                      n_pairs=A_HEADS, per_pair=True, batch=batch, seq=seq, name="attn_diff")


def _attn_gqa_call(qb, kb, vb, sb, batch, seq):
    return _attn_call(_attn_gqa_kernel, (), [], qb, kb, vb, sb, None,
                      n_pairs=B_Q_HEADS // 2, per_pair=False, batch=batch, seq=seq,
                      name="attn_gqa")


def _out_kernel(ya_ref, yb_ref, wa_ref, wb_ref, x_ref, g_ref, o_ref):
    f32 = jnp.float32
    y = (jnp.dot(ya_ref[...], wa_ref[...], preferred_element_type=f32)
         + jnp.dot(yb_ref[...], wb_ref[...], preferred_element_type=f32))
    ms = jnp.mean(y * y, axis=-1, keepdims=True)
    o_ref[...] = x_ref[...] + y * lax.rsqrt(ms + EPS) * g_ref[...]


def _out_call(ya, yb, wa, wb, xf, g_post):
    m = xf.shape[0]
    row = lambda i: (i, 0)
    const = lambda i: (0, 0)
    return pl.pallas_call(
        _out_kernel,
        grid=(m // ROW_TILE,),
        in_specs=[pl.BlockSpec((ROW_TILE, A_WIDTH), row),
                  pl.BlockSpec((ROW_TILE, B_WIDTH), row),
                  pl.BlockSpec((A_WIDTH, D_MODEL), const),
                  pl.BlockSpec((B_WIDTH, D_MODEL), const),
                  pl.BlockSpec((ROW_TILE, D_MODEL), row),
                  pl.BlockSpec((1, D_MODEL), const)],
        out_specs=pl.BlockSpec((ROW_TILE, D_MODEL), row),
        out_shape=jax.ShapeDtypeStruct((m, D_MODEL), jnp.float32),
        compiler_params=_cparams(1),
        name="out_proj",
    )(ya, yb, wa, wb, xf, g_post)


def _layer_weights(w_in_l, w_out_l, q_norm_g_l, k_norm_g_l):
    bf16 = jnp.bfloat16
    d = w_in_l.shape[0]
    per_group = B_Q_HEADS // B_KV_HEADS
    half = B_HEAD // 2
    bq = w_in_l[:, _BQ:_BK].reshape(d, B_KV_HEADS, per_group, half, 2)
    bq = bq.transpose(0, 2, 1, 4, 3).reshape(d, B_WIDTH)
    bk = w_in_l[:, _BK:_BV].reshape(d, B_KV_HEADS, half, 2)
    bk = bk.transpose(0, 1, 3, 2).reshape(d, B_KV_HEADS * B_HEAD)
    bg = w_in_l[:, _BG:].reshape(d, B_KV_HEADS, per_group, B_HEAD)
    bg = bg.transpose(0, 2, 1, 3).reshape(d, B_WIDTH)
    wn = jnp.concatenate([w_in_l[:, _AK:_AV], w_in_l[:, _AG:_BQ], bg], axis=1).astype(bf16)
    wt = jnp.concatenate([w_in_l[:, _AQ:_AK], w_in_l[:, _AV:_AG], bq, bk, w_in_l[:, _BV:_BG]],
                         axis=1).astype(bf16).T
    wa = w_out_l[:A_WIDTH].astype(bf16)
    wb = w_out_l[A_WIDTH:].reshape(B_KV_HEADS, per_group, B_HEAD, -1)
    wb = wb.transpose(1, 0, 2, 3).reshape(B_WIDTH, -1).astype(bf16)
    qg = q_norm_g_l.reshape(half, 2).T.reshape(B_HEAD, 1)
    kg = k_norm_g_l.reshape(half, 2).T.reshape(B_HEAD, 1)
    return wn, wt, wa, wb, qg, kg


def kernel(x, rel_bias, pre_norm_g, w_in, diff_lambda, diff_subln_g, q_norm_g, k_norm_g,
           w_out, post_norm_g):
    batch, seq, d_model = x.shape
    xf = x.reshape(batch * seq, d_model)
    band = _band_call(rel_bias)
    cos_t, sin_t = _rope_tables_t(seq)
    for l in range(DEPTH):
        lam_init = 0.8 - 0.6 * math.exp(-0.3 * l)
        wn, wt, wa, wb, qg, kg = _layer_weights(w_in[l], w_out[l], q_norm_g[l], k_norm_g[l])
        qa, ka, va, sa, qb, kb, vb, sb = _proj_call(
            xf, pre_norm_g[l].reshape(1, d_model), wn, wt, cos_t, sin_t, qg, kg, seq)
        ya = _attn_diff_call(rel_bias, diff_lambda[l], diff_subln_g[l].reshape(1, A_V_HEAD),
                             qa, ka, va, sa, band, lam_init, batch, seq)
        yb = _attn_gqa_call(qb, kb, vb, sb, batch, seq)
        xf = _out_call(ya, yb, wa, wb, xf, post_norm_g[l].reshape(1, d_model))
    return xf.reshape(batch, seq, d_model)
```

```python
import functools
import math

import jax
import jax.numpy as jnp
from jax import lax
from jax.experimental import pallas as pl
from jax.experimental.pallas import tpu as pltpu

D_MODEL = 1024
DEPTH = 2
A_WIDTH = 512
B_WIDTH = 512
A_HEADS = 4
A_V_HEAD = 128
A_QK_HEAD = 64
B_HEAD = 64
B_Q_HEADS = 8
B_KV_HEADS = 2
GRID_W = 64
ROPE_THETA = 10000.0
NUM_BUCKETS = 32
MAX_DISTANCE = 128
EPS = 1e-6
LOG2E = math.log2(math.e)
Q_SCALE = (A_QK_HEAD ** -0.5) * LOG2E

LANES = 128
SUBLANES = 8
ROW_TILE = 512
Q_TILE = 256
K_CHUNK = 256
BAND_CHUNKS = 3
BIAS_SLOTS = 16
BIAS_TERMS = 3
VMEM_LIMIT = 48 * 1024 * 1024

_AQ, _AK, _AV, _AG, _BQ, _BK, _BV, _BG = 0, 512, 1024, 1536, 2048, 2560, 2688, 2816
_T_AQ, _T_AV, _T_BQ, _T_BK, _T_BV, _T_END = 0, 512, 1024, 1536, 1664, 1792
_N_AK, _N_AG, _N_BG, _N_END = 0, 512, 1024, 1536


def _t5_bucket(rel):
    nb = NUM_BUCKETS // 2
    max_exact = nb // 2
    n = jnp.abs(rel)
    large = max_exact + (jnp.log(jnp.maximum(n, 1).astype(jnp.float32) / max_exact)
                         / math.log(MAX_DISTANCE / max_exact) * (nb - max_exact)).astype(jnp.int32)
    large = jnp.minimum(large, nb - 1)
    return jnp.where(rel > 0, nb, 0) + jnp.where(n < max_exact, n, large)


def _rope_tables_t(n):
    rows = n // GRID_W
    axis_dim = B_HEAD // 2
    inv = ROPE_THETA ** (-jnp.arange(0, axis_dim, 2, dtype=jnp.float32) / axis_dim)
    row_ang = (jnp.arange(rows, dtype=jnp.float32)[:, None] * inv).T
    col_ang = (jnp.arange(GRID_W, dtype=jnp.float32)[:, None] * inv).T

    def expand(fn):
        by_row = jnp.repeat(fn(row_ang), GRID_W, axis=1)
        by_col = jnp.tile(fn(col_ang), (1, rows))
        return jnp.concatenate([by_row, by_col], axis=0)

    return expand(jnp.cos), expand(jnp.sin)


def _cparams(n_axes):
    return pltpu.CompilerParams(dimension_semantics=("arbitrary",) * n_axes,
                                vmem_limit_bytes=VMEM_LIMIT)


def _bias_slot_mask(slot_index, chunk):
    in_terms = jnp.right_shift(jnp.bitwise_and(slot_index, A_QK_HEAD - 1), 4) < BIAS_TERMS
    return jnp.logical_and(in_terms, jnp.bitwise_and(slot_index, BIAS_SLOTS - 1) == chunk)


BAND_TABLE = 2 * Q_TILE


def _band_kernel(rb_ref, idx_ref, out_ref):
    h = pl.program_id(0)
    for d in range(BAND_CHUNKS):
        idx = jnp.broadcast_to(idx_ref[d], (SUBLANES, BAND_TABLE))
        vals = jnp.zeros(idx.shape, jnp.float32)
        for b in range(NUM_BUCKETS):
            vals = jnp.where(idx == b, rb_ref[b, h] * LOG2E, vals)

        for j0 in range(0, K_CHUNK, SUBLANES):
            rolled = pltpu.roll(vals, (j0 + Q_TILE + 1) % BAND_TABLE, 1, stride=1, stride_axis=0)
            out_ref[0, d, j0:j0 + SUBLANES, :] = rolled[:, :Q_TILE]


def _band_call(rel_bias):
    d = jnp.arange(BAND_CHUNKS, dtype=jnp.int32)[:, None, None]
    c = jnp.arange(BAND_TABLE, dtype=jnp.int32)[None, None, :]
    idx = _t5_bucket((d - 1) * K_CHUNK + (Q_TILE - 1) - c)
    idx = jnp.take(jnp.arange(NUM_BUCKETS, dtype=jnp.int32), idx)
    return pl.pallas_call(
        _band_kernel,
        grid=(A_HEADS,),
        in_specs=[pl.BlockSpec(memory_space=pltpu.SMEM),
                  pl.BlockSpec((BAND_CHUNKS, 1, BAND_TABLE), lambda h: (0, 0, 0))],
        out_specs=pl.BlockSpec((1, BAND_CHUNKS, K_CHUNK, Q_TILE), lambda h: (h, 0, 0, 0)),
        out_shape=jax.ShapeDtypeStruct((A_HEADS, BAND_CHUNKS, K_CHUNK, Q_TILE), jnp.float32),
        compiler_params=_cparams(1),
        name="bias_band",
    )(rel_bias, idx)


def _proj_kernel(x_ref, g_ref, wn_ref, wt_ref, cos_ref, sin_ref, qg_ref, kg_ref,
                 qa_ref, ka_ref, va_ref, sa_ref, qb_ref, kb_ref, vb_ref, sb_ref,
                 *, tiles_per_seq):
    f32, bf16 = jnp.float32, jnp.bfloat16
    x = x_ref[...]
    ms = jnp.mean(x * x, axis=-1, keepdims=True)
    h = (x * lax.rsqrt(ms + EPS) * g_ref[...]).astype(bf16)

    def nat(lo, hi):
        return jnp.dot(h, wn_ref[:, lo:hi], preferred_element_type=f32)

    def silu(v):
        return v * jax.nn.sigmoid(v)

    pt = lax.dot_general(wt_ref[...], h, (((1,), (1,)), ((), ())), preferred_element_type=f32)
    qa_ref[...] = (pt[_T_AQ:_T_AV] * Q_SCALE).astype(bf16)
    va_ref[...] = pt[_T_AV:_T_BQ].astype(bf16)
    vb_ref[...] = pt[_T_BV:_T_END].astype(bf16)

    cos = cos_ref[...]
    sin = sin_ref[...]
    half = B_HEAD // 2

    def norm_rope(xt, g):
        msq = jnp.mean(xt * xt, axis=0, keepdims=True)
        y = xt * lax.rsqrt(msq + EPS) * g
        e, o = y[:half], y[half:]
        return jnp.concatenate([e * cos - o * sin, e * sin + o * cos], axis=0)

    qg = qg_ref[...]
    qn = jnp.concatenate([norm_rope(pt[_T_BQ + i * B_HEAD:_T_BQ + (i + 1) * B_HEAD], qg)
                          for i in range(B_Q_HEADS)], axis=0) * Q_SCALE
    qb_ref[...] = qn.astype(bf16)
    kg = kg_ref[...]
    kn = jnp.concatenate([norm_rope(pt[_T_BK + i * B_HEAD:_T_BK + (i + 1) * B_HEAD], kg)
                          for i in range(B_KV_HEADS)], axis=0)
    kb_ref[...] = kn.T.astype(bf16)

    sa_ref[...] = silu(nat(_N_AG, _N_BG)).astype(bf16)
    sb_ref[...] = silu(nat(_N_BG, _N_END)).astype(bf16)

    ak = nat(_N_AK, _N_AG)
    lane = lax.broadcasted_iota(jnp.int32, (ROW_TILE, LANES), 1)
    row = lax.broadcasted_iota(jnp.int32, (ROW_TILE, LANES), 0)
    chunk = ((pl.program_id(0) % tiles_per_seq) * (ROW_TILE // K_CHUNK)
             + jnp.right_shift(row, K_CHUNK.bit_length() - 1))
    hot = jnp.where(_bias_slot_mask(lane, chunk), 1.0, 0.0)
    low = lane < A_QK_HEAD
    for hd in range(A_HEADS):
        akh = ak[:, hd * LANES:(hd + 1) * LANES]
        ka_ref[:, 2 * hd * LANES:(2 * hd + 1) * LANES] = jnp.where(low, akh, hot).astype(bf16)
        ka_ref[:, (2 * hd + 1) * LANES:(2 * hd + 2) * LANES] = jnp.where(low, hot, akh).astype(bf16)


def _proj_call(xf, g_pre, wn, wt, cos_t, sin_t, qg, kg, seq):
    m = xf.shape[0]
    tiles_per_seq = seq // ROW_TILE
    bf16 = jnp.bfloat16
    row = lambda i: (i, 0)
    col = lambda i: (0, i)
    const = lambda i: (0, 0)
    pos = lambda i: (0, i % tiles_per_seq)
    return pl.pallas_call(
        functools.partial(_proj_kernel, tiles_per_seq=tiles_per_seq),
        grid=(m // ROW_TILE,),
        in_specs=[pl.BlockSpec((ROW_TILE, D_MODEL), row),
                  pl.BlockSpec((1, D_MODEL), const),
                  pl.BlockSpec(wn.shape, const),
                  pl.BlockSpec(wt.shape, const),
                  pl.BlockSpec((B_HEAD // 2, ROW_TILE), pos),
                  pl.BlockSpec((B_HEAD // 2, ROW_TILE), pos),
                  pl.BlockSpec((B_HEAD, 1), const),
                  pl.BlockSpec((B_HEAD, 1), const)],
        out_specs=[pl.BlockSpec((A_WIDTH, ROW_TILE), col),
                   pl.BlockSpec((ROW_TILE, 2 * A_WIDTH), row),
                   pl.BlockSpec((A_WIDTH, ROW_TILE), col),
                   pl.BlockSpec((ROW_TILE, A_WIDTH), row),
                   pl.BlockSpec((B_WIDTH, ROW_TILE), col),
                   pl.BlockSpec((ROW_TILE, B_KV_HEADS * B_HEAD), row),
                   pl.BlockSpec((B_KV_HEADS * B_HEAD, ROW_TILE), col),
                   pl.BlockSpec((ROW_TILE, B_WIDTH), row)],
        out_shape=[jax.ShapeDtypeStruct((A_WIDTH, m), bf16),
                   jax.ShapeDtypeStruct((m, 2 * A_WIDTH), bf16),
                   jax.ShapeDtypeStruct((A_WIDTH, m), bf16),
                   jax.ShapeDtypeStruct((m, A_WIDTH), bf16),
                   jax.ShapeDtypeStruct((B_WIDTH, m), bf16),
                   jax.ShapeDtypeStruct((m, B_KV_HEADS * B_HEAD), bf16),
                   jax.ShapeDtypeStruct((B_KV_HEADS * B_HEAD, m), bf16),
                   jax.ShapeDtypeStruct((m, B_WIDTH), bf16)],
        compiler_params=_cparams(1),
        name="in_proj",
    )(xf, g_pre, wn, wt, cos_t, sin_t, qg, kg)


def _fold_rows(x, op):
    parts = [x[g * SUBLANES:(g + 1) * SUBLANES] for g in range(x.shape[0] // SUBLANES)]
    while len(parts) > 1:
        parts = [op(parts[i], parts[i + 1]) for i in range(0, len(parts), 2)]
    return parts[0]


def _pipeline_tiles(n_tiles, nq):
    t = pl.program_id(0)
    tile1 = jnp.minimum(t, n_tiles - 1)
    return t, tile1, tile1 % nq


def _zero_first_previous(t, s_odd, m_odd):
    @pl.when(t == 0)
    def _():
        s_odd[...] = jnp.zeros(s_odd.shape, jnp.float32)
        m_odd[...] = jnp.zeros(m_odd.shape, jnp.float32)


def _by_parity(t, step, s_even, m_even, s_odd, m_odd):
    @pl.when(t % 2 == 0)
    def _():
        step(s_even, m_even, s_odd, m_odd)

    @pl.when(t % 2 == 1)
    def _():
        step(s_odd, m_odd, s_even, m_even)


def _attn_gqa_kernel(q_ref, k_ref, v_ref, gate_ref, o_ref,
                     w_ref, s_even, s_odd, m_even, m_odd, *, n_chunks, nq, n_tiles):
    f32, bf16 = jnp.float32, jnp.bfloat16
    t, _, _ = _pipeline_tiles(n_tiles, nq)
    _zero_first_previous(t, s_odd, m_odd)

    def step(s1_ref, m1_ref, s2_ref, m2_ref):
        top = lax.broadcasted_iota(jnp.int32, (LANES, Q_TILE), 0) < B_HEAD
        qt = q_ref[...].astype(f32)
        w_ref[0] = jnp.where(top, qt, 0.0).astype(bf16)
        w_ref[1] = jnp.where(top, 0.0, qt).astype(bf16)

        m_row = [jnp.max(m2_ref[st], axis=0, keepdims=True) for st in range(2)]

        m_acc = [None, None]
        l_acc = [None, None]
        o_acc = [None, None]
        for c in range(n_chunks):
            keys = slice(c * K_CHUNK, (c + 1) * K_CHUNK)
            for st in range(2):
                s = jnp.dot(k_ref[keys, :], w_ref[st], preferred_element_type=f32)
                s1_ref[c, st] = s
                cm = _fold_rows(s, jnp.maximum)
                m_acc[st] = cm if c == 0 else jnp.maximum(m_acc[st], cm)
            for st in range(2):
                p = jnp.exp2(s2_ref[c, st] - m_row[st])
                ps = _fold_rows(p, jnp.add)
                l_acc[st] = ps if c == 0 else l_acc[st] + ps
                part = jnp.dot(v_ref[st * B_HEAD:(st + 1) * B_HEAD, keys], p.astype(bf16),
                               preferred_element_type=f32)
                o_acc[st] = part if c == 0 else o_acc[st] + part
        for st in range(2):
            m1_ref[st] = m_acc[st]

        l_row = [jnp.sum(l_acc[st], axis=0, keepdims=True) for st in range(2)]
        ot = jnp.concatenate([o_acc[0] / l_row[0], o_acc[1] / l_row[1]], axis=0)
        o_ref[...] = (ot.T * gate_ref[...].astype(f32)).astype(o_ref.dtype)

    _by_parity(t, step, s_even, m_even, s_odd, m_odd)


def _attn_diff_kernel(rb_ref, lamp_ref, g_ref, q_ref, k_ref, v_ref, gate_ref, band_ref, o_ref,
                      w_ref, s_even, s_odd, m_even, m_odd, *, lam_init, n_chunks, nq, n_tiles):
    f32, bf16 = jnp.float32, jnp.bfloat16
    t, tile1, qi1 = _pipeline_tiles(n_tiles, nq)
    head1 = (tile1 // nq) % A_HEADS
    _zero_first_previous(t, s_odd, m_odd)

    def step(s1_ref, m1_ref, s2_ref, m2_ref):
        row = lax.broadcasted_iota(jnp.int32, (LANES, Q_TILE), 0)
        top = row < A_QK_HEAD
        slot = jnp.bitwise_and(row, BIAS_SLOTS - 1)
        term = jnp.right_shift(jnp.bitwise_and(row, A_QK_HEAD - 1), 4)
        left = rb_ref[NUM_BUCKETS // 2 - 1, head1] * LOG2E
        right = rb_ref[NUM_BUCKETS - 1, head1] * LOG2E
        const = jnp.where(jnp.abs(slot - qi1) <= 1, 0.0, jnp.where(slot < qi1, left, right))
        t0 = const.astype(bf16).astype(f32)
        t1 = (const - t0).astype(bf16).astype(f32)
        t2 = (const - t0 - t1).astype(bf16).astype(f32)
        bias_rows = jnp.where(term == 0, t0, jnp.where(term == 1, t1,
                                                      jnp.where(term == 2, t2, 0.0)))
        qt = q_ref[...].astype(f32)
        w_ref[0] = jnp.where(top, qt, bias_rows).astype(bf16)
        w_ref[1] = jnp.where(top, bias_rows, qt).astype(bf16)

        m_row = [jnp.max(m2_ref[st], axis=0, keepdims=True) for st in range(2)]

        m_acc = [None, None]
        l_acc = [None, None]
        o_acc = [None, None]
        for i in range(n_chunks):
            c1 = qi1 - 1 + i
            j1 = jnp.bitwise_and(c1, n_chunks - 1)
            key0 = pl.multiple_of(j1 * K_CHUNK, K_CHUNK)
            if i < BAND_CHUNKS:
                in_range = jnp.logical_and(c1 >= 0, c1 < n_chunks)
                band = jnp.where(in_range, band_ref[0, i], 0.0)
            for st in range(2):
                s = jnp.dot(k_ref[pl.ds(key0, K_CHUNK), st * LANES:(st + 1) * LANES], w_ref[st],
                            preferred_element_type=f32)
                if i < BAND_CHUNKS:
                    s = s + band
                s1_ref[j1, st] = s
                cm = _fold_rows(s, jnp.maximum)
                m_acc[st] = cm if i == 0 else jnp.maximum(m_acc[st], cm)
            for st in range(2):
                p = jnp.exp2(s2_ref[i, st] - m_row[st])
                ps = _fold_rows(p, jnp.add)
                l_acc[st] = ps if i == 0 else l_acc[st] + ps
                part = jnp.dot(v_ref[:, i * K_CHUNK:(i + 1) * K_CHUNK], p.astype(bf16),
                               preferred_element_type=f32)
                o_acc[st] = part if i == 0 else o_acc[st] + part
        for st in range(2):
            m1_ref[st] = m_acc[st]

        l_row = [jnp.sum(l_acc[st], axis=0, keepdims=True) for st in range(2)]
        lp = lamp_ref[...]
        lam = (jnp.exp(jnp.sum(lp[0:1] * lp[1:2], axis=-1, keepdims=True))
               - jnp.exp(jnp.sum(lp[2:3] * lp[3:4], axis=-1, keepdims=True)) + lam_init)
        o = (o_acc[0] / l_row[0] - lam * (o_acc[1] / l_row[1])).T
        ms = jnp.mean(o * o, axis=-1, keepdims=True)
        o = (o * lax.rsqrt(ms + EPS) * g_ref[...]) * (1.0 - lam_init)
        o_ref[...] = (o * gate_ref[...].astype(f32)).astype(o_ref.dtype)

    _by_parity(t, step, s_even, m_even, s_odd, m_odd)


def _attn_call(kernel_fn, operands, prefix_specs, q_arr, k_arr, v_arr, gate_arr, band_arr,
               *, n_pairs, per_pair, batch, seq, name):
    m = gate_arr.shape[0]
    nq = seq // Q_TILE
    n_chunks = seq // K_CHUNK
    n_tiles = batch * n_pairs * nq
    k_lanes = k_arr.shape[1] // (n_pairs if per_pair else 1)

    def decode(tile):
        return tile // (n_pairs * nq), (tile // nq) % n_pairs, tile % nq

    def stage1(t):
        return decode(jnp.minimum(t, n_tiles - 1))

    def stage2(t):
        return decode(jnp.maximum(t - 1, 0))

    def q_map(t):
        b, p, i = stage1(t)
        return p, b * nq + i

    def k_map(t):
        b, p, _ = stage1(t)
        return b, p if per_pair else 0

    def v_map(t):
        b, p, _ = stage2(t)
        return p if per_pair else 0, b

    def out_map(t):
        b, p, i = stage2(t)
        return b * nq + i, p

    in_specs = list(prefix_specs) + [
        pl.BlockSpec((LANES, Q_TILE), q_map),
        pl.BlockSpec((seq, k_lanes), k_map),
        pl.BlockSpec((LANES, seq), v_map),
        pl.BlockSpec((Q_TILE, LANES), out_map)]
    tail = ()
    if band_arr is not None:
        in_specs.append(pl.BlockSpec((1, BAND_CHUNKS, K_CHUNK, Q_TILE),
                                     lambda t: (stage1(t)[1], 0, 0, 0)))
        tail = (band_arr,)
    scores = pltpu.VMEM((n_chunks, 2, K_CHUNK, Q_TILE), jnp.float32)
    key_max = pltpu.VMEM((2, SUBLANES, Q_TILE), jnp.float32)
    scratch = [pltpu.VMEM((2, LANES, Q_TILE), jnp.bfloat16),
               scores, scores,
               key_max, key_max]
    return pl.pallas_call(
        functools.partial(kernel_fn, n_chunks=n_chunks, nq=nq, n_tiles=n_tiles),
        grid=(n_tiles + 1,),
        in_specs=in_specs,
        out_specs=pl.BlockSpec((Q_TILE, LANES), out_map),
        out_shape=jax.ShapeDtypeStruct((m, n_pairs * LANES), jnp.bfloat16),
        scratch_shapes=scratch,
        compiler_params=_cparams(1),
        name=name,
    )(*operands, q_arr, k_arr, v_arr, gate_arr, *tail)


def _attn_diff_call(rel_bias, lam_params, subln_g, qa, ka, va, sa, band, lam_init, batch, seq):
    n_chunks = seq // K_CHUNK
    assert n_chunks <= BIAS_SLOTS and n_chunks & (n_chunks - 1) == 0
    prefix = [pl.BlockSpec(memory_space=pltpu.SMEM),
              pl.BlockSpec((4, A_QK_HEAD), lambda t: (0, 0)),
              pl.BlockSpec((1, A_V_HEAD), lambda t: (0, 0))]
    return _attn_call(functools.partial(_attn_diff_kernel, lam_init=lam_init),
                      (rel_bias, lam_params, subln_g), prefix, qa, ka, va, sa, band,
                      n_pairs=A_HEADS, per_pair=True, batch=batch, seq=seq, name="attn_diff")


def _attn_gqa_call(qb, kb, vb, sb, batch, seq):
    return _attn_call(_attn_gqa_kernel, (), [], qb, kb, vb, sb, None,
                      n_pairs=B_Q_HEADS // 2, per_pair=False, batch=batch, seq=seq,
                      name="attn_gqa")


def _out_kernel(ya_ref, yb_ref, wa_ref, wb_ref, x_ref, g_ref, o_ref):
    f32 = jnp.float32
    y = (jnp.dot(ya_ref[...], wa_ref[...], preferred_element_type=f32)
         + jnp.dot(yb_ref[...], wb_ref[...], preferred_element_type=f32))
    ms = jnp.mean(y * y, axis=-1, keepdims=True)
    o_ref[...] = x_ref[...] + y * lax.rsqrt(ms + EPS) * g_ref[...]


def _out_call(ya, yb, wa, wb, xf, g_post):
    m = xf.shape[0]
    row = lambda i: (i, 0)
    const = lambda i: (0, 0)
    return pl.pallas_call(
        _out_kernel,
        grid=(m // ROW_TILE,),
        in_specs=[pl.BlockSpec((ROW_TILE, A_WIDTH), row),
                  pl.BlockSpec((ROW_TILE, B_WIDTH), row),
                  pl.BlockSpec((A_WIDTH, D_MODEL), const),
                  pl.BlockSpec((B_WIDTH, D_MODEL), const),
                  pl.BlockSpec((ROW_TILE, D_MODEL), row),
                  pl.BlockSpec((1, D_MODEL), const)],
        out_specs=pl.BlockSpec((ROW_TILE, D_MODEL), row),
        out_shape=jax.ShapeDtypeStruct((m, D_MODEL), jnp.float32),
        compiler_params=_cparams(1),
        name="out_proj",
    )(ya, yb, wa, wb, xf, g_post)


def _layer_weights(w_in_l, w_out_l, q_norm_g_l, k_norm_g_l):
    bf16 = jnp.bfloat16
    d = w_in_l.shape[0]
    per_group = B_Q_HEADS // B_KV_HEADS
    half = B_HEAD // 2
    bq = w_in_l[:, _BQ:_BK].reshape(d, B_KV_HEADS, per_group, half, 2)
    bq = bq.transpose(0, 2, 1, 4, 3).reshape(d, B_WIDTH)
    bk = w_in_l[:, _BK:_BV].reshape(d, B_KV_HEADS, half, 2)
    bk = bk.transpose(0, 1, 3, 2).reshape(d, B_KV_HEADS * B_HEAD)
    bg = w_in_l[:, _BG:].reshape(d, B_KV_HEADS, per_group, B_HEAD)
    bg = bg.transpose(0, 2, 1, 3).reshape(d, B_WIDTH)
    wn = jnp.concatenate([w_in_l[:, _AK:_AV], w_in_l[:, _AG:_BQ], bg], axis=1).astype(bf16)
    wt = jnp.concatenate([w_in_l[:, _AQ:_AK], w_in_l[:, _AV:_AG], bq, bk, w_in_l[:, _BV:_BG]],
                         axis=1).astype(bf16).T
    wa = w_out_l[:A_WIDTH].astype(bf16)
    wb = w_out_l[A_WIDTH:].reshape(B_KV_HEADS, per_group, B_HEAD, -1)
    wb = wb.transpose(1, 0, 2, 3).reshape(B_WIDTH, -1).astype(bf16)
    qg = q_norm_g_l.reshape(half, 2).T.reshape(B_HEAD, 1)
    kg = k_norm_g_l.reshape(half, 2).T.reshape(B_HEAD, 1)
    return wn, wt, wa, wb, qg, kg


def kernel(x, rel_bias, pre_norm_g, w_in, diff_lambda, diff_subln_g, q_norm_g, k_norm_g,
           w_out, post_norm_g):
    batch, seq, d_model = x.shape
    xf = x.reshape(batch * seq, d_model)
    band = _band_call(rel_bias)
    cos_t, sin_t = _rope_tables_t(seq)
    for l in range(DEPTH):
        lam_init = 0.8 - 0.6 * math.exp(-0.3 * l)
        wn, wt, wa, wb, qg, kg = _layer_weights(w_in[l], w_out[l], q_norm_g[l], k_norm_g[l])
        qa, ka, va, sa, qb, kb, vb, sb = _proj_call(
            xf, pre_norm_g[l].reshape(1, d_model), wn, wt, cos_t, sin_t, qg, kg, seq)
        ya = _attn_diff_call(rel_bias, diff_lambda[l], diff_subln_g[l].reshape(1, A_V_HEAD),
                             qa, ka, va, sa, band, lam_init, batch, seq)
        yb = _attn_gqa_call(qb, kb, vb, sb, batch, seq)
        xf = _out_call(ya, yb, wa, wb, xf, post_norm_g[l].reshape(1, d_model))
    return xf.reshape(batch, seq, d_model)
```

```python
import functools
import math

import jax
import jax.numpy as jnp
from jax import lax
from jax.experimental import pallas as pl
from jax.experimental.pallas import tpu as pltpu

D_MODEL = 1024
DEPTH = 2
A_WIDTH = 512
B_WIDTH = 512
A_HEADS = 4
A_V_HEAD = 128
A_QK_HEAD = 64
B_HEAD = 64
B_Q_HEADS = 8
B_KV_HEADS = 2
GRID_W = 64
ROPE_THETA = 10000.0
NUM_BUCKETS = 32
MAX_DISTANCE = 128
EPS = 1e-6
LOG2E = math.log2(math.e)
Q_SCALE = (A_QK_HEAD ** -0.5) * LOG2E

LANES = 128
SUBLANES = 8
ROW_TILE = 1024
Q_TILE = 256
K_CHUNK = 256
BAND_CHUNKS = 3
BIAS_SLOTS = 16
BIAS_TERMS = 3
VMEM_LIMIT = 48 * 1024 * 1024

_AQ, _AK, _AV, _AG, _BQ, _BK, _BV, _BG = 0, 512, 1024, 1536, 2048, 2560, 2688, 2816
_T_AQ, _T_AV, _T_BQ, _T_BK, _T_BV, _T_END = 0, 512, 1024, 1536, 1664, 1792
_N_AK, _N_AG, _N_BG, _N_END = 0, 512, 1024, 1536


def _t5_bucket(rel):
    nb = NUM_BUCKETS // 2
    max_exact = nb // 2
    n = jnp.abs(rel)
    large = max_exact + (jnp.log(jnp.maximum(n, 1).astype(jnp.float32) / max_exact)
                         / math.log(MAX_DISTANCE / max_exact) * (nb - max_exact)).astype(jnp.int32)
    large = jnp.minimum(large, nb - 1)
    return jnp.where(rel > 0, nb, 0) + jnp.where(n < max_exact, n, large)


def _rope_tables_t(n):
    rows = n // GRID_W
    axis_dim = B_HEAD // 2
    inv = ROPE_THETA ** (-jnp.arange(0, axis_dim, 2, dtype=jnp.float32) / axis_dim)
    row_ang = (jnp.arange(rows, dtype=jnp.float32)[:, None] * inv).T
    col_ang = (jnp.arange(GRID_W, dtype=jnp.float32)[:, None] * inv).T

    def expand(fn):
        by_row = jnp.repeat(fn(row_ang), GRID_W, axis=1)
        by_col = jnp.tile(fn(col_ang), (1, rows))
        return jnp.concatenate([by_row, by_col], axis=0)

    return expand(jnp.cos), expand(jnp.sin)


def _cparams(n_axes):
    return pltpu.CompilerParams(dimension_semantics=("arbitrary",) * n_axes,
                                vmem_limit_bytes=VMEM_LIMIT)


def _bias_slot_mask(slot_index, chunk):
    in_terms = jnp.right_shift(jnp.bitwise_and(slot_index, A_QK_HEAD - 1), 4) < BIAS_TERMS
    return jnp.logical_and(in_terms, jnp.bitwise_and(slot_index, BIAS_SLOTS - 1) == chunk)


BAND_TABLE = 2 * Q_TILE


def _band_kernel(rb_ref, idx_ref, out_ref):
    h = pl.program_id(0)
    for d in range(BAND_CHUNKS):
        idx = jnp.broadcast_to(idx_ref[d], (SUBLANES, BAND_TABLE))
        vals = jnp.zeros(idx.shape, jnp.float32)
        for b in range(NUM_BUCKETS):
            vals = jnp.where(idx == b, rb_ref[b, h] * LOG2E, vals)

        for j0 in range(0, K_CHUNK, SUBLANES):
            rolled = pltpu.roll(vals, (j0 + Q_TILE + 1) % BAND_TABLE, 1, stride=1, stride_axis=0)
            out_ref[0, d, j0:j0 + SUBLANES, :] = rolled[:, :Q_TILE]


def _band_call(rel_bias):
    d = jnp.arange(BAND_CHUNKS, dtype=jnp.int32)[:, None, None]
    c = jnp.arange(BAND_TABLE, dtype=jnp.int32)[None, None, :]
    idx = _t5_bucket((d - 1) * K_CHUNK + (Q_TILE - 1) - c)
    idx = jnp.take(jnp.arange(NUM_BUCKETS, dtype=jnp.int32), idx)
    return pl.pallas_call(
        _band_kernel,
        grid=(A_HEADS,),
        in_specs=[pl.BlockSpec(memory_space=pltpu.SMEM),
                  pl.BlockSpec((BAND_CHUNKS, 1, BAND_TABLE), lambda h: (0, 0, 0))],
        out_specs=pl.BlockSpec((1, BAND_CHUNKS, K_CHUNK, Q_TILE), lambda h: (h, 0, 0, 0)),
        out_shape=jax.ShapeDtypeStruct((A_HEADS, BAND_CHUNKS, K_CHUNK, Q_TILE), jnp.float32),
        compiler_params=_cparams(1),
        name="bias_band",
    )(rel_bias, idx)


def _proj_kernel(x_ref, g_ref, wn_ref, wt_ref, cos_ref, sin_ref, qg_ref, kg_ref,
                 qa_ref, ka_ref, va_ref, sa_ref, qb_ref, kb_ref, vb_ref, sb_ref,
                 *, tiles_per_seq):
    f32, bf16 = jnp.float32, jnp.bfloat16
    x = x_ref[...]
    ms = jnp.mean(x * x, axis=-1, keepdims=True)
    h = (x * lax.rsqrt(ms + EPS) * g_ref[...]).astype(bf16)

    def nat(lo, hi):
        return jnp.dot(h, wn_ref[:, lo:hi], preferred_element_type=f32)

    def silu(v):
        return v * jax.nn.sigmoid(v)

    pt = lax.dot_general(wt_ref[...], h, (((1,), (1,)), ((), ())), preferred_element_type=f32)
    qa_ref[...] = (pt[_T_AQ:_T_AV] * Q_SCALE).astype(bf16)
    va_ref[...] = pt[_T_AV:_T_BQ].astype(bf16)
    vb_ref[...] = pt[_T_BV:_T_END].astype(bf16)

    cos = cos_ref[...]
    sin = sin_ref[...]
    half = B_HEAD // 2

    def norm_rope(xt, g):
        msq = jnp.mean(xt * xt, axis=0, keepdims=True)
        y = xt * lax.rsqrt(msq + EPS) * g
        e, o = y[:half], y[half:]
        return jnp.concatenate([e * cos - o * sin, e * sin + o * cos], axis=0)

    qg = qg_ref[...]
    qn = jnp.concatenate([norm_rope(pt[_T_BQ + i * B_HEAD:_T_BQ + (i + 1) * B_HEAD], qg)
                          for i in range(B_Q_HEADS)], axis=0) * Q_SCALE
    qb_ref[...] = qn.astype(bf16)
    kg = kg_ref[...]
    kn = jnp.concatenate([norm_rope(pt[_T_BK + i * B_HEAD:_T_BK + (i + 1) * B_HEAD], kg)
                          for i in range(B_KV_HEADS)], axis=0)
    kb_ref[...] = kn.T.astype(bf16)

    sa_ref[...] = silu(nat(_N_AG, _N_BG)).astype(bf16)
    sb_ref[...] = silu(nat(_N_BG, _N_END)).astype(bf16)

    ak = nat(_N_AK, _N_AG)
    lane = lax.broadcasted_iota(jnp.int32, (ROW_TILE, LANES), 1)
    row = lax.broadcasted_iota(jnp.int32, (ROW_TILE, LANES), 0)
    chunk = ((pl.program_id(0) % tiles_per_seq) * (ROW_TILE // K_CHUNK)
             + jnp.right_shift(row, K_CHUNK.bit_length() - 1))
    hot = jnp.where(_bias_slot_mask(lane, chunk), 1.0, 0.0)
    low = lane < A_QK_HEAD
    for hd in range(A_HEADS):
        akh = ak[:, hd * LANES:(hd + 1) * LANES]
        ka_ref[:, 2 * hd * LANES:(2 * hd + 1) * LANES] = jnp.where(low, akh, hot).astype(bf16)
        ka_ref[:, (2 * hd + 1) * LANES:(2 * hd + 2) * LANES] = jnp.where(low, hot, akh).astype(bf16)


def _proj_call(xf, g_pre, wn, wt, cos_t, sin_t, qg, kg, seq):
    m = xf.shape[0]
    tiles_per_seq = seq // ROW_TILE
    bf16 = jnp.bfloat16
    row = lambda i: (i, 0)
    col = lambda i: (0, i)
    const = lambda i: (0, 0)
    pos = lambda i: (0, i % tiles_per_seq)
    return pl.pallas_call(
        functools.partial(_proj_kernel, tiles_per_seq=tiles_per_seq),
        grid=(m // ROW_TILE,),
        in_specs=[pl.BlockSpec((ROW_TILE, D_MODEL), row),
                  pl.BlockSpec((1, D_MODEL), const),
                  pl.BlockSpec(wn.shape, const),
                  pl.BlockSpec(wt.shape, const),
                  pl.BlockSpec((B_HEAD // 2, ROW_TILE), pos),
                  pl.BlockSpec((B_HEAD // 2, ROW_TILE), pos),
                  pl.BlockSpec((B_HEAD, 1), const),
                  pl.BlockSpec((B_HEAD, 1), const)],
        out_specs=[pl.BlockSpec((A_WIDTH, ROW_TILE), col),
                   pl.BlockSpec((ROW_TILE, 2 * A_WIDTH), row),
                   pl.BlockSpec((A_WIDTH, ROW_TILE), col),
                   pl.BlockSpec((ROW_TILE, A_WIDTH), row),
                   pl.BlockSpec((B_WIDTH, ROW_TILE), col),
                   pl.BlockSpec((ROW_TILE, B_KV_HEADS * B_HEAD), row),
                   pl.BlockSpec((B_KV_HEADS * B_HEAD, ROW_TILE), col),
                   pl.BlockSpec((ROW_TILE, B_WIDTH), row)],
        out_shape=[jax.ShapeDtypeStruct((A_WIDTH, m), bf16),
                   jax.ShapeDtypeStruct((m, 2 * A_WIDTH), bf16),
                   jax.ShapeDtypeStruct((A_WIDTH, m), bf16),
                   jax.ShapeDtypeStruct((m, A_WIDTH), bf16),
                   jax.ShapeDtypeStruct((B_WIDTH, m), bf16),
                   jax.ShapeDtypeStruct((m, B_KV_HEADS * B_HEAD), bf16),
                   jax.ShapeDtypeStruct((B_KV_HEADS * B_HEAD, m), bf16),
                   jax.ShapeDtypeStruct((m, B_WIDTH), bf16)],
        compiler_params=_cparams(1),
        name="in_proj",
    )(xf, g_pre, wn, wt, cos_t, sin_t, qg, kg)


def _fold_rows(x, op):
    parts = [x[g * SUBLANES:(g + 1) * SUBLANES] for g in range(x.shape[0] // SUBLANES)]
    while len(parts) > 1:
        parts = [op(parts[i], parts[i + 1]) for i in range(0, len(parts), 2)]
    return parts[0]


def _pipeline_tiles(n_tiles, nq):
    t = pl.program_id(0)
    tile1 = jnp.minimum(t, n_tiles - 1)
    return t, tile1, tile1 % nq


def _zero_first_previous(t, s_odd, m_odd):
    @pl.when(t == 0)
    def _():
        s_odd[...] = jnp.zeros(s_odd.shape, jnp.float32)
        m_odd[...] = jnp.zeros(m_odd.shape, jnp.float32)


def _by_parity(t, step, s_even, m_even, s_odd, m_odd):
    @pl.when(t % 2 == 0)
    def _():
        step(s_even, m_even, s_odd, m_odd)

    @pl.when(t % 2 == 1)
    def _():
        step(s_odd, m_odd, s_even, m_even)


def _attn_gqa_kernel(q_ref, k_ref, v_ref, gate_ref, o_ref,
                     w_ref, s_even, s_odd, m_even, m_odd, *, n_chunks, nq, n_tiles):
    f32, bf16 = jnp.float32, jnp.bfloat16
    t, _, _ = _pipeline_tiles(n_tiles, nq)
    _zero_first_previous(t, s_odd, m_odd)

    def step(s1_ref, m1_ref, s2_ref, m2_ref):
        top = lax.broadcasted_iota(jnp.int32, (LANES, Q_TILE), 0) < B_HEAD
        qt = q_ref[...].astype(f32)
        w_ref[0] = jnp.where(top, qt, 0.0).astype(bf16)
        w_ref[1] = jnp.where(top, 0.0, qt).astype(bf16)

        m_row = [jnp.max(m2_ref[st], axis=0, keepdims=True) for st in range(2)]

        m_acc = [None, None]
        l_acc = [None, None]
        o_acc = [None, None]
        for c in range(n_chunks):
            keys = slice(c * K_CHUNK, (c + 1) * K_CHUNK)
            for st in range(2):
                s = jnp.dot(k_ref[keys, :], w_ref[st], preferred_element_type=f32)
                s1_ref[c, st] = s
                cm = _fold_rows(s, jnp.maximum)
                m_acc[st] = cm if c == 0 else jnp.maximum(m_acc[st], cm)
            for st in range(2):
                p = jnp.exp2(s2_ref[c, st] - m_row[st])
                ps = _fold_rows(p, jnp.add)
                l_acc[st] = ps if c == 0 else l_acc[st] + ps
                part = jnp.dot(v_ref[st * B_HEAD:(st + 1) * B_HEAD, keys], p.astype(bf16),
                               preferred_element_type=f32)
                o_acc[st] = part if c == 0 else o_acc[st] + part
        for st in range(2):
            m1_ref[st] = m_acc[st]

        l_row = [jnp.sum(l_acc[st], axis=0, keepdims=True) for st in range(2)]
        ot = jnp.concatenate([o_acc[0] / l_row[0], o_acc[1] / l_row[1]], axis=0)
        o_ref[...] = (ot.T * gate_ref[...].astype(f32)).astype(o_ref.dtype)

    _by_parity(t, step, s_even, m_even, s_odd, m_odd)


def _attn_diff_kernel(rb_ref, lamp_ref, g_ref, q_ref, k_ref, v_ref, gate_ref, band_ref, o_ref,
                      w_ref, s_even, s_odd, m_even, m_odd, *, lam_init, n_chunks, nq, n_tiles):
    f32, bf16 = jnp.float32, jnp.bfloat16
    t, tile1, qi1 = _pipeline_tiles(n_tiles, nq)
    head1 = (tile1 // nq) % A_HEADS
    _zero_first_previous(t, s_odd, m_odd)

    def step(s1_ref, m1_ref, s2_ref, m2_ref):
        row = lax.broadcasted_iota(jnp.int32, (LANES, Q_TILE), 0)
        top = row < A_QK_HEAD
        slot = jnp.bitwise_and(row, BIAS_SLOTS - 1)
        term = jnp.right_shift(jnp.bitwise_and(row, A_QK_HEAD - 1), 4)
        left = rb_ref[NUM_BUCKETS // 2 - 1, head1] * LOG2E
        right = rb_ref[NUM_BUCKETS - 1, head1] * LOG2E
        const = jnp.where(jnp.abs(slot - qi1) <= 1, 0.0, jnp.where(slot < qi1, left, right))
        t0 = const.astype(bf16).astype(f32)
        t1 = (const - t0).astype(bf16).astype(f32)
        t2 = (const - t0 - t1).astype(bf16).astype(f32)
        bias_rows = jnp.where(term == 0, t0, jnp.where(term == 1, t1,
                                                      jnp.where(term == 2, t2, 0.0)))
        qt = q_ref[...].astype(f32)
        w_ref[0] = jnp.where(top, qt, bias_rows).astype(bf16)
        w_ref[1] = jnp.where(top, bias_rows, qt).astype(bf16)

        m_row = [jnp.max(m2_ref[st], axis=0, keepdims=True) for st in range(2)]

        m_acc = [None, None]
        l_acc = [None, None]
        o_acc = [None, None]
        for i in range(n_chunks):
            c1 = qi1 - 1 + i
            j1 = jnp.bitwise_and(c1, n_chunks - 1)
            key0 = pl.multiple_of(j1 * K_CHUNK, K_CHUNK)
            if i < BAND_CHUNKS:
                in_range = jnp.logical_and(c1 >= 0, c1 < n_chunks)
                band = jnp.where(in_range, band_ref[0, i], 0.0)
            for st in range(2):
                s = jnp.dot(k_ref[pl.ds(key0, K_CHUNK), st * LANES:(st + 1) * LANES], w_ref[st],
                            preferred_element_type=f32)
                if i < BAND_CHUNKS:
                    s = s + band
                s1_ref[j1, st] = s
                cm = _fold_rows(s, jnp.maximum)
                m_acc[st] = cm if i == 0 else jnp.maximum(m_acc[st], cm)
            for st in range(2):
                p = jnp.exp2(s2_ref[i, st] - m_row[st])
                ps = _fold_rows(p, jnp.add)
                l_acc[st] = ps if i == 0 else l_acc[st] + ps
                part = jnp.dot(v_ref[:, i * K_CHUNK:(i + 1) * K_CHUNK], p.astype(bf16),
                               preferred_element_type=f32)
                o_acc[st] = part if i == 0 else o_acc[st] + part
        for st in range(2):
            m1_ref[st] = m_acc[st]

        l_row = [jnp.sum(l_acc[st], axis=0, keepdims=True) for st in range(2)]
        lp = lamp_ref[...]
        lam = (jnp.exp(jnp.sum(lp[0:1] * lp[1:2], axis=-1, keepdims=True))
               - jnp.exp(jnp.sum(lp[2:3] * lp[3:4], axis=-1, keepdims=True)) + lam_init)
        o = (o_acc[0] / l_row[0] - lam * (o_acc[1] / l_row[1])).T
        ms = jnp.mean(o * o, axis=-1, keepdims=True)
        o = (o * lax.rsqrt(ms + EPS) * g_ref[...]) * (1.0 - lam_init)
        o_ref[...] = (o * gate_ref[...].astype(f32)).astype(o_ref.dtype)

    _by_parity(t, step, s_even, m_even, s_odd, m_odd)


def _attn_call(kernel_fn, operands, prefix_specs, q_arr, k_arr, v_arr, gate_arr, band_arr,
               *, n_pairs, per_pair, batch, seq, name):
    m = gate_arr.shape[0]
    nq = seq // Q_TILE
    n_chunks = seq // K_CHUNK
    n_tiles = batch * n_pairs * nq
    k_lanes = k_arr.shape[1] // (n_pairs if per_pair else 1)

    def decode(tile):
        return tile // (n_pairs * nq), (tile // nq) % n_pairs, tile % nq

    def stage1(t):
        return decode(jnp.minimum(t, n_tiles - 1))

    def stage2(t):
        return decode(jnp.maximum(t - 1, 0))

    def q_map(t):
        b, p, i = stage1(t)
        return p, b * nq + i

    def k_map(t):
        b, p, _ = stage1(t)
        return b, p if per_pair else 0

    def v_map(t):
        b, p, _ = stage2(t)
        return p if per_pair else 0, b

    def out_map(t):
        b, p, i = stage2(t)
        return b * nq + i, p

    in_specs = list(prefix_specs) + [
        pl.BlockSpec((LANES, Q_TILE), q_map),
        pl.BlockSpec((seq, k_lanes), k_map),
        pl.BlockSpec((LANES, seq), v_map),
        pl.BlockSpec((Q_TILE, LANES), out_map)]
    tail = ()
    if band_arr is not None:
        in_specs.append(pl.BlockSpec((1, BAND_CHUNKS, K_CHUNK, Q_TILE),
                                     lambda t: (stage1(t)[1], 0, 0, 0)))
        tail = (band_arr,)
    scores = pltpu.VMEM((n_chunks, 2, K_CHUNK, Q_TILE), jnp.float32)
    key_max = pltpu.VMEM((2, SUBLANES, Q_TILE), jnp.float32)
    scratch = [pltpu.VMEM((2, LANES, Q_TILE), jnp.bfloat16),
               scores, scores,
               key_max, key_max]
    return pl.pallas_call(
        functools.partial(kernel_fn, n_chunks=n_chunks, nq=nq, n_tiles=n_tiles),
        grid=(n_tiles + 1,),
        in_specs=in_specs,
        out_specs=pl.BlockSpec((Q_TILE, LANES), out_map),
        out_shape=jax.ShapeDtypeStruct((m, n_pairs * LANES), jnp.bfloat16),
        scratch_shapes=scratch,
        compiler_params=_cparams(1),
        name=name,
    )(*operands, q_arr, k_arr, v_arr, gate_arr, *tail)


def _attn_diff_call(rel_bias, lam_params, subln_g, qa, ka, va, sa, band, lam_init, batch, seq):
    n_chunks = seq // K_CHUNK
    assert n_chunks <= BIAS_SLOTS and n_chunks & (n_chunks - 1) == 0
    prefix = [pl.BlockSpec(memory_space=pltpu.SMEM),
              pl.BlockSpec((4, A_QK_HEAD), lambda t: (0, 0)),
              pl.BlockSpec((1, A_V_HEAD), lambda t: (0, 0))]
    return _attn_call(functools.partial(_attn_diff_kernel, lam_init=lam_init),
                      (rel_bias, lam_params, subln_g), prefix, qa, ka, va, sa, band,
                      n_pairs=A_HEADS, per_pair=True, batch=batch, seq=seq, name="attn_diff")


def _attn_gqa_call(qb, kb, vb, sb, batch, seq):
    return _attn_call(_attn_gqa_kernel, (), [], qb, kb, vb, sb, None,
                      n_pairs=B_Q_HEADS // 2, per_pair=False, batch=batch, seq=seq,
                      name="attn_gqa")


def _out_kernel(ya_ref, yb_ref, wa_ref, wb_ref, x_ref, g_ref, o_ref):
    f32 = jnp.float32
    y = (jnp.dot(ya_ref[...], wa_ref[...], preferred_element_type=f32)
         + jnp.dot(yb_ref[...], wb_ref[...], preferred_element_type=f32))
    ms = jnp.mean(y * y, axis=-1, keepdims=True)
    o_ref[...] = x_ref[...] + y * lax.rsqrt(ms + EPS) * g_ref[...]


def _out_call(ya, yb, wa, wb, xf, g_post):
    m = xf.shape[0]
    row = lambda i: (i, 0)
    const = lambda i: (0, 0)
    return pl.pallas_call(
        _out_kernel,
        grid=(m // ROW_TILE,),
        in_specs=[pl.BlockSpec((ROW_TILE, A_WIDTH), row),
                  pl.BlockSpec((ROW_TILE, B_WIDTH), row),
                  pl.BlockSpec((A_WIDTH, D_MODEL), const),
                  pl.BlockSpec((B_WIDTH, D_MODEL), const),
                  pl.BlockSpec((ROW_TILE, D_MODEL), row),
                  pl.BlockSpec((1, D_MODEL), const)],
        out_specs=pl.BlockSpec((ROW_TILE, D_MODEL), row),
        out_shape=jax.ShapeDtypeStruct((m, D_MODEL), jnp.float32),
        compiler_params=_cparams(1),
        name="out_proj",
    )(ya, yb, wa, wb, xf, g_post)


def _layer_weights(w_in_l, w_out_l, q_norm_g_l, k_norm_g_l):
    bf16 = jnp.bfloat16
    d = w_in_l.shape[0]
    per_group = B_Q_HEADS // B_KV_HEADS
    half = B_HEAD // 2
    bq = w_in_l[:, _BQ:_BK].reshape(d, B_KV_HEADS, per_group, half, 2)
    bq = bq.transpose(0, 2, 1, 4, 3).reshape(d, B_WIDTH)
    bk = w_in_l[:, _BK:_BV].reshape(d, B_KV_HEADS, half, 2)
    bk = bk.transpose(0, 1, 3, 2).reshape(d, B_KV_HEADS * B_HEAD)
    bg = w_in_l[:, _BG:].reshape(d, B_KV_HEADS, per_group, B_HEAD)
    bg = bg.transpose(0, 2, 1, 3).reshape(d, B_WIDTH)
    wn = jnp.concatenate([w_in_l[:, _AK:_AV], w_in_l[:, _AG:_BQ], bg], axis=1).astype(bf16)
    wt = jnp.concatenate([w_in_l[:, _AQ:_AK], w_in_l[:, _AV:_AG], bq, bk, w_in_l[:, _BV:_BG]],
                         axis=1).astype(bf16).T
    wa = w_out_l[:A_WIDTH].astype(bf16)
    wb = w_out_l[A_WIDTH:].reshape(B_KV_HEADS, per_group, B_HEAD, -1)
    wb = wb.transpose(1, 0, 2, 3).reshape(B_WIDTH, -1).astype(bf16)
    qg = q_norm_g_l.reshape(half, 2).T.reshape(B_HEAD, 1)
    kg = k_norm_g_l.reshape(half, 2).T.reshape(B_HEAD, 1)
    return wn, wt, wa, wb, qg, kg


def kernel(x, rel_bias, pre_norm_g, w_in, diff_lambda, diff_subln_g, q_norm_g, k_norm_g,
           w_out, post_norm_g):
    batch, seq, d_model = x.shape
    xf = x.reshape(batch * seq, d_model)
    band = _band_call(rel_bias)
    cos_t, sin_t = _rope_tables_t(seq)
    for l in range(DEPTH):
        lam_init = 0.8 - 0.6 * math.exp(-0.3 * l)
        wn, wt, wa, wb, qg, kg = _layer_weights(w_in[l], w_out[l], q_norm_g[l], k_norm_g[l])
        qa, ka, va, sa, qb, kb, vb, sb = _proj_call(
            xf, pre_norm_g[l].reshape(1, d_model), wn, wt, cos_t, sin_t, qg, kg, seq)
        ya = _attn_diff_call(rel_bias, diff_lambda[l], diff_subln_g[l].reshape(1, A_V_HEAD),
                             qa, ka, va, sa, band, lam_init, batch, seq)
        yb = _attn_gqa_call(qb, kb, vb, sb, batch, seq)
        xf = _out_call(ya, yb, wa, wb, xf, post_norm_g[l].reshape(1, d_model))
    return xf.reshape(batch, seq, d_model)
```

```python
import functools
import math

import jax
import jax.numpy as jnp
from jax import lax
from jax.experimental import pallas as pl
from jax.experimental.pallas import tpu as pltpu

D_MODEL = 1024
DEPTH = 2
A_WIDTH = 512
B_WIDTH = 512
A_HEADS = 4
A_V_HEAD = 128
A_QK_HEAD = 64
B_HEAD = 64
B_Q_HEADS = 8
B_KV_HEADS = 2
GRID_W = 64
ROPE_THETA = 10000.0
NUM_BUCKETS = 32
MAX_DISTANCE = 128
EPS = 1e-6
LOG2E = math.log2(math.e)
Q_SCALE = (A_QK_HEAD ** -0.5) * LOG2E

LANES = 128
SUBLANES = 8
ROW_TILE = 1024
Q_TILE = 256
K_CHUNK = 256
BAND_CHUNKS = 3
BIAS_SLOTS = 16
BIAS_TERMS = 3
VMEM_LIMIT = 48 * 1024 * 1024

_AQ, _AK, _AV, _AG, _BQ, _BK, _BV, _BG = 0, 512, 1024, 1536, 2048, 2560, 2688, 2816
_T_AQ, _T_AV, _T_BQ, _T_BK, _T_BV, _T_END = 0, 512, 1024, 1536, 1664, 1792
_N_AK, _N_AG, _N_BG, _N_END = 0, 512, 1024, 1536


def _t5_bucket(rel):
    nb = NUM_BUCKETS // 2
    max_exact = nb // 2
    n = jnp.abs(rel)
    large = max_exact + (jnp.log(jnp.maximum(n, 1).astype(jnp.float32) / max_exact)
                         / math.log(MAX_DISTANCE / max_exact) * (nb - max_exact)).astype(jnp.int32)
    large = jnp.minimum(large, nb - 1)
    return jnp.where(rel > 0, nb, 0) + jnp.where(n < max_exact, n, large)


def _rope_tables_t(n):
    rows = n // GRID_W
    axis_dim = B_HEAD // 2
    inv = ROPE_THETA ** (-jnp.arange(0, axis_dim, 2, dtype=jnp.float32) / axis_dim)
    row_ang = (jnp.arange(rows, dtype=jnp.float32)[:, None] * inv).T
    col_ang = (jnp.arange(GRID_W, dtype=jnp.float32)[:, None] * inv).T

    def expand(fn):
        by_row = jnp.repeat(fn(row_ang), GRID_W, axis=1)
        by_col = jnp.tile(fn(col_ang), (1, rows))
        return jnp.concatenate([by_row, by_col], axis=0)

    return expand(jnp.cos), expand(jnp.sin)


def _cparams(n_axes):
    return pltpu.CompilerParams(dimension_semantics=("arbitrary",) * n_axes,
                                vmem_limit_bytes=VMEM_LIMIT)


def _bias_slot_mask(slot_index, chunk):
    in_terms = jnp.right_shift(jnp.bitwise_and(slot_index, A_QK_HEAD - 1), 4) < BIAS_TERMS
    return jnp.logical_and(in_terms, jnp.bitwise_and(slot_index, BIAS_SLOTS - 1) == chunk)


BAND_TABLE = 2 * Q_TILE


def _band_kernel(rb_ref, idx_ref, out_ref):
    h = pl.program_id(0)
    for d in range(BAND_CHUNKS):
        idx = jnp.broadcast_to(idx_ref[d], (SUBLANES, BAND_TABLE))
        vals = jnp.zeros(idx.shape, jnp.float32)
        for b in range(NUM_BUCKETS):
            vals = jnp.where(idx == b, rb_ref[b, h] * LOG2E, vals)

        for j0 in range(0, K_CHUNK, SUBLANES):
            rolled = pltpu.roll(vals, (j0 + Q_TILE + 1) % BAND_TABLE, 1, stride=1, stride_axis=0)
            out_ref[0, d, j0:j0 + SUBLANES, :] = rolled[:, :Q_TILE]


def _band_call(rel_bias):
    d = jnp.arange(BAND_CHUNKS, dtype=jnp.int32)[:, None, None]
    c = jnp.arange(BAND_TABLE, dtype=jnp.int32)[None, None, :]
    idx = _t5_bucket((d - 1) * K_CHUNK + (Q_TILE - 1) - c)
    idx = jnp.take(jnp.arange(NUM_BUCKETS, dtype=jnp.int32), idx)
    return pl.pallas_call(
        _band_kernel,
        grid=(A_HEADS,),
        in_specs=[pl.BlockSpec(memory_space=pltpu.SMEM),
                  pl.BlockSpec((BAND_CHUNKS, 1, BAND_TABLE), lambda h: (0, 0, 0))],
        out_specs=pl.BlockSpec((1, BAND_CHUNKS, K_CHUNK, Q_TILE), lambda h: (h, 0, 0, 0)),
        out_shape=jax.ShapeDtypeStruct((A_HEADS, BAND_CHUNKS, K_CHUNK, Q_TILE), jnp.float32),
        compiler_params=_cparams(1),
        name="bias_band",
    )(rel_bias, idx)


def _proj_kernel(x_ref, g_ref, wn_ref, wt_ref, cos_ref, sin_ref, qg_ref, kg_ref,
                 qa_ref, ka_ref, va_ref, sa_ref, qb_ref, kb_ref, vb_ref, sb_ref,
                 *, tiles_per_seq):
    f32, bf16 = jnp.float32, jnp.bfloat16
    x = x_ref[...]
    ms = jnp.mean(x * x, axis=-1, keepdims=True)
    h = (x * lax.rsqrt(ms + EPS) * g_ref[...]).astype(bf16)

    def nat(lo, hi):
        return jnp.dot(h, wn_ref[:, lo:hi], preferred_element_type=f32)

    def silu(v):
        return v * jax.nn.sigmoid(v)

    pt = lax.dot_general(wt_ref[...], h, (((1,), (1,)), ((), ())), preferred_element_type=f32)
    qa_ref[...] = (pt[_T_AQ:_T_AV] * Q_SCALE).astype(bf16)
    va_ref[...] = pt[_T_AV:_T_BQ].astype(bf16)
    vb_ref[...] = pt[_T_BV:_T_END].astype(bf16)

    cos = cos_ref[...]
    sin = sin_ref[...]
    half = B_HEAD // 2

    def norm_rope(xt, g):
        msq = jnp.mean(xt * xt, axis=0, keepdims=True)
        y = xt * lax.rsqrt(msq + EPS) * g
        e, o = y[:half], y[half:]
        return jnp.concatenate([e * cos - o * sin, e * sin + o * cos], axis=0)

    qg = qg_ref[...]
    qn = jnp.concatenate([norm_rope(pt[_T_BQ + i * B_HEAD:_T_BQ + (i + 1) * B_HEAD], qg)
                          for i in range(B_Q_HEADS)], axis=0) * Q_SCALE
    qb_ref[...] = qn.astype(bf16)
    kg = kg_ref[...]
    kn = jnp.concatenate([norm_rope(pt[_T_BK + i * B_HEAD:_T_BK + (i + 1) * B_HEAD], kg)
                          for i in range(B_KV_HEADS)], axis=0)
    kb_ref[...] = kn.T.astype(bf16)

    sa_ref[...] = silu(nat(_N_AG, _N_BG)).astype(bf16)
    sb_ref[...] = silu(nat(_N_BG, _N_END)).astype(bf16)

    ak = nat(_N_AK, _N_AG)
    lane = lax.broadcasted_iota(jnp.int32, (ROW_TILE, LANES), 1)
    row = lax.broadcasted_iota(jnp.int32, (ROW_TILE, LANES), 0)
    chunk = ((pl.program_id(0) % tiles_per_seq) * (ROW_TILE // K_CHUNK)
             + jnp.right_shift(row, K_CHUNK.bit_length() - 1))
    hot = jnp.where(_bias_slot_mask(lane, chunk), 1.0, 0.0)
    low = lane < A_QK_HEAD
    for hd in range(A_HEADS):
        akh = ak[:, hd * LANES:(hd + 1) * LANES]
        ka_ref[:, 2 * hd * LANES:(2 * hd + 1) * LANES] = jnp.where(low, akh, hot).astype(bf16)
        ka_ref[:, (2 * hd + 1) * LANES:(2 * hd + 2) * LANES] = jnp.where(low, hot, akh).astype(bf16)


def _proj_call(xf, g_pre, wn, wt, cos_t, sin_t, qg, kg, seq):
    m = xf.shape[0]
    tiles_per_seq = seq // ROW_TILE
    bf16 = jnp.bfloat16
    row = lambda i: (i, 0)
    col = lambda i: (0, i)
    const = lambda i: (0, 0)
    pos = lambda i: (0, i % tiles_per_seq)
    return pl.pallas_call(
        functools.partial(_proj_kernel, tiles_per_seq=tiles_per_seq),
        grid=(m // ROW_TILE,),
        in_specs=[pl.BlockSpec((ROW_TILE, D_MODEL), row),
                  pl.BlockSpec((1, D_MODEL), const),
                  pl.BlockSpec(wn.shape, const),
                  pl.BlockSpec(wt.shape, const),
                  pl.BlockSpec((B_HEAD // 2, ROW_TILE), pos),
                  pl.BlockSpec((B_HEAD // 2, ROW_TILE), pos),
                  pl.BlockSpec((B_HEAD, 1), const),
                  pl.BlockSpec((B_HEAD, 1), const)],
        out_specs=[pl.BlockSpec((A_WIDTH, ROW_TILE), col),
                   pl.BlockSpec((ROW_TILE, 2 * A_WIDTH), row),
                   pl.BlockSpec((A_WIDTH, ROW_TILE), col),
                   pl.BlockSpec((ROW_TILE, A_WIDTH), row),
                   pl.BlockSpec((B_WIDTH, ROW_TILE), col),
                   pl.BlockSpec((ROW_TILE, B_KV_HEADS * B_HEAD), row),
                   pl.BlockSpec((B_KV_HEADS * B_HEAD, ROW_TILE), col),
                   pl.BlockSpec((ROW_TILE, B_WIDTH), row)],
        out_shape=[jax.ShapeDtypeStruct((A_WIDTH, m), bf16),
                   jax.ShapeDtypeStruct((m, 2 * A_WIDTH), bf16),
                   jax.ShapeDtypeStruct((A_WIDTH, m), bf16),
                   jax.ShapeDtypeStruct((m, A_WIDTH), bf16),
                   jax.ShapeDtypeStruct((B_WIDTH, m), bf16),
                   jax.ShapeDtypeStruct((m, B_KV_HEADS * B_HEAD), bf16),
                   jax.ShapeDtypeStruct((B_KV_HEADS * B_HEAD, m), bf16),
                   jax.ShapeDtypeStruct((m, B_WIDTH), bf16)],
        compiler_params=_cparams(1),
        name="in_proj",
    )(xf, g_pre, wn, wt, cos_t, sin_t, qg, kg)


def _fold_rows(x, op):
    parts = [x[g * SUBLANES:(g + 1) * SUBLANES] for g in range(x.shape[0] // SUBLANES)]
    while len(parts) > 1:
        parts = [op(parts[i], parts[i + 1]) for i in range(0, len(parts), 2)]
    return parts[0]


def _pipeline_tiles(n_tiles, nq):
    t = pl.program_id(0)
    tile1 = jnp.minimum(t, n_tiles - 1)
    return t, tile1, tile1 % nq


def _fill_first_steps(t, s_odd, m_odd, acc_ref, sum_ref):
    @pl.when(t == 0)
    def _():
        s_odd[...] = jnp.zeros(s_odd.shape, jnp.float32)
        m_odd[...] = jnp.zeros(m_odd.shape, jnp.float32)
        acc_ref[...] = jnp.zeros(acc_ref.shape, jnp.float32)
        sum_ref[...] = jnp.ones(sum_ref.shape, jnp.float32)


def _by_parity(t, step, s_even, m_even, s_odd, m_odd):
    @pl.when(t % 2 == 0)
    def _():
        step(s_even, m_even, s_odd, m_odd)

    @pl.when(t % 2 == 1)
    def _():
        step(s_odd, m_odd, s_even, m_even)


def _attn_gqa_kernel(q_ref, k_ref, v_ref, gate_ref, o_ref,
                     w_ref, s_even, s_odd, m_even, m_odd, acc_ref, sum_ref,
                     *, n_chunks, nq, n_tiles):
    f32, bf16 = jnp.float32, jnp.bfloat16
    t, _, _ = _pipeline_tiles(n_tiles, nq)
    _fill_first_steps(t, s_odd, m_odd, acc_ref, sum_ref)

    def step(s1_ref, m1_ref, s2_ref, m2_ref):
        top = lax.broadcasted_iota(jnp.int32, (LANES, Q_TILE), 0) < B_HEAD
        qt = q_ref[...].astype(f32)
        w_ref[0] = jnp.where(top, qt, 0.0).astype(bf16)
        w_ref[1] = jnp.where(top, 0.0, qt).astype(bf16)

        ot = jnp.concatenate([acc_ref[st] / jnp.sum(sum_ref[st], axis=0, keepdims=True)
                              for st in range(2)], axis=0)
        o_ref[...] = (ot.T * gate_ref[...].astype(f32)).astype(o_ref.dtype)

        m_row = [jnp.max(m2_ref[st], axis=0, keepdims=True) for st in range(2)]

        m_acc = [None, None]
        l_acc = [None, None]
        o_acc = [None, None]
        for c in range(n_chunks):
            keys = slice(c * K_CHUNK, (c + 1) * K_CHUNK)
            for st in range(2):
                s = jnp.dot(k_ref[keys, :], w_ref[st], preferred_element_type=f32)
                s1_ref[c, st] = s
                cm = _fold_rows(s, jnp.maximum)
                m_acc[st] = cm if c == 0 else jnp.maximum(m_acc[st], cm)
            for st in range(2):
                p = jnp.exp2(s2_ref[c, st] - m_row[st])
                ps = _fold_rows(p, jnp.add)
                l_acc[st] = ps if c == 0 else l_acc[st] + ps
                part = jnp.dot(v_ref[st * B_HEAD:(st + 1) * B_HEAD, keys], p.astype(bf16),
                               preferred_element_type=f32)
                o_acc[st] = part if c == 0 else o_acc[st] + part
        for st in range(2):
            m1_ref[st] = m_acc[st]
            acc_ref[st] = o_acc[st]
            sum_ref[st] = l_acc[st]

    _by_parity(t, step, s_even, m_even, s_odd, m_odd)


def _attn_diff_kernel(rb_ref, lamp_ref, g_ref, q_ref, k_ref, v_ref, gate_ref, band_ref, o_ref,
                      w_ref, s_even, s_odd, m_even, m_odd, acc_ref, sum_ref,
                      *, lam_init, n_chunks, nq, n_tiles):
    f32, bf16 = jnp.float32, jnp.bfloat16
    t, tile1, qi1 = _pipeline_tiles(n_tiles, nq)
    head1 = (tile1 // nq) % A_HEADS
    _fill_first_steps(t, s_odd, m_odd, acc_ref, sum_ref)

    def step(s1_ref, m1_ref, s2_ref, m2_ref):
        row = lax.broadcasted_iota(jnp.int32, (LANES, Q_TILE), 0)
        top = row < A_QK_HEAD
        slot = jnp.bitwise_and(row, BIAS_SLOTS - 1)
        term = jnp.right_shift(jnp.bitwise_and(row, A_QK_HEAD - 1), 4)
        left = rb_ref[NUM_BUCKETS // 2 - 1, head1] * LOG2E
        right = rb_ref[NUM_BUCKETS - 1, head1] * LOG2E
        const = jnp.where(jnp.abs(slot - qi1) <= 1, 0.0, jnp.where(slot < qi1, left, right))
        t0 = const.astype(bf16).astype(f32)
        t1 = (const - t0).astype(bf16).astype(f32)
        t2 = (const - t0 - t1).astype(bf16).astype(f32)
        bias_rows = jnp.where(term == 0, t0, jnp.where(term == 1, t1,
                                                      jnp.where(term == 2, t2, 0.0)))
        qt = q_ref[...].astype(f32)
        w_ref[0] = jnp.where(top, qt, bias_rows).astype(bf16)
        w_ref[1] = jnp.where(top, bias_rows, qt).astype(bf16)

        l_row = [jnp.sum(sum_ref[st], axis=0, keepdims=True) for st in range(2)]
        lp = lamp_ref[...]
        lam = (jnp.exp(jnp.sum(lp[0:1] * lp[1:2], axis=-1, keepdims=True))
               - jnp.exp(jnp.sum(lp[2:3] * lp[3:4], axis=-1, keepdims=True)) + lam_init)
        o = (acc_ref[0] / l_row[0] - lam * (acc_ref[1] / l_row[1])).T
        ms = jnp.mean(o * o, axis=-1, keepdims=True)
        o = (o * lax.rsqrt(ms + EPS) * g_ref[...]) * (1.0 - lam_init)
        o_ref[...] = (o * gate_ref[...].astype(f32)).astype(o_ref.dtype)

        m_row = [jnp.max(m2_ref[st], axis=0, keepdims=True) for st in range(2)]

        m_acc = [None, None]
        l_acc = [None, None]
        o_acc = [None, None]
        for i in range(n_chunks):
            c1 = qi1 - 1 + i
            j1 = jnp.bitwise_and(c1, n_chunks - 1)
            key0 = pl.multiple_of(j1 * K_CHUNK, K_CHUNK)
            if i < BAND_CHUNKS:
                in_range = jnp.logical_and(c1 >= 0, c1 < n_chunks)
                band = jnp.where(in_range, band_ref[0, i], 0.0)
            for st in range(2):
                s = jnp.dot(k_ref[pl.ds(key0, K_CHUNK), st * LANES:(st + 1) * LANES], w_ref[st],
                            preferred_element_type=f32)
                if i < BAND_CHUNKS:
                    s = s + band
                s1_ref[j1, st] = s
                cm = _fold_rows(s, jnp.maximum)
                m_acc[st] = cm if i == 0 else jnp.maximum(m_acc[st], cm)
            for st in range(2):
                p = jnp.exp2(s2_ref[i, st] - m_row[st])
                ps = _fold_rows(p, jnp.add)
                l_acc[st] = ps if i == 0 else l_acc[st] + ps
                part = jnp.dot(v_ref[:, i * K_CHUNK:(i + 1) * K_CHUNK], p.astype(bf16),
                               preferred_element_type=f32)
                o_acc[st] = part if i == 0 else o_acc[st] + part
        for st in range(2):
            m1_ref[st] = m_acc[st]
            acc_ref[st] = o_acc[st]
            sum_ref[st] = l_acc[st]

    _by_parity(t, step, s_even, m_even, s_odd, m_odd)


def _attn_call(kernel_fn, operands, prefix_specs, q_arr, k_arr, v_arr, gate_arr, band_arr,
               *, n_pairs, per_pair, batch, seq, name):
    m = gate_arr.shape[0]
    nq = seq // Q_TILE
    n_chunks = seq // K_CHUNK
    n_tiles = batch * n_pairs * nq
    k_lanes = k_arr.shape[1] // (n_pairs if per_pair else 1)
    v_rows = LANES if per_pair else B_HEAD

    def decode(tile):
        return tile // (n_pairs * nq), (tile // nq) % n_pairs, tile % nq

    def stage1(t):
        return decode(jnp.minimum(t, n_tiles - 1))

    def stage2(t):
        return decode(jnp.clip(t - 1, 0, n_tiles - 1))

    def stage3(t):
        return decode(jnp.maximum(t - 2, 0))

    def q_map(t):
        b, p, i = stage1(t)
        return p, b * nq + i

    def k_map(t):
        b, p, _ = stage1(t)
        return b, p if per_pair else 0

    def v_map(t):
        b, p, _ = stage2(t)
        return p if per_pair else 0, b

    def out_map(t):
        b, p, i = stage3(t)
        return b * nq + i, p

    in_specs = list(prefix_specs) + [
        pl.BlockSpec((LANES, Q_TILE), q_map),
        pl.BlockSpec((seq, k_lanes), k_map),
        pl.BlockSpec((LANES, seq), v_map),
        pl.BlockSpec((Q_TILE, LANES), out_map)]
    tail = ()
    if band_arr is not None:
        in_specs.append(pl.BlockSpec((1, BAND_CHUNKS, K_CHUNK, Q_TILE),
                                     lambda t: (stage1(t)[1], 0, 0, 0)))
        tail = (band_arr,)
    scores = pltpu.VMEM((n_chunks, 2, K_CHUNK, Q_TILE), jnp.float32)
    per_key_group = pltpu.VMEM((2, SUBLANES, Q_TILE), jnp.float32)
    scratch = [pltpu.VMEM((2, LANES, Q_TILE), jnp.bfloat16),
               scores, scores,
               per_key_group, per_key_group,
               pltpu.VMEM((2, v_rows, Q_TILE), jnp.float32),
               per_key_group]
    return pl.pallas_call(
        functools.partial(kernel_fn, n_chunks=n_chunks, nq=nq, n_tiles=n_tiles),
        grid=(n_tiles + 2,),
        in_specs=in_specs,
        out_specs=pl.BlockSpec((Q_TILE, LANES), out_map),
        out_shape=jax.ShapeDtypeStruct((m, n_pairs * LANES), jnp.bfloat16),
        scratch_shapes=scratch,
        compiler_params=_cparams(1),
        name=name,
    )(*operands, q_arr, k_arr, v_arr, gate_arr, *tail)


def _attn_diff_call(rel_bias, lam_params, subln_g, qa, ka, va, sa, band, lam_init, batch, seq):
    n_chunks = seq // K_CHUNK
    assert n_chunks <= BIAS_SLOTS and n_chunks & (n_chunks - 1) == 0
    prefix = [pl.BlockSpec(memory_space=pltpu.SMEM),
              pl.BlockSpec((4, A_QK_HEAD), lambda t: (0, 0)),
              pl.BlockSpec((1, A_V_HEAD), lambda t: (0, 0))]
    return _attn_call(functools.partial(_attn_diff_kernel, lam_init=lam_init),
                      (rel_bias, lam_params, subln_g), prefix, qa, ka, va, sa, band,
                      n_pairs=A_HEADS, per_pair=True, batch=batch, seq=seq, name="attn_diff")


def _attn_gqa_call(qb, kb, vb, sb, batch, seq):
    return _attn_call(_attn_gqa_kernel, (), [], qb, kb, vb, sb, None,
                      n_pairs=B_Q_HEADS // 2, per_pair=False, batch=batch, seq=seq,
                      name="attn_gqa")


def _out_kernel(ya_ref, yb_ref, wa_ref, wb_ref, x_ref, g_ref, o_ref):
    f32 = jnp.float32
    y = (jnp.dot(ya_ref[...], wa_ref[...], preferred_element_type=f32)
         + jnp.dot(yb_ref[...], wb_ref[...], preferred_element_type=f32))
    ms = jnp.mean(y * y, axis=-1, keepdims=True)
    o_ref[...] = x_ref[...] + y * lax.rsqrt(ms + EPS) * g_ref[...]


def _out_call(ya, yb, wa, wb, xf, g_post):
    m = xf.shape[0]
    row = lambda i: (i, 0)
    const = lambda i: (0, 0)
    return pl.pallas_call(
        _out_kernel,
        grid=(m // ROW_TILE,),
        in_specs=[pl.BlockSpec((ROW_TILE, A_WIDTH), row),
                  pl.BlockSpec((ROW_TILE, B_WIDTH), row),
                  pl.BlockSpec((A_WIDTH, D_MODEL), const),
                  pl.BlockSpec((B_WIDTH, D_MODEL), const),
                  pl.BlockSpec((ROW_TILE, D_MODEL), row),
                  pl.BlockSpec((1, D_MODEL), const)],
        out_specs=pl.BlockSpec((ROW_TILE, D_MODEL), row),
        out_shape=jax.ShapeDtypeStruct((m, D_MODEL), jnp.float32),
        compiler_params=_cparams(1),
        name="out_proj",
    )(ya, yb, wa, wb, xf, g_post)


def _layer_weights(w_in_l, w_out_l, q_norm_g_l, k_norm_g_l):
    bf16 = jnp.bfloat16
    d = w_in_l.shape[0]
    per_group = B_Q_HEADS // B_KV_HEADS
    half = B_HEAD // 2
    bq = w_in_l[:, _BQ:_BK].reshape(d, B_KV_HEADS, per_group, half, 2)
    bq = bq.transpose(0, 2, 1, 4, 3).reshape(d, B_WIDTH)
    bk = w_in_l[:, _BK:_BV].reshape(d, B_KV_HEADS, half, 2)
    bk = bk.transpose(0, 1, 3, 2).reshape(d, B_KV_HEADS * B_HEAD)
    bg = w_in_l[:, _BG:].reshape(d, B_KV_HEADS, per_group, B_HEAD)
    bg = bg.transpose(0, 2, 1, 3).reshape(d, B_WIDTH)
    wn = jnp.concatenate([w_in_l[:, _AK:_AV], w_in_l[:, _AG:_BQ], bg], axis=1).astype(bf16)
    wt = jnp.concatenate([w_in_l[:, _AQ:_AK], w_in_l[:, _AV:_AG], bq, bk, w_in_l[:, _BV:_BG]],
                         axis=1).astype(bf16).T
    wa = w_out_l[:A_WIDTH].astype(bf16)
    wb = w_out_l[A_WIDTH:].reshape(B_KV_HEADS, per_group, B_HEAD, -1)
    wb = wb.transpose(1, 0, 2, 3).reshape(B_WIDTH, -1).astype(bf16)
    qg = q_norm_g_l.reshape(half, 2).T.reshape(B_HEAD, 1)
    kg = k_norm_g_l.reshape(half, 2).T.reshape(B_HEAD, 1)
    return wn, wt, wa, wb, qg, kg


def kernel(x, rel_bias, pre_norm_g, w_in, diff_lambda, diff_subln_g, q_norm_g, k_norm_g,
           w_out, post_norm_g):
    batch, seq, d_model = x.shape
    xf = x.reshape(batch * seq, d_model)
    band = _band_call(rel_bias)
    cos_t, sin_t = _rope_tables_t(seq)
    for l in range(DEPTH):
        lam_init = 0.8 - 0.6 * math.exp(-0.3 * l)
        wn, wt, wa, wb, qg, kg = _layer_weights(w_in[l], w_out[l], q_norm_g[l], k_norm_g[l])
        qa, ka, va, sa, qb, kb, vb, sb = _proj_call(
            xf, pre_norm_g[l].reshape(1, d_model), wn, wt, cos_t, sin_t, qg, kg, seq)
        ya = _attn_diff_call(rel_bias, diff_lambda[l], diff_subln_g[l].reshape(1, A_V_HEAD),
                             qa, ka, va, sa, band, lam_init, batch, seq)
        yb = _attn_gqa_call(qb, kb, vb, sb, batch, seq)
        xf = _out_call(ya, yb, wa, wb, xf, post_norm_g[l].reshape(1, d_model))
    return xf.reshape(batch, seq, d_model)
```

```python
import functools
import math

import jax
import jax.numpy as jnp
from jax import lax
from jax.experimental import pallas as pl
from jax.experimental.pallas import tpu as pltpu

D_MODEL = 1024
DEPTH = 2
A_WIDTH = 512
B_WIDTH = 512
A_HEADS = 4
A_V_HEAD = 128
A_QK_HEAD = 64
B_HEAD = 64
B_Q_HEADS = 8
B_KV_HEADS = 2
GRID_W = 64
ROPE_THETA = 10000.0
NUM_BUCKETS = 32
MAX_DISTANCE = 128
EPS = 1e-6
LOG2E = math.log2(math.e)
Q_SCALE = (A_QK_HEAD ** -0.5) * LOG2E

LANES = 128
SUBLANES = 8
ROW_TILE = 1024
Q_TILE = 256
K_CHUNK = 256
BAND_CHUNKS = 3
BIAS_SLOTS = 16
BIAS_TERMS = 3
VMEM_LIMIT = 48 * 1024 * 1024

_AQ, _AK, _AV, _AG, _BQ, _BK, _BV, _BG = 0, 512, 1024, 1536, 2048, 2560, 2688, 2816
_T_AQ, _T_AV, _T_BQ, _T_BK, _T_BV, _T_END = 0, 512, 1024, 1536, 1664, 1792
_N_AK, _N_AG, _N_BG, _N_END = 0, 512, 1024, 1536


def _t5_bucket(rel):
    nb = NUM_BUCKETS // 2
    max_exact = nb // 2
    n = jnp.abs(rel)
    large = max_exact + (jnp.log(jnp.maximum(n, 1).astype(jnp.float32) / max_exact)
                         / math.log(MAX_DISTANCE / max_exact) * (nb - max_exact)).astype(jnp.int32)
    large = jnp.minimum(large, nb - 1)
    return jnp.where(rel > 0, nb, 0) + jnp.where(n < max_exact, n, large)


def _rope_tables_t(n):
    rows = n // GRID_W
    axis_dim = B_HEAD // 2
    inv = ROPE_THETA ** (-jnp.arange(0, axis_dim, 2, dtype=jnp.float32) / axis_dim)
    row_ang = (jnp.arange(rows, dtype=jnp.float32)[:, None] * inv).T
    col_ang = (jnp.arange(GRID_W, dtype=jnp.float32)[:, None] * inv).T

    def expand(fn):
        by_row = jnp.repeat(fn(row_ang), GRID_W, axis=1)
        by_col = jnp.tile(fn(col_ang), (1, rows))
        return jnp.concatenate([by_row, by_col], axis=0)

    return expand(jnp.cos), expand(jnp.sin)


def _cparams(n_axes):
    return pltpu.CompilerParams(dimension_semantics=("arbitrary",) * n_axes,
                                vmem_limit_bytes=VMEM_LIMIT)


def _bias_slot_mask(slot_index, chunk):
    in_terms = jnp.right_shift(jnp.bitwise_and(slot_index, A_QK_HEAD - 1), 4) < BIAS_TERMS
    return jnp.logical_and(in_terms, jnp.bitwise_and(slot_index, BIAS_SLOTS - 1) == chunk)


BAND_TABLE = 2 * Q_TILE


def _band_kernel(rb_ref, idx_ref, out_ref):
    h = pl.program_id(0)
    for d in range(BAND_CHUNKS):
        idx = jnp.broadcast_to(idx_ref[d], (SUBLANES, BAND_TABLE))
        vals = jnp.zeros(idx.shape, jnp.float32)
        for b in range(NUM_BUCKETS):
            vals = jnp.where(idx == b, rb_ref[b, h] * LOG2E, vals)

        for j0 in range(0, K_CHUNK, SUBLANES):
            rolled = pltpu.roll(vals, (j0 + Q_TILE + 1) % BAND_TABLE, 1, stride=1, stride_axis=0)
            out_ref[0, d, j0:j0 + SUBLANES, :] = rolled[:, :Q_TILE]


def _band_call(rel_bias):
    d = jnp.arange(BAND_CHUNKS, dtype=jnp.int32)[:, None, None]
    c = jnp.arange(BAND_TABLE, dtype=jnp.int32)[None, None, :]
    idx = _t5_bucket((d - 1) * K_CHUNK + (Q_TILE - 1) - c)
    idx = jnp.take(jnp.arange(NUM_BUCKETS, dtype=jnp.int32), idx)
    return pl.pallas_call(
        _band_kernel,
        grid=(A_HEADS,),
        in_specs=[pl.BlockSpec(memory_space=pltpu.SMEM),
                  pl.BlockSpec((BAND_CHUNKS, 1, BAND_TABLE), lambda h: (0, 0, 0))],
        out_specs=pl.BlockSpec((1, BAND_CHUNKS, K_CHUNK, Q_TILE), lambda h: (h, 0, 0, 0)),
        out_shape=jax.ShapeDtypeStruct((A_HEADS, BAND_CHUNKS, K_CHUNK, Q_TILE), jnp.float32),
        compiler_params=_cparams(1),
        name="bias_band",
    )(rel_bias, idx)


def _proj_kernel(x_ref, g_ref, wn_ref, wt_ref, cos_ref, sin_ref, qg_ref, kg_ref,
                 qa_ref, ka_ref, va_ref, sa_ref, qb_ref, kb_ref, vb_ref, sb_ref,
                 *, tiles_per_seq):
    f32, bf16 = jnp.float32, jnp.bfloat16
    x = x_ref[...]
    ms = jnp.mean(x * x, axis=-1, keepdims=True)
    h = (x * lax.rsqrt(ms + EPS) * g_ref[...]).astype(bf16)

    def nat(lo, hi):
        return jnp.dot(h, wn_ref[:, lo:hi], preferred_element_type=f32)

    def silu(v):
        return v * jax.nn.sigmoid(v)

    pt = lax.dot_general(wt_ref[...], h, (((1,), (1,)), ((), ())), preferred_element_type=f32)
    qa_ref[...] = (pt[_T_AQ:_T_AV] * Q_SCALE).astype(bf16)
    va_ref[...] = pt[_T_AV:_T_BQ].astype(bf16)
    vb_ref[...] = pt[_T_BV:_T_END].astype(bf16)

    cos = cos_ref[...]
    sin = sin_ref[...]
    half = B_HEAD // 2

    def norm_rope(xt, g):
        msq = jnp.mean(xt * xt, axis=0, keepdims=True)
        y = xt * lax.rsqrt(msq + EPS) * g
        e, o = y[:half], y[half:]
        return jnp.concatenate([e * cos - o * sin, e * sin + o * cos], axis=0)

    qg = qg_ref[...]
    qn = jnp.concatenate([norm_rope(pt[_T_BQ + i * B_HEAD:_T_BQ + (i + 1) * B_HEAD], qg)
                          for i in range(B_Q_HEADS)], axis=0) * Q_SCALE
    qb_ref[...] = qn.astype(bf16)
    kg = kg_ref[...]
    kn = jnp.concatenate([norm_rope(pt[_T_BK + i * B_HEAD:_T_BK + (i + 1) * B_HEAD], kg)
                          for i in range(B_KV_HEADS)], axis=0)
    kb_ref[...] = kn.T.astype(bf16)

    sa_ref[...] = silu(nat(_N_AG, _N_BG)).astype(bf16)
    sb_ref[...] = silu(nat(_N_BG, _N_END)).astype(bf16)

    ak = nat(_N_AK, _N_AG)
    lane = lax.broadcasted_iota(jnp.int32, (ROW_TILE, LANES), 1)
    row = lax.broadcasted_iota(jnp.int32, (ROW_TILE, LANES), 0)
    chunk = ((pl.program_id(0) % tiles_per_seq) * (ROW_TILE // K_CHUNK)
             + jnp.right_shift(row, K_CHUNK.bit_length() - 1))
    hot = jnp.where(_bias_slot_mask(lane, chunk), 1.0, 0.0)
    low = lane < A_QK_HEAD
    for hd in range(A_HEADS):
        akh = ak[:, hd * LANES:(hd + 1) * LANES]
        ka_ref[:, 2 * hd * LANES:(2 * hd + 1) * LANES] = jnp.where(low, akh, hot).astype(bf16)
        ka_ref[:, (2 * hd + 1) * LANES:(2 * hd + 2) * LANES] = jnp.where(low, hot, akh).astype(bf16)


def _proj_call(xf, g_pre, wn, wt, cos_t, sin_t, qg, kg, seq, layer):
    m = xf.shape[0]
    tiles_per_seq = seq // ROW_TILE
    bf16 = jnp.bfloat16
    row = lambda i: (i, 0)
    col = lambda i: (0, i)
    const = lambda i: (0, 0)
    of_layer = lambda i: (layer, 0, 0)
    pos = lambda i: (0, i % tiles_per_seq)
    return pl.pallas_call(
        functools.partial(_proj_kernel, tiles_per_seq=tiles_per_seq),
        grid=(m // ROW_TILE,),
        in_specs=[pl.BlockSpec((ROW_TILE, D_MODEL), row),
                  pl.BlockSpec((1, D_MODEL), const),
                  pl.BlockSpec((None,) + wn.shape[1:], of_layer),
                  pl.BlockSpec((None,) + wt.shape[1:], of_layer),
                  pl.BlockSpec((B_HEAD // 2, ROW_TILE), pos),
                  pl.BlockSpec((B_HEAD // 2, ROW_TILE), pos),
                  pl.BlockSpec((None, B_HEAD, 1), of_layer),
                  pl.BlockSpec((None, B_HEAD, 1), of_layer)],
        out_specs=[pl.BlockSpec((A_WIDTH, ROW_TILE), col),
                   pl.BlockSpec((ROW_TILE, 2 * A_WIDTH), row),
                   pl.BlockSpec((A_WIDTH, ROW_TILE), col),
                   pl.BlockSpec((ROW_TILE, A_WIDTH), row),
                   pl.BlockSpec((B_WIDTH, ROW_TILE), col),
                   pl.BlockSpec((ROW_TILE, B_KV_HEADS * B_HEAD), row),
                   pl.BlockSpec((B_KV_HEADS * B_HEAD, ROW_TILE), col),
                   pl.BlockSpec((ROW_TILE, B_WIDTH), row)],
        out_shape=[jax.ShapeDtypeStruct((A_WIDTH, m), bf16),
                   jax.ShapeDtypeStruct((m, 2 * A_WIDTH), bf16),
                   jax.ShapeDtypeStruct((A_WIDTH, m), bf16),
                   jax.ShapeDtypeStruct((m, A_WIDTH), bf16),
                   jax.ShapeDtypeStruct((B_WIDTH, m), bf16),
                   jax.ShapeDtypeStruct((m, B_KV_HEADS * B_HEAD), bf16),
                   jax.ShapeDtypeStruct((B_KV_HEADS * B_HEAD, m), bf16),
                   jax.ShapeDtypeStruct((m, B_WIDTH), bf16)],
        compiler_params=_cparams(1),
        name="in_proj",
    )(xf, g_pre, wn, wt, cos_t, sin_t, qg, kg)


def _fold_rows(x, op):
    parts = [x[g * SUBLANES:(g + 1) * SUBLANES] for g in range(x.shape[0] // SUBLANES)]
    while len(parts) > 1:
        parts = [op(parts[i], parts[i + 1]) for i in range(0, len(parts), 2)]
    return parts[0]


def _pipeline_tiles(n_tiles, nq):
    t = pl.program_id(0)
    tile1 = jnp.minimum(t, n_tiles - 1)
    return t, tile1, tile1 % nq


def _fill_first_steps(t, s_odd, m_odd, acc_ref, sum_ref):
    @pl.when(t == 0)
    def _():
        s_odd[...] = jnp.zeros(s_odd.shape, jnp.float32)
        m_odd[...] = jnp.zeros(m_odd.shape, jnp.float32)
        acc_ref[...] = jnp.zeros(acc_ref.shape, jnp.float32)
        sum_ref[...] = jnp.ones(sum_ref.shape, jnp.float32)


def _by_parity(t, step, s_even, m_even, s_odd, m_odd):
    @pl.when(t % 2 == 0)
    def _():
        step(s_even, m_even, s_odd, m_odd)

    @pl.when(t % 2 == 1)
    def _():
        step(s_odd, m_odd, s_even, m_even)


def _attn_gqa_kernel(q_ref, k_ref, v_ref, gate_ref, o_ref,
                     w_ref, s_even, s_odd, m_even, m_odd, acc_ref, sum_ref,
                     *, n_chunks, nq, n_tiles):
    f32, bf16 = jnp.float32, jnp.bfloat16
    t, _, _ = _pipeline_tiles(n_tiles, nq)
    _fill_first_steps(t, s_odd, m_odd, acc_ref, sum_ref)

    def step(s1_ref, m1_ref, s2_ref, m2_ref):
        top = lax.broadcasted_iota(jnp.int32, (LANES, Q_TILE), 0) < B_HEAD
        qt = q_ref[...].astype(f32)
        w_ref[0] = jnp.where(top, qt, 0.0).astype(bf16)
        w_ref[1] = jnp.where(top, 0.0, qt).astype(bf16)

        ot = jnp.concatenate([acc_ref[st] / jnp.sum(sum_ref[st], axis=0, keepdims=True)
                              for st in range(2)], axis=0)
        o_ref[...] = (ot.T * gate_ref[...].astype(f32)).astype(o_ref.dtype)

        m_row = [jnp.max(m2_ref[st], axis=0, keepdims=True) for st in range(2)]

        m_acc = [None, None]
        l_acc = [None, None]
        o_acc = [None, None]
        for c in range(n_chunks):
            keys = slice(c * K_CHUNK, (c + 1) * K_CHUNK)
            for st in range(2):
                s = jnp.dot(k_ref[keys, :], w_ref[st], preferred_element_type=f32)
                s1_ref[c, st] = s
                cm = _fold_rows(s, jnp.maximum)
                m_acc[st] = cm if c == 0 else jnp.maximum(m_acc[st], cm)
            for st in range(2):
                p = jnp.exp2(s2_ref[c, st] - m_row[st])
                ps = _fold_rows(p, jnp.add)
                l_acc[st] = ps if c == 0 else l_acc[st] + ps
                part = jnp.dot(v_ref[st * B_HEAD:(st + 1) * B_HEAD, keys], p.astype(bf16),
                               preferred_element_type=f32)
                o_acc[st] = part if c == 0 else o_acc[st] + part
        for st in range(2):
            m1_ref[st] = m_acc[st]
            acc_ref[st] = o_acc[st]
            sum_ref[st] = l_acc[st]

    _by_parity(t, step, s_even, m_even, s_odd, m_odd)


def _attn_diff_kernel(rb_ref, lamp_ref, g_ref, q_ref, k_ref, v_ref, gate_ref, band_ref, o_ref,
                      w_ref, s_even, s_odd, m_even, m_odd, acc_ref, sum_ref,
                      *, lam_init, n_chunks, nq, n_tiles):
    f32, bf16 = jnp.float32, jnp.bfloat16
    t, tile1, qi1 = _pipeline_tiles(n_tiles, nq)
    head1 = (tile1 // nq) % A_HEADS
    _fill_first_steps(t, s_odd, m_odd, acc_ref, sum_ref)

    def step(s1_ref, m1_ref, s2_ref, m2_ref):
        row = lax.broadcasted_iota(jnp.int32, (LANES, Q_TILE), 0)
        top = row < A_QK_HEAD
        slot = jnp.bitwise_and(row, BIAS_SLOTS - 1)
        term = jnp.right_shift(jnp.bitwise_and(row, A_QK_HEAD - 1), 4)
        left = rb_ref[NUM_BUCKETS // 2 - 1, head1] * LOG2E
        right = rb_ref[NUM_BUCKETS - 1, head1] * LOG2E
        const = jnp.where(jnp.abs(slot - qi1) <= 1, 0.0, jnp.where(slot < qi1, left, right))
        t0 = const.astype(bf16).astype(f32)
        t1 = (const - t0).astype(bf16).astype(f32)
        t2 = (const - t0 - t1).astype(bf16).astype(f32)
        bias_rows = jnp.where(term == 0, t0, jnp.where(term == 1, t1,
                                                      jnp.where(term == 2, t2, 0.0)))
        qt = q_ref[...].astype(f32)
        w_ref[0] = jnp.where(top, qt, bias_rows).astype(bf16)
        w_ref[1] = jnp.where(top, bias_rows, qt).astype(bf16)

        l_row = [jnp.sum(sum_ref[st], axis=0, keepdims=True) for st in range(2)]
        lp = lamp_ref[...]
        lam = (jnp.exp(jnp.sum(lp[0:1] * lp[1:2], axis=-1, keepdims=True))
               - jnp.exp(jnp.sum(lp[2:3] * lp[3:4], axis=-1, keepdims=True)) + lam_init)
        o = (acc_ref[0] / l_row[0] - lam * (acc_ref[1] / l_row[1])).T
        ms = jnp.mean(o * o, axis=-1, keepdims=True)
        o = (o * lax.rsqrt(ms + EPS) * g_ref[...]) * (1.0 - lam_init)
        o_ref[...] = (o * gate_ref[...].astype(f32)).astype(o_ref.dtype)

        m_row = [jnp.max(m2_ref[st], axis=0, keepdims=True) for st in range(2)]

        m_acc = [None, None]
        l_acc = [None, None]
        o_acc = [None, None]
        for i in range(n_chunks):
            c1 = qi1 - 1 + i
            j1 = jnp.bitwise_and(c1, n_chunks - 1)
            key0 = pl.multiple_of(j1 * K_CHUNK, K_CHUNK)
            if i < BAND_CHUNKS:
                in_range = jnp.logical_and(c1 >= 0, c1 < n_chunks)
                band = jnp.where(in_range, band_ref[0, i], 0.0)
            for st in range(2):
                s = jnp.dot(k_ref[pl.ds(key0, K_CHUNK), st * LANES:(st + 1) * LANES], w_ref[st],
                            preferred_element_type=f32)
                if i < BAND_CHUNKS:
                    s = s + band
                s1_ref[j1, st] = s
                cm = _fold_rows(s, jnp.maximum)
                m_acc[st] = cm if i == 0 else jnp.maximum(m_acc[st], cm)
            for st in range(2):
                p = jnp.exp2(s2_ref[i, st] - m_row[st])
                ps = _fold_rows(p, jnp.add)
                l_acc[st] = ps if i == 0 else l_acc[st] + ps
                part = jnp.dot(v_ref[:, i * K_CHUNK:(i + 1) * K_CHUNK], p.astype(bf16),
                               preferred_element_type=f32)
                o_acc[st] = part if i == 0 else o_acc[st] + part
        for st in range(2):
            m1_ref[st] = m_acc[st]
            acc_ref[st] = o_acc[st]
            sum_ref[st] = l_acc[st]

    _by_parity(t, step, s_even, m_even, s_odd, m_odd)


def _attn_call(kernel_fn, operands, prefix_specs, q_arr, k_arr, v_arr, gate_arr, band_arr,
               *, n_pairs, per_pair, batch, seq, name):
    m = gate_arr.shape[0]
    nq = seq // Q_TILE
    n_chunks = seq // K_CHUNK
    n_tiles = batch * n_pairs * nq
    k_lanes = k_arr.shape[1] // (n_pairs if per_pair else 1)
    v_rows = LANES if per_pair else B_HEAD

    def decode(tile):
        return tile // (n_pairs * nq), (tile // nq) % n_pairs, tile % nq

    def stage1(t):
        return decode(jnp.minimum(t, n_tiles - 1))

    def stage2(t):
        return decode(jnp.clip(t - 1, 0, n_tiles - 1))

    def stage3(t):
        return decode(jnp.maximum(t - 2, 0))

    def q_map(t):
        b, p, i = stage1(t)
        return p, b * nq + i

    def k_map(t):
        b, p, _ = stage1(t)
        return b, p if per_pair else 0

    def v_map(t):
        b, p, _ = stage2(t)
        return p if per_pair else 0, b

    def out_map(t):
        b, p, i = stage3(t)
        return b * nq + i, p

    in_specs = list(prefix_specs) + [
        pl.BlockSpec((LANES, Q_TILE), q_map),
        pl.BlockSpec((seq, k_lanes), k_map),
        pl.BlockSpec((LANES, seq), v_map),
        pl.BlockSpec((Q_TILE, LANES), out_map)]
    tail = ()
    if band_arr is not None:
        in_specs.append(pl.BlockSpec((1, BAND_CHUNKS, K_CHUNK, Q_TILE),
                                     lambda t: (stage1(t)[1], 0, 0, 0)))
        tail = (band_arr,)
    scores = pltpu.VMEM((n_chunks, 2, K_CHUNK, Q_TILE), jnp.float32)
    per_key_group = pltpu.VMEM((2, SUBLANES, Q_TILE), jnp.float32)
    scratch = [pltpu.VMEM((2, LANES, Q_TILE), jnp.bfloat16),
               scores, scores,
               per_key_group, per_key_group,
               pltpu.VMEM((2, v_rows, Q_TILE), jnp.float32),
               per_key_group]
    return pl.pallas_call(
        functools.partial(kernel_fn, n_chunks=n_chunks, nq=nq, n_tiles=n_tiles),
        grid=(n_tiles + 2,),
        in_specs=in_specs,
        out_specs=pl.BlockSpec((Q_TILE, LANES), out_map),
        out_shape=jax.ShapeDtypeStruct((m, n_pairs * LANES), jnp.bfloat16),
        scratch_shapes=scratch,
        compiler_params=_cparams(1),
        name=name,
    )(*operands, q_arr, k_arr, v_arr, gate_arr, *tail)


def _attn_diff_call(rel_bias, lam_params, subln_g, qa, ka, va, sa, band, lam_init, batch, seq):
    n_chunks = seq // K_CHUNK
    assert n_chunks <= BIAS_SLOTS and n_chunks & (n_chunks - 1) == 0
    prefix = [pl.BlockSpec(memory_space=pltpu.SMEM),
              pl.BlockSpec((4, A_QK_HEAD), lambda t: (0, 0)),
              pl.BlockSpec((1, A_V_HEAD), lambda t: (0, 0))]
    return _attn_call(functools.partial(_attn_diff_kernel, lam_init=lam_init),
                      (rel_bias, lam_params, subln_g), prefix, qa, ka, va, sa, band,
                      n_pairs=A_HEADS, per_pair=True, batch=batch, seq=seq, name="attn_diff")


def _attn_gqa_call(qb, kb, vb, sb, batch, seq):
    return _attn_call(_attn_gqa_kernel, (), [], qb, kb, vb, sb, None,
                      n_pairs=B_Q_HEADS // 2, per_pair=False, batch=batch, seq=seq,
                      name="attn_gqa")


def _out_kernel(ya_ref, yb_ref, wa_ref, wb_ref, x_ref, g_ref, o_ref):
    f32 = jnp.float32
    y = (jnp.dot(ya_ref[...], wa_ref[...], preferred_element_type=f32)
         + jnp.dot(yb_ref[...], wb_ref[...], preferred_element_type=f32))
    ms = jnp.mean(y * y, axis=-1, keepdims=True)
    o_ref[...] = x_ref[...] + y * lax.rsqrt(ms + EPS) * g_ref[...]


def _out_call(ya, yb, wa, wb, xf, g_post, layer):
    m = xf.shape[0]
    row = lambda i: (i, 0)
    const = lambda i: (0, 0)
    return pl.pallas_call(
        _out_kernel,
        grid=(m // ROW_TILE,),
        in_specs=[pl.BlockSpec((ROW_TILE, A_WIDTH), row),
                  pl.BlockSpec((ROW_TILE, B_WIDTH), row),
                  pl.BlockSpec((None, A_WIDTH, D_MODEL), lambda i: (layer, 0, 0)),
                  pl.BlockSpec((None, B_WIDTH, D_MODEL), lambda i: (layer, 0, 0)),
                  pl.BlockSpec((ROW_TILE, D_MODEL), row),
                  pl.BlockSpec((1, D_MODEL), const)],
        out_specs=pl.BlockSpec((ROW_TILE, D_MODEL), row),
        out_shape=jax.ShapeDtypeStruct((m, D_MODEL), jnp.float32),
        compiler_params=_cparams(1),
        name="out_proj",
    )(ya, yb, wa, wb, xf, g_post)


def _layer_weights(w_in_l, w_out_l, q_norm_g_l, k_norm_g_l):
    bf16 = jnp.bfloat16
    d = w_in_l.shape[0]
    per_group = B_Q_HEADS // B_KV_HEADS
    half = B_HEAD // 2
    bq = w_in_l[:, _BQ:_BK].reshape(d, B_KV_HEADS, per_group, half, 2)
    bq = bq.transpose(0, 2, 1, 4, 3).reshape(d, B_WIDTH)
    bk = w_in_l[:, _BK:_BV].reshape(d, B_KV_HEADS, half, 2)
    bk = bk.transpose(0, 1, 3, 2).reshape(d, B_KV_HEADS * B_HEAD)
    bg = w_in_l[:, _BG:].reshape(d, B_KV_HEADS, per_group, B_HEAD)
    bg = bg.transpose(0, 2, 1, 3).reshape(d, B_WIDTH)
    wn = jnp.concatenate([w_in_l[:, _AK:_AV], w_in_l[:, _AG:_BQ], bg], axis=1).astype(bf16)
    wt = jnp.concatenate([w_in_l[:, _AQ:_AK], w_in_l[:, _AV:_AG], bq, bk, w_in_l[:, _BV:_BG]],
                         axis=1).astype(bf16).T
    wa = w_out_l[:A_WIDTH].astype(bf16)
    wb = w_out_l[A_WIDTH:].reshape(B_KV_HEADS, per_group, B_HEAD, -1)
    wb = wb.transpose(1, 0, 2, 3).reshape(B_WIDTH, -1).astype(bf16)
    qg = q_norm_g_l.reshape(half, 2).T.reshape(B_HEAD, 1)
    kg = k_norm_g_l.reshape(half, 2).T.reshape(B_HEAD, 1)
    return wn, wt, wa, wb, qg, kg


def kernel(x, rel_bias, pre_norm_g, w_in, diff_lambda, diff_subln_g, q_norm_g, k_norm_g,
           w_out, post_norm_g):
    batch, seq, d_model = x.shape
    xf = x.reshape(batch * seq, d_model)
    band = _band_call(rel_bias)
    cos_t, sin_t = _rope_tables_t(seq)
    wn, wt, wa, wb, qg, kg = jax.vmap(_layer_weights)(w_in, w_out, q_norm_g, k_norm_g)
    for l in range(DEPTH):
        lam_init = 0.8 - 0.6 * math.exp(-0.3 * l)
        qa, ka, va, sa, qb, kb, vb, sb = _proj_call(
            xf, pre_norm_g[l].reshape(1, d_model), wn, wt, cos_t, sin_t, qg, kg, seq, l)
        ya = _attn_diff_call(rel_bias, diff_lambda[l], diff_subln_g[l].reshape(1, A_V_HEAD),
                             qa, ka, va, sa, band, lam_init, batch, seq)
        yb = _attn_gqa_call(qb, kb, vb, sb, batch, seq)
        xf = _out_call(ya, yb, wa, wb, xf, post_norm_g[l].reshape(1, d_model), l)
    return xf.reshape(batch, seq, d_model)
```

```python
import functools
import math

import jax
import jax.numpy as jnp
from jax import lax
from jax.experimental import pallas as pl
from jax.experimental.pallas import tpu as pltpu

D_MODEL = 1024
DEPTH = 2
A_WIDTH = 512
B_WIDTH = 512
A_HEADS = 4
A_V_HEAD = 128
A_QK_HEAD = 64
B_HEAD = 64
B_Q_HEADS = 8
B_KV_HEADS = 2
GRID_W = 64
ROPE_THETA = 10000.0
NUM_BUCKETS = 32
MAX_DISTANCE = 128
EPS = 1e-6
LOG2E = math.log2(math.e)
Q_SCALE = (A_QK_HEAD ** -0.5) * LOG2E

LANES = 128
SUBLANES = 8
ROW_TILE = 1024
Q_TILE = 256
K_CHUNK = 256
BAND_CHUNKS = 3
BIAS_SLOTS = 16
BIAS_SLOT_BITS = BIAS_SLOTS.bit_length() - 1
BIAS_TERMS = 3
VMEM_LIMIT = 48 * 1024 * 1024

_AQ, _AK, _AV, _AG, _BQ, _BK, _BV, _BG = 0, 512, 1024, 1536, 2048, 2560, 2688, 2816
_T_AQ, _T_AV, _T_BQ, _T_BK, _T_BV, _T_END = 0, 512, 1024, 1536, 1664, 1792
_N_AK, _N_AG, _N_BG, _N_END = 0, 512, 1024, 1536


def _t5_bucket(rel):
    nb = NUM_BUCKETS // 2
    max_exact = nb // 2
    n = jnp.abs(rel)
    large = max_exact + (jnp.log(jnp.maximum(n, 1).astype(jnp.float32) / max_exact)
                         / math.log(MAX_DISTANCE / max_exact) * (nb - max_exact)).astype(jnp.int32)
    large = jnp.minimum(large, nb - 1)
    return jnp.where(rel > 0, nb, 0) + jnp.where(n < max_exact, n, large)


def _rope_tables_t(n):
    rows = n // GRID_W
    axis_dim = B_HEAD // 2
    inv = ROPE_THETA ** (-jnp.arange(0, axis_dim, 2, dtype=jnp.float32) / axis_dim)
    row_ang = (jnp.arange(rows, dtype=jnp.float32)[:, None] * inv).T
    col_ang = (jnp.arange(GRID_W, dtype=jnp.float32)[:, None] * inv).T

    def expand(fn):
        by_row = jnp.repeat(fn(row_ang), GRID_W, axis=1)
        by_col = jnp.tile(fn(col_ang), (1, rows))
        return jnp.concatenate([by_row, by_col], axis=0)

    return expand(jnp.cos), expand(jnp.sin)


def _cparams(n_axes):
    return pltpu.CompilerParams(dimension_semantics=("arbitrary",) * n_axes,
                                vmem_limit_bytes=VMEM_LIMIT)


def _bias_slot_mask(slot_index, chunk):
    term = jnp.right_shift(jnp.bitwise_and(slot_index, A_QK_HEAD - 1), BIAS_SLOT_BITS)
    in_terms = term < BIAS_TERMS
    return jnp.logical_and(in_terms, jnp.bitwise_and(slot_index, BIAS_SLOTS - 1) == chunk)


BAND_TABLE = 2 * Q_TILE


def _band_kernel(rb_ref, idx_ref, out_ref):
    h = pl.program_id(0)
    for d in range(BAND_CHUNKS):
        idx = jnp.broadcast_to(idx_ref[d], (SUBLANES, BAND_TABLE))
        vals = jnp.zeros(idx.shape, jnp.float32)
        for b in range(NUM_BUCKETS):
            vals = jnp.where(idx == b, rb_ref[b, h] * LOG2E, vals)

        for j0 in range(0, K_CHUNK, SUBLANES):
            rolled = pltpu.roll(vals, (j0 + Q_TILE + 1) % BAND_TABLE, 1, stride=1, stride_axis=0)
            out_ref[0, d, j0:j0 + SUBLANES, :] = rolled[:, :Q_TILE]


def _band_call(rel_bias):
    d = jnp.arange(BAND_CHUNKS, dtype=jnp.int32)[:, None, None]
    c = jnp.arange(BAND_TABLE, dtype=jnp.int32)[None, None, :]
    idx = _t5_bucket((d - 1) * K_CHUNK + (Q_TILE - 1) - c)
    idx = jnp.take(jnp.arange(NUM_BUCKETS, dtype=jnp.int32), idx)
    return pl.pallas_call(
        _band_kernel,
        grid=(A_HEADS,),
        in_specs=[pl.BlockSpec(memory_space=pltpu.SMEM),
                  pl.BlockSpec((BAND_CHUNKS, 1, BAND_TABLE), lambda h: (0, 0, 0))],
        out_specs=pl.BlockSpec((1, BAND_CHUNKS, K_CHUNK, Q_TILE), lambda h: (h, 0, 0, 0)),
        out_shape=jax.ShapeDtypeStruct((A_HEADS, BAND_CHUNKS, K_CHUNK, Q_TILE), jnp.float32),
        compiler_params=_cparams(1),
        name="bias_band",
    )(rel_bias, idx)


def _proj_kernel(x_ref, g_ref, wn_ref, wt_ref, cos_ref, sin_ref, qg_ref, kg_ref,
                 qa_ref, ka_ref, va_ref, sa_ref, qb_ref, kb_ref, vb_ref, sb_ref,
                 *, tiles_per_seq):
    f32, bf16 = jnp.float32, jnp.bfloat16
    x = x_ref[...]
    ms = jnp.mean(x * x, axis=-1, keepdims=True)
    h = (x * lax.rsqrt(ms + EPS) * g_ref[...]).astype(bf16)

    def nat(lo, hi):
        return jnp.dot(h, wn_ref[:, lo:hi], preferred_element_type=f32)

    def silu(v):
        return v * jax.nn.sigmoid(v)

    pt = lax.dot_general(wt_ref[...], h, (((1,), (1,)), ((), ())), preferred_element_type=f32)
    qa_ref[...] = (pt[_T_AQ:_T_AV] * Q_SCALE).astype(bf16)
    va_ref[...] = pt[_T_AV:_T_BQ].astype(bf16)
    vb_ref[...] = pt[_T_BV:_T_END].astype(bf16)

    cos = cos_ref[...]
    sin = sin_ref[...]
    half = B_HEAD // 2

    def norm_rope(xt, g):
        msq = jnp.mean(xt * xt, axis=0, keepdims=True)
        y = xt * lax.rsqrt(msq + EPS) * g
        e, o = y[:half], y[half:]
        return jnp.concatenate([e * cos - o * sin, e * sin + o * cos], axis=0)

    qg = qg_ref[...]
    qn = jnp.concatenate([norm_rope(pt[_T_BQ + i * B_HEAD:_T_BQ + (i + 1) * B_HEAD], qg)
                          for i in range(B_Q_HEADS)], axis=0) * Q_SCALE
    qb_ref[...] = qn.astype(bf16)
    kg = kg_ref[...]
    kn = jnp.concatenate([norm_rope(pt[_T_BK + i * B_HEAD:_T_BK + (i + 1) * B_HEAD], kg)
                          for i in range(B_KV_HEADS)], axis=0)
    kb_ref[...] = kn.T.astype(bf16)

    sa_ref[...] = silu(nat(_N_AG, _N_BG)).astype(bf16)
    sb_ref[...] = silu(nat(_N_BG, _N_END)).astype(bf16)

    ak = nat(_N_AK, _N_AG)
    lane = lax.broadcasted_iota(jnp.int32, (ROW_TILE, LANES), 1)
    row = lax.broadcasted_iota(jnp.int32, (ROW_TILE, LANES), 0)
    chunk = ((pl.program_id(0) % tiles_per_seq) * (ROW_TILE // K_CHUNK)
             + jnp.right_shift(row, K_CHUNK.bit_length() - 1))
    hot = jnp.where(_bias_slot_mask(lane, chunk), 1.0, 0.0)
    low = lane < A_QK_HEAD
    for hd in range(A_HEADS):
        akh = ak[:, hd * LANES:(hd + 1) * LANES]
        ka_ref[:, 2 * hd * LANES:(2 * hd + 1) * LANES] = jnp.where(low, akh, hot).astype(bf16)
        ka_ref[:, (2 * hd + 1) * LANES:(2 * hd + 2) * LANES] = jnp.where(low, hot, akh).astype(bf16)


def _proj_call(xf, g_pre, wn, wt, cos_t, sin_t, qg, kg, seq, layer):
    m = xf.shape[0]
    tiles_per_seq = seq // ROW_TILE
    bf16 = jnp.bfloat16
    row = lambda i: (i, 0)
    col = lambda i: (0, i)
    of_layer = lambda i: (layer, 0, 0)
    pos = lambda i: (0, i % tiles_per_seq)
    return pl.pallas_call(
        functools.partial(_proj_kernel, tiles_per_seq=tiles_per_seq),
        grid=(m // ROW_TILE,),
        in_specs=[pl.BlockSpec((ROW_TILE, D_MODEL), row),
                  pl.BlockSpec((None, 1, D_MODEL), of_layer),
                  pl.BlockSpec((None,) + wn.shape[1:], of_layer),
                  pl.BlockSpec((None,) + wt.shape[1:], of_layer),
                  pl.BlockSpec((B_HEAD // 2, ROW_TILE), pos),
                  pl.BlockSpec((B_HEAD // 2, ROW_TILE), pos),
                  pl.BlockSpec((None, B_HEAD, 1), of_layer),
                  pl.BlockSpec((None, B_HEAD, 1), of_layer)],
        out_specs=[pl.BlockSpec((A_WIDTH, ROW_TILE), col),
                   pl.BlockSpec((ROW_TILE, 2 * A_WIDTH), row),
                   pl.BlockSpec((A_WIDTH, ROW_TILE), col),
                   pl.BlockSpec((ROW_TILE, A_WIDTH), row),
                   pl.BlockSpec((B_WIDTH, ROW_TILE), col),
                   pl.BlockSpec((ROW_TILE, B_KV_HEADS * B_HEAD), row),
                   pl.BlockSpec((B_KV_HEADS * B_HEAD, ROW_TILE), col),
                   pl.BlockSpec((ROW_TILE, B_WIDTH), row)],
        out_shape=[jax.ShapeDtypeStruct((A_WIDTH, m), bf16),
                   jax.ShapeDtypeStruct((m, 2 * A_WIDTH), bf16),
                   jax.ShapeDtypeStruct((A_WIDTH, m), bf16),
                   jax.ShapeDtypeStruct((m, A_WIDTH), bf16),
                   jax.ShapeDtypeStruct((B_WIDTH, m), bf16),
                   jax.ShapeDtypeStruct((m, B_KV_HEADS * B_HEAD), bf16),
                   jax.ShapeDtypeStruct((B_KV_HEADS * B_HEAD, m), bf16),
                   jax.ShapeDtypeStruct((m, B_WIDTH), bf16)],
        compiler_params=_cparams(1),
        name="in_proj",
    )(xf, g_pre, wn, wt, cos_t, sin_t, qg, kg)


def _fold_rows(x, op):
    parts = [x[g * SUBLANES:(g + 1) * SUBLANES] for g in range(x.shape[0] // SUBLANES)]
    while len(parts) > 1:
        parts = [op(parts[i], parts[i + 1]) for i in range(0, len(parts), 2)]
    return parts[0]


def _pipeline_tiles(n_tiles, nq):
    t = pl.program_id(0)
    tile1 = jnp.minimum(t, n_tiles - 1)
    return t, tile1, tile1 % nq


def _fill_first_steps(t, s_odd, m_odd, acc_ref, sum_ref):
    @pl.when(t == 0)
    def _():
        s_odd[...] = jnp.zeros(s_odd.shape, jnp.float32)
        m_odd[...] = jnp.zeros(m_odd.shape, jnp.float32)
        acc_ref[...] = jnp.zeros(acc_ref.shape, jnp.float32)
        sum_ref[...] = jnp.ones(sum_ref.shape, jnp.float32)


def _by_parity(t, step, s_even, m_even, s_odd, m_odd):
    @pl.when(t % 2 == 0)
    def _():
        step(s_even, m_even, s_odd, m_odd)

    @pl.when(t % 2 == 1)
    def _():
        step(s_odd, m_odd, s_even, m_even)


def _attn_gqa_kernel(q_ref, k_ref, v_ref, gate_ref, o_ref,
                     w_ref, s_even, s_odd, m_even, m_odd, acc_ref, sum_ref,
                     *, n_chunks, nq, n_tiles):
    f32, bf16 = jnp.float32, jnp.bfloat16
    t, _, _ = _pipeline_tiles(n_tiles, nq)
    _fill_first_steps(t, s_odd, m_odd, acc_ref, sum_ref)

    def step(s1_ref, m1_ref, s2_ref, m2_ref):
        top = lax.broadcasted_iota(jnp.int32, (LANES, Q_TILE), 0) < B_HEAD
        qt = q_ref[...].astype(f32)
        w_ref[0] = jnp.where(top, qt, 0.0).astype(bf16)
        w_ref[1] = jnp.where(top, 0.0, qt).astype(bf16)

        ot = jnp.concatenate([acc_ref[st] / jnp.sum(sum_ref[st], axis=0, keepdims=True)
                              for st in range(2)], axis=0)
        o_ref[...] = (ot.T * gate_ref[...].astype(f32)).astype(o_ref.dtype)

        m_row = [jnp.max(m2_ref[st], axis=0, keepdims=True) for st in range(2)]

        m_acc = [None, None]
        l_acc = [None, None]
        o_acc = [None, None]
        for c in range(n_chunks):
            keys = slice(c * K_CHUNK, (c + 1) * K_CHUNK)
            for st in range(2):
                s = jnp.dot(k_ref[keys, :], w_ref[st], preferred_element_type=f32)
                s1_ref[c, st] = s
                cm = _fold_rows(s, jnp.maximum)
                m_acc[st] = cm if c == 0 else jnp.maximum(m_acc[st], cm)
            for st in range(2):
                p = jnp.exp2(s2_ref[c, st] - m_row[st])
                ps = _fold_rows(p, jnp.add)
                l_acc[st] = ps if c == 0 else l_acc[st] + ps
                part = jnp.dot(v_ref[st * B_HEAD:(st + 1) * B_HEAD, keys], p.astype(bf16),
                               preferred_element_type=f32)
                o_acc[st] = part if c == 0 else o_acc[st] + part
        for st in range(2):
            m1_ref[st] = m_acc[st]
            acc_ref[st] = o_acc[st]
            sum_ref[st] = l_acc[st]

    _by_parity(t, step, s_even, m_even, s_odd, m_odd)


def _attn_diff_kernel(rb_ref, lamp_ref, g_ref, q_ref, k_ref, v_ref, gate_ref, band_ref, o_ref,
                      w_ref, s_even, s_odd, m_even, m_odd, acc_ref, sum_ref,
                      *, lam_init, n_chunks, nq, n_tiles):
    f32, bf16 = jnp.float32, jnp.bfloat16
    t, tile1, qi1 = _pipeline_tiles(n_tiles, nq)
    head1 = (tile1 // nq) % A_HEADS
    _fill_first_steps(t, s_odd, m_odd, acc_ref, sum_ref)

    def step(s1_ref, m1_ref, s2_ref, m2_ref):
        row = lax.broadcasted_iota(jnp.int32, (LANES, LANES), 0)
        slot = jnp.bitwise_and(row, BIAS_SLOTS - 1)
        term = jnp.right_shift(jnp.bitwise_and(row, A_QK_HEAD - 1), BIAS_SLOT_BITS)
        left = rb_ref[NUM_BUCKETS // 2 - 1, head1] * LOG2E
        right = rb_ref[NUM_BUCKETS - 1, head1] * LOG2E
        const = jnp.where(jnp.abs(slot - qi1) <= 1, 0.0, jnp.where(slot < qi1, left, right))
        t0 = const.astype(bf16).astype(f32)
        t1 = (const - t0).astype(bf16).astype(f32)
        t2 = (const - t0 - t1).astype(bf16).astype(f32)
        bias_tile = jnp.where(term == 0, t0, jnp.where(term == 1, t1,
                                                      jnp.where(term == 2, t2, 0.0)))
        bias_rows = jnp.concatenate([bias_tile] * (Q_TILE // LANES), axis=1)
        top = lax.broadcasted_iota(jnp.int32, (LANES, Q_TILE), 0) < A_QK_HEAD
        qt = q_ref[...].astype(f32)
        w_ref[0] = jnp.where(top, qt, bias_rows).astype(bf16)
        w_ref[1] = jnp.where(top, bias_rows, qt).astype(bf16)

        l_row = [jnp.sum(sum_ref[st], axis=0, keepdims=True) for st in range(2)]
        lp = lamp_ref[...]
        lam = (jnp.exp(jnp.sum(lp[0:1] * lp[1:2], axis=-1, keepdims=True))
               - jnp.exp(jnp.sum(lp[2:3] * lp[3:4], axis=-1, keepdims=True)) + lam_init)
        o = (acc_ref[0] / l_row[0] - lam * (acc_ref[1] / l_row[1])).T
        ms = jnp.mean(o * o, axis=-1, keepdims=True)
        o = (o * lax.rsqrt(ms + EPS) * g_ref[...]) * (1.0 - lam_init)
        o_ref[...] = (o * gate_ref[...].astype(f32)).astype(o_ref.dtype)

        m_row = [jnp.max(m2_ref[st], axis=0, keepdims=True) for st in range(2)]

        m_acc = [None, None]
        l_acc = [None, None]
        o_acc = [None, None]
        for i in range(n_chunks):
            c1 = qi1 - 1 + i
            j1 = jnp.bitwise_and(c1, n_chunks - 1)
            key0 = pl.multiple_of(j1 * K_CHUNK, K_CHUNK)
            if i < BAND_CHUNKS:
                in_range = jnp.logical_and(c1 >= 0, c1 < n_chunks)
                band = jnp.where(in_range, band_ref[0, i], 0.0)
            for st in range(2):
                s = jnp.dot(k_ref[pl.ds(key0, K_CHUNK), st * LANES:(st + 1) * LANES], w_ref[st],
                            preferred_element_type=f32)
                if i < BAND_CHUNKS:
                    s = s + band
                s1_ref[j1, st] = s
                cm = _fold_rows(s, jnp.maximum)
                m_acc[st] = cm if i == 0 else jnp.maximum(m_acc[st], cm)
            for st in range(2):
                p = jnp.exp2(s2_ref[i, st] - m_row[st])
                ps = _fold_rows(p, jnp.add)
                l_acc[st] = ps if i == 0 else l_acc[st] + ps
                part = jnp.dot(v_ref[:, i * K_CHUNK:(i + 1) * K_CHUNK], p.astype(bf16),
                               preferred_element_type=f32)
                o_acc[st] = part if i == 0 else o_acc[st] + part
        for st in range(2):
            m1_ref[st] = m_acc[st]
            acc_ref[st] = o_acc[st]
            sum_ref[st] = l_acc[st]

    _by_parity(t, step, s_even, m_even, s_odd, m_odd)


def _attn_call(kernel_fn, operands, prefix_specs, q_arr, k_arr, v_arr, gate_arr, band_arr,
               *, n_pairs, per_pair, batch, seq, name):
    m = gate_arr.shape[0]
    nq = seq // Q_TILE
    n_chunks = seq // K_CHUNK
    n_tiles = batch * n_pairs * nq
    k_lanes = k_arr.shape[1] // (n_pairs if per_pair else 1)
    v_rows = LANES if per_pair else B_HEAD

    def decode(tile):
        return tile // (n_pairs * nq), (tile // nq) % n_pairs, tile % nq

    def stage1(t):
        return decode(jnp.minimum(t, n_tiles - 1))

    def stage2(t):
        return decode(jnp.clip(t - 1, 0, n_tiles - 1))

    def stage3(t):
        return decode(jnp.maximum(t - 2, 0))

    def q_map(t):
        b, p, i = stage1(t)
        return p, b * nq + i

    def k_map(t):
        b, p, _ = stage1(t)
        return b, p if per_pair else 0

    def v_map(t):
        b, p, _ = stage2(t)
        return p if per_pair else 0, b

    def out_map(t):
        b, p, i = stage3(t)
        return b * nq + i, p

    in_specs = list(prefix_specs) + [
        pl.BlockSpec((LANES, Q_TILE), q_map),
        pl.BlockSpec((seq, k_lanes), k_map),
        pl.BlockSpec((LANES, seq), v_map),
        pl.BlockSpec((Q_TILE, LANES), out_map)]
    tail = ()
    if band_arr is not None:
        in_specs.append(pl.BlockSpec((1, BAND_CHUNKS, K_CHUNK, Q_TILE),
                                     lambda t: (stage1(t)[1], 0, 0, 0)))
        tail = (band_arr,)
    scores = pltpu.VMEM((n_chunks, 2, K_CHUNK, Q_TILE), jnp.float32)
    per_key_group = pltpu.VMEM((2, SUBLANES, Q_TILE), jnp.float32)
    scratch = [pltpu.VMEM((2, LANES, Q_TILE), jnp.bfloat16),
               scores, scores,
               per_key_group, per_key_group,
               pltpu.VMEM((2, v_rows, Q_TILE), jnp.float32),
               per_key_group]
    return pl.pallas_call(
        functools.partial(kernel_fn, n_chunks=n_chunks, nq=nq, n_tiles=n_tiles),
        grid=(n_tiles + 2,),
        in_specs=in_specs,
        out_specs=pl.BlockSpec((Q_TILE, LANES), out_map),
        out_shape=jax.ShapeDtypeStruct((m, n_pairs * LANES), jnp.bfloat16),
        scratch_shapes=scratch,
        compiler_params=_cparams(1),
        name=name,
    )(*operands, q_arr, k_arr, v_arr, gate_arr, *tail)


def _attn_diff_call(rel_bias, lam_params, subln_g, qa, ka, va, sa, band, lam_init, batch, seq,
                    layer):
    n_chunks = seq // K_CHUNK
    assert n_chunks <= BIAS_SLOTS and n_chunks & (n_chunks - 1) == 0
    prefix = [pl.BlockSpec(memory_space=pltpu.SMEM),
              pl.BlockSpec((None,) + lam_params.shape[1:], lambda t: (layer, 0, 0)),
              pl.BlockSpec((None, 1, A_V_HEAD), lambda t: (layer, 0, 0))]
    return _attn_call(functools.partial(_attn_diff_kernel, lam_init=lam_init),
                      (rel_bias, lam_params, subln_g), prefix, qa, ka, va, sa, band,
                      n_pairs=A_HEADS, per_pair=True, batch=batch, seq=seq, name="attn_diff")


def _attn_gqa_call(qb, kb, vb, sb, batch, seq):
    return _attn_call(_attn_gqa_kernel, (), [], qb, kb, vb, sb, None,
                      n_pairs=B_Q_HEADS // 2, per_pair=False, batch=batch, seq=seq,
                      name="attn_gqa")


def _out_kernel(ya_ref, yb_ref, wa_ref, wb_ref, x_ref, g_ref, o_ref):
    f32 = jnp.float32
    y = (jnp.dot(ya_ref[...], wa_ref[...], preferred_element_type=f32)
         + jnp.dot(yb_ref[...], wb_ref[...], preferred_element_type=f32))
    ms = jnp.mean(y * y, axis=-1, keepdims=True)
    o_ref[...] = x_ref[...] + y * lax.rsqrt(ms + EPS) * g_ref[...]


def _out_call(ya, yb, wa, wb, xf, g_post, layer):
    m = xf.shape[0]
    row = lambda i: (i, 0)
    return pl.pallas_call(
        _out_kernel,
        grid=(m // ROW_TILE,),
        in_specs=[pl.BlockSpec((ROW_TILE, A_WIDTH), row),
                  pl.BlockSpec((ROW_TILE, B_WIDTH), row),
                  pl.BlockSpec((None, A_WIDTH, D_MODEL), lambda i: (layer, 0, 0)),
                  pl.BlockSpec((None, B_WIDTH, D_MODEL), lambda i: (layer, 0, 0)),
                  pl.BlockSpec((ROW_TILE, D_MODEL), row),
                  pl.BlockSpec((None, 1, D_MODEL), lambda i: (layer, 0, 0))],
        out_specs=pl.BlockSpec((ROW_TILE, D_MODEL), row),
        out_shape=jax.ShapeDtypeStruct((m, D_MODEL), jnp.float32),
        compiler_params=_cparams(1),
        name="out_proj",
    )(ya, yb, wa, wb, xf, g_post)


def _layer_weights(w_in_l, w_out_l, q_norm_g_l, k_norm_g_l):
    bf16 = jnp.bfloat16
    d = w_in_l.shape[0]
    per_group = B_Q_HEADS // B_KV_HEADS
    half = B_HEAD // 2
    bq = w_in_l[:, _BQ:_BK].reshape(d, B_KV_HEADS, per_group, half, 2)
    bq = bq.transpose(0, 2, 1, 4, 3).reshape(d, B_WIDTH)
    bk = w_in_l[:, _BK:_BV].reshape(d, B_KV_HEADS, half, 2)
    bk = bk.transpose(0, 1, 3, 2).reshape(d, B_KV_HEADS * B_HEAD)
    bg = w_in_l[:, _BG:].reshape(d, B_KV_HEADS, per_group, B_HEAD)
    bg = bg.transpose(0, 2, 1, 3).reshape(d, B_WIDTH)
    wn = jnp.concatenate([w_in_l[:, _AK:_AV], w_in_l[:, _AG:_BQ], bg], axis=1).astype(bf16)
    wt = jnp.concatenate([w_in_l[:, _AQ:_AK], w_in_l[:, _AV:_AG], bq, bk, w_in_l[:, _BV:_BG]],
                         axis=1).astype(bf16).T
    wa = w_out_l[:A_WIDTH].astype(bf16)
    wb = w_out_l[A_WIDTH:].reshape(B_KV_HEADS, per_group, B_HEAD, -1)
    wb = wb.transpose(1, 0, 2, 3).reshape(B_WIDTH, -1).astype(bf16)
    qg = q_norm_g_l.reshape(half, 2).T.reshape(B_HEAD, 1)
    kg = k_norm_g_l.reshape(half, 2).T.reshape(B_HEAD, 1)
    return wn, wt, wa, wb, qg, kg


def kernel(x, rel_bias, pre_norm_g, w_in, diff_lambda, diff_subln_g, q_norm_g, k_norm_g,
           w_out, post_norm_g):
    batch, seq, d_model = x.shape
    xf = x.reshape(batch * seq, d_model)
    band = _band_call(rel_bias)
    cos_t, sin_t = _rope_tables_t(seq)
    wn, wt, wa, wb, qg, kg = jax.vmap(_layer_weights)(w_in, w_out, q_norm_g, k_norm_g)
    g_pre = pre_norm_g.reshape(DEPTH, 1, d_model)
    g_post = post_norm_g.reshape(DEPTH, 1, d_model)
    g_subln = diff_subln_g.reshape(DEPTH, 1, A_V_HEAD)
    for l in range(DEPTH):
        lam_init = 0.8 - 0.6 * math.exp(-0.3 * l)
        qa, ka, va, sa, qb, kb, vb, sb = _proj_call(
            xf, g_pre, wn, wt, cos_t, sin_t, qg, kg, seq, l)
        ya = _attn_diff_call(rel_bias, diff_lambda, g_subln, qa, ka, va, sa, band, lam_init,
                             batch, seq, l)
        yb = _attn_gqa_call(qb, kb, vb, sb, batch, seq)
        xf = _out_call(ya, yb, wa, wb, xf, g_post, l)
    return xf.reshape(batch, seq, d_model)
```

```python
import functools
import math

import jax
import jax.numpy as jnp
from jax import lax
from jax.experimental import pallas as pl
from jax.experimental.pallas import tpu as pltpu

D_MODEL = 1024
DEPTH = 2
A_WIDTH = 512
B_WIDTH = 512
A_HEADS = 4
A_V_HEAD = 128
A_QK_HEAD = 64
B_HEAD = 64
B_Q_HEADS = 8
B_KV_HEADS = 2
GRID_W = 64
ROPE_THETA = 10000.0
NUM_BUCKETS = 32
MAX_DISTANCE = 128
EPS = 1e-6
LOG2E = math.log2(math.e)
Q_SCALE = (A_QK_HEAD ** -0.5) * LOG2E

LANES = 128
SUBLANES = 8
ROW_TILE = 1024
Q_TILE = 256
K_CHUNK = 256
BAND_CHUNKS = 3
BIAS_SLOTS = 16
BIAS_SLOT_BITS = BIAS_SLOTS.bit_length() - 1
BIAS_TERMS = 3
VMEM_LIMIT = 62 * 1024 * 1024

_AQ, _AK, _AV, _AG, _BQ, _BK, _BV, _BG = 0, 512, 1024, 1536, 2048, 2560, 2688, 2816
_T_AQ, _T_AV, _T_BQ, _T_BK, _T_BV, _T_END = 0, 512, 1024, 1536, 1664, 1792
_N_AK, _N_AG, _N_BG, _N_END = 0, 512, 1024, 1536


def _t5_bucket(rel):
    nb = NUM_BUCKETS // 2
    max_exact = nb // 2
    n = jnp.abs(rel)
    large = max_exact + (jnp.log(jnp.maximum(n, 1).astype(jnp.float32) / max_exact)
                         / math.log(MAX_DISTANCE / max_exact) * (nb - max_exact)).astype(jnp.int32)
    large = jnp.minimum(large, nb - 1)
    return jnp.where(rel > 0, nb, 0) + jnp.where(n < max_exact, n, large)


def _rope_tables_t(n):
    rows = n // GRID_W
    axis_dim = B_HEAD // 2
    inv = ROPE_THETA ** (-jnp.arange(0, axis_dim, 2, dtype=jnp.float32) / axis_dim)
    row_ang = (jnp.arange(rows, dtype=jnp.float32)[:, None] * inv).T
    col_ang = (jnp.arange(GRID_W, dtype=jnp.float32)[:, None] * inv).T

    def expand(fn):
        by_row = jnp.repeat(fn(row_ang), GRID_W, axis=1)
        by_col = jnp.tile(fn(col_ang), (1, rows))
        return jnp.concatenate([by_row, by_col], axis=0)

    return expand(jnp.cos), expand(jnp.sin)


def _cparams(n_axes):
    return pltpu.CompilerParams(dimension_semantics=("arbitrary",) * n_axes,
                                vmem_limit_bytes=VMEM_LIMIT)


def _bias_slot_mask(slot_index, chunk):
    term = jnp.right_shift(jnp.bitwise_and(slot_index, A_QK_HEAD - 1), BIAS_SLOT_BITS)
    in_terms = term < BIAS_TERMS
    return jnp.logical_and(in_terms, jnp.bitwise_and(slot_index, BIAS_SLOTS - 1) == chunk)


BAND_TABLE = 2 * Q_TILE


def _band_kernel(rb_ref, idx_ref, out_ref):
    h = pl.program_id(0)
    for d in range(BAND_CHUNKS):
        idx = jnp.broadcast_to(idx_ref[d], (SUBLANES, BAND_TABLE))
        vals = jnp.zeros(idx.shape, jnp.float32)
        for b in range(NUM_BUCKETS):
            vals = jnp.where(idx == b, rb_ref[b, h] * LOG2E, vals)

        for j0 in range(0, K_CHUNK, SUBLANES):
            rolled = pltpu.roll(vals, (j0 + Q_TILE + 1) % BAND_TABLE, 1, stride=1, stride_axis=0)
            out_ref[0, d, j0:j0 + SUBLANES, :] = rolled[:, :Q_TILE]


def _band_call(rel_bias):
    d = jnp.arange(BAND_CHUNKS, dtype=jnp.int32)[:, None, None]
    c = jnp.arange(BAND_TABLE, dtype=jnp.int32)[None, None, :]
    idx = _t5_bucket((d - 1) * K_CHUNK + (Q_TILE - 1) - c)
    idx = jnp.take(jnp.arange(NUM_BUCKETS, dtype=jnp.int32), idx)
    return pl.pallas_call(
        _band_kernel,
        grid=(A_HEADS,),
        in_specs=[pl.BlockSpec(memory_space=pltpu.SMEM),
                  pl.BlockSpec((BAND_CHUNKS, 1, BAND_TABLE), lambda h: (0, 0, 0))],
        out_specs=pl.BlockSpec((1, BAND_CHUNKS, K_CHUNK, Q_TILE), lambda h: (h, 0, 0, 0)),
        out_shape=jax.ShapeDtypeStruct((A_HEADS, BAND_CHUNKS, K_CHUNK, Q_TILE), jnp.float32),
        compiler_params=_cparams(1),
        name="bias_band",
    )(rel_bias, idx)


def _proj_kernel(x_ref, g_ref, wn_ref, wt_ref, cos_ref, sin_ref, qg_ref, kg_ref,
                 qa_ref, ka_ref, va_ref, sa_ref, qb_ref, kb_ref, vb_ref, sb_ref,
                 *, tiles_per_seq):
    f32, bf16 = jnp.float32, jnp.bfloat16
    x = x_ref[...]
    ms = jnp.mean(x * x, axis=-1, keepdims=True)
    h = (x * lax.rsqrt(ms + EPS) * g_ref[...]).astype(bf16)

    def nat(lo, hi):
        return jnp.dot(h, wn_ref[:, lo:hi], preferred_element_type=f32)

    def silu(v):
        return v * jax.nn.sigmoid(v)

    pt = lax.dot_general(wt_ref[...], h, (((1,), (1,)), ((), ())), preferred_element_type=f32)
    qa_ref[...] = (pt[_T_AQ:_T_AV] * Q_SCALE).astype(bf16)
    va_ref[...] = pt[_T_AV:_T_BQ].astype(bf16)
    vb_ref[...] = pt[_T_BV:_T_END].astype(bf16)

    cos = cos_ref[...]
    sin = sin_ref[...]
    half = B_HEAD // 2

    def norm_rope(xt, g):
        msq = jnp.mean(xt * xt, axis=0, keepdims=True)
        y = xt * lax.rsqrt(msq + EPS) * g
        e, o = y[:half], y[half:]
        return jnp.concatenate([e * cos - o * sin, e * sin + o * cos], axis=0)

    qg = qg_ref[...]
    qn = jnp.concatenate([norm_rope(pt[_T_BQ + i * B_HEAD:_T_BQ + (i + 1) * B_HEAD], qg)
                          for i in range(B_Q_HEADS)], axis=0) * Q_SCALE
    qb_ref[...] = qn.astype(bf16)
    kg = kg_ref[...]
    kn = jnp.concatenate([norm_rope(pt[_T_BK + i * B_HEAD:_T_BK + (i + 1) * B_HEAD], kg)
                          for i in range(B_KV_HEADS)], axis=0)
    kb_ref[...] = kn.T.astype(bf16)

    sa_ref[...] = silu(nat(_N_AG, _N_BG)).astype(bf16)
    sb_ref[...] = silu(nat(_N_BG, _N_END)).astype(bf16)

    ak = nat(_N_AK, _N_AG)
    lane = lax.broadcasted_iota(jnp.int32, (ROW_TILE, LANES), 1)
    row = lax.broadcasted_iota(jnp.int32, (ROW_TILE, LANES), 0)
    chunk = ((pl.program_id(0) % tiles_per_seq) * (ROW_TILE // K_CHUNK)
             + jnp.right_shift(row, K_CHUNK.bit_length() - 1))
    hot = jnp.where(_bias_slot_mask(lane, chunk), 1.0, 0.0)
    low = lane < A_QK_HEAD
    for hd in range(A_HEADS):
        akh = ak[:, hd * LANES:(hd + 1) * LANES]
        ka_ref[:, 2 * hd * LANES:(2 * hd + 1) * LANES] = jnp.where(low, akh, hot).astype(bf16)
        ka_ref[:, (2 * hd + 1) * LANES:(2 * hd + 2) * LANES] = jnp.where(low, hot, akh).astype(bf16)


def _proj_call(xf, g_pre, wn, wt, cos_t, sin_t, qg, kg, seq, layer):
    m = xf.shape[0]
    tiles_per_seq = seq // ROW_TILE
    bf16 = jnp.bfloat16
    row = lambda i: (i, 0)
    col = lambda i: (0, i)
    of_layer = lambda i: (layer, 0, 0)
    pos = lambda i: (0, i % tiles_per_seq)
    return pl.pallas_call(
        functools.partial(_proj_kernel, tiles_per_seq=tiles_per_seq),
        grid=(m // ROW_TILE,),
        in_specs=[pl.BlockSpec((ROW_TILE, D_MODEL), row),
                  pl.BlockSpec((None, 1, D_MODEL), of_layer),
                  pl.BlockSpec((None,) + wn.shape[1:], of_layer),
                  pl.BlockSpec((None,) + wt.shape[1:], of_layer),
                  pl.BlockSpec((B_HEAD // 2, ROW_TILE), pos),
                  pl.BlockSpec((B_HEAD // 2, ROW_TILE), pos),
                  pl.BlockSpec((None, B_HEAD, 1), of_layer),
                  pl.BlockSpec((None, B_HEAD, 1), of_layer)],
        out_specs=[pl.BlockSpec((A_WIDTH, ROW_TILE), col),
                   pl.BlockSpec((ROW_TILE, 2 * A_WIDTH), row),
                   pl.BlockSpec((A_WIDTH, ROW_TILE), col),
                   pl.BlockSpec((ROW_TILE, A_WIDTH), row),
                   pl.BlockSpec((B_WIDTH, ROW_TILE), col),
                   pl.BlockSpec((ROW_TILE, B_KV_HEADS * B_HEAD), row),
                   pl.BlockSpec((B_KV_HEADS * B_HEAD, ROW_TILE), col),
                   pl.BlockSpec((ROW_TILE, B_WIDTH), row)],
        out_shape=[jax.ShapeDtypeStruct((A_WIDTH, m), bf16),
                   jax.ShapeDtypeStruct((m, 2 * A_WIDTH), bf16),
                   jax.ShapeDtypeStruct((A_WIDTH, m), bf16),
                   jax.ShapeDtypeStruct((m, A_WIDTH), bf16),
                   jax.ShapeDtypeStruct((B_WIDTH, m), bf16),
                   jax.ShapeDtypeStruct((m, B_KV_HEADS * B_HEAD), bf16),
                   jax.ShapeDtypeStruct((B_KV_HEADS * B_HEAD, m), bf16),
                   jax.ShapeDtypeStruct((m, B_WIDTH), bf16)],
        compiler_params=_cparams(1),
        name="in_proj",
    )(xf, g_pre, wn, wt, cos_t, sin_t, qg, kg)


def _fold_rows(x, op):
    parts = [x[g * SUBLANES:(g + 1) * SUBLANES] for g in range(x.shape[0] // SUBLANES)]
    while len(parts) > 1:
        parts = [op(parts[i], parts[i + 1]) for i in range(0, len(parts), 2)]
    return parts[0]


def _pipeline_tiles(n_tiles, nq):
    t = pl.program_id(0)
    tile1 = jnp.minimum(t, n_tiles - 1)
    return t, tile1, tile1 % nq


def _fill_first_steps(t, s_odd, m_odd, acc_ref, sum_ref):
    @pl.when(t == 0)
    def _():
        s_odd[...] = jnp.zeros(s_odd.shape, jnp.float32)
        m_odd[...] = jnp.zeros(m_odd.shape, jnp.float32)
        acc_ref[...] = jnp.zeros(acc_ref.shape, jnp.float32)
        sum_ref[...] = jnp.ones(sum_ref.shape, jnp.float32)


def _by_parity(t, step, s_even, m_even, s_odd, m_odd):
    @pl.when(t % 2 == 0)
    def _():
        step(s_even, m_even, s_odd, m_odd)

    @pl.when(t % 2 == 1)
    def _():
        step(s_odd, m_odd, s_even, m_even)


def _attn_gqa_kernel(q_ref, k_ref, v_ref, gate_ref, o_ref,
                     w_ref, s_even, s_odd, m_even, m_odd, acc_ref, sum_ref,
                     *, n_chunks, nq, n_tiles):
    f32, bf16 = jnp.float32, jnp.bfloat16
    t, _, _ = _pipeline_tiles(n_tiles, nq)
    _fill_first_steps(t, s_odd, m_odd, acc_ref, sum_ref)

    def step(s1_ref, m1_ref, s2_ref, m2_ref):
        top = lax.broadcasted_iota(jnp.int32, (LANES, Q_TILE), 0) < B_HEAD
        qt = q_ref[...].astype(f32)
        w_ref[0] = jnp.where(top, qt, 0.0).astype(bf16)
        w_ref[1] = jnp.where(top, 0.0, qt).astype(bf16)

        ot = jnp.concatenate([acc_ref[st] / jnp.sum(sum_ref[st], axis=0, keepdims=True)
                              for st in range(2)], axis=0)
        o_ref[...] = (ot.T * gate_ref[...].astype(f32)).astype(o_ref.dtype)

        m_row = [jnp.max(m2_ref[st], axis=0, keepdims=True) for st in range(2)]

        m_acc = [None, None]
        l_acc = [None, None]
        o_acc = [None, None]
        for c in range(n_chunks):
            keys = slice(c * K_CHUNK, (c + 1) * K_CHUNK)
            for st in range(2):
                s = jnp.dot(k_ref[keys, :], w_ref[st], preferred_element_type=f32)
                s1_ref[c, st] = s
                cm = _fold_rows(s, jnp.maximum)
                m_acc[st] = cm if c == 0 else jnp.maximum(m_acc[st], cm)
            for st in range(2):
                p = jnp.exp2(s2_ref[c, st] - m_row[st])
                ps = _fold_rows(p, jnp.add)
                l_acc[st] = ps if c == 0 else l_acc[st] + ps
                part = jnp.dot(v_ref[st * B_HEAD:(st + 1) * B_HEAD, keys], p.astype(bf16),
                               preferred_element_type=f32)
                o_acc[st] = part if c == 0 else o_acc[st] + part
        for st in range(2):
            m1_ref[st] = m_acc[st]
            acc_ref[st] = o_acc[st]
            sum_ref[st] = l_acc[st]

    _by_parity(t, step, s_even, m_even, s_odd, m_odd)


def _attn_diff_kernel(rb_ref, lamp_ref, g_ref, q_ref, k_ref, v_ref, gate_ref, band_ref, o_ref,
                      w_ref, s_even, s_odd, m_even, m_odd, acc_ref, sum_ref,
                      *, lam_init, n_chunks, nq, n_tiles):
    f32, bf16 = jnp.float32, jnp.bfloat16
    t, tile1, qi1 = _pipeline_tiles(n_tiles, nq)
    head1 = (tile1 // nq) % A_HEADS
    _fill_first_steps(t, s_odd, m_odd, acc_ref, sum_ref)

    def step(s1_ref, m1_ref, s2_ref, m2_ref):
        row = lax.broadcasted_iota(jnp.int32, (LANES, LANES), 0)
        slot = jnp.bitwise_and(row, BIAS_SLOTS - 1)
        term = jnp.right_shift(jnp.bitwise_and(row, A_QK_HEAD - 1), BIAS_SLOT_BITS)
        left = rb_ref[NUM_BUCKETS // 2 - 1, head1] * LOG2E
        right = rb_ref[NUM_BUCKETS - 1, head1] * LOG2E
        const = jnp.where(jnp.abs(slot - qi1) <= 1, 0.0, jnp.where(slot < qi1, left, right))
        t0 = const.astype(bf16).astype(f32)
        t1 = (const - t0).astype(bf16).astype(f32)
        t2 = (const - t0 - t1).astype(bf16).astype(f32)
        bias_tile = jnp.where(term == 0, t0, jnp.where(term == 1, t1,
                                                      jnp.where(term == 2, t2, 0.0)))
        bias_rows = jnp.concatenate([bias_tile] * (Q_TILE // LANES), axis=1)
        top = lax.broadcasted_iota(jnp.int32, (LANES, Q_TILE), 0) < A_QK_HEAD
        qt = q_ref[...].astype(f32)
        w_ref[0] = jnp.where(top, qt, bias_rows).astype(bf16)
        w_ref[1] = jnp.where(top, bias_rows, qt).astype(bf16)

        l_row = [jnp.sum(sum_ref[st], axis=0, keepdims=True) for st in range(2)]
        lp = lamp_ref[...]
        lam = (jnp.exp(jnp.sum(lp[0:1] * lp[1:2], axis=-1, keepdims=True))
               - jnp.exp(jnp.sum(lp[2:3] * lp[3:4], axis=-1, keepdims=True)) + lam_init)
        o = (acc_ref[0] / l_row[0] - lam * (acc_ref[1] / l_row[1])).T
        ms = jnp.mean(o * o, axis=-1, keepdims=True)
        o = (o * lax.rsqrt(ms + EPS) * g_ref[...]) * (1.0 - lam_init)
        o_ref[...] = (o * gate_ref[...].astype(f32)).astype(o_ref.dtype)

        m_row = [jnp.max(m2_ref[st], axis=0, keepdims=True) for st in range(2)]

        m_acc = [None, None]
        l_acc = [None, None]
        o_acc = [None, None]
        for i in range(n_chunks):
            c1 = qi1 - 1 + i
            j1 = jnp.bitwise_and(c1, n_chunks - 1)
            key0 = pl.multiple_of(j1 * K_CHUNK, K_CHUNK)
            if i < BAND_CHUNKS:
                in_range = jnp.logical_and(c1 >= 0, c1 < n_chunks)
                band = jnp.where(in_range, band_ref[0, i], 0.0)
            for st in range(2):
                s = jnp.dot(k_ref[pl.ds(key0, K_CHUNK), st * LANES:(st + 1) * LANES], w_ref[st],
                            preferred_element_type=f32)
                if i < BAND_CHUNKS:
                    s = s + band
                s1_ref[j1, st] = s
                cm = _fold_rows(s, jnp.maximum)
                m_acc[st] = cm if i == 0 else jnp.maximum(m_acc[st], cm)
            for st in range(2):
                p = jnp.exp2(s2_ref[i, st] - m_row[st])
                ps = _fold_rows(p, jnp.add)
                l_acc[st] = ps if i == 0 else l_acc[st] + ps
                part = jnp.dot(v_ref[:, i * K_CHUNK:(i + 1) * K_CHUNK], p.astype(bf16),
                               preferred_element_type=f32)
                o_acc[st] = part if i == 0 else o_acc[st] + part
        for st in range(2):
            m1_ref[st] = m_acc[st]
            acc_ref[st] = o_acc[st]
            sum_ref[st] = l_acc[st]

    _by_parity(t, step, s_even, m_even, s_odd, m_odd)


def _attn_call(kernel_fn, operands, prefix_specs, q_arr, k_arr, v_arr, gate_arr, band_arr,
               *, n_pairs, per_pair, batch, seq, name):
    m = gate_arr.shape[0]
    nq = seq // Q_TILE
    n_chunks = seq // K_CHUNK
    n_tiles = batch * n_pairs * nq
    k_lanes = k_arr.shape[1] // (n_pairs if per_pair else 1)
    v_rows = LANES if per_pair else B_HEAD

    def decode(tile):
        return tile // (n_pairs * nq), (tile // nq) % n_pairs, tile % nq

    def stage1(t):
        return decode(jnp.minimum(t, n_tiles - 1))

    def stage2(t):
        return decode(jnp.clip(t - 1, 0, n_tiles - 1))

    def stage3(t):
        return decode(jnp.maximum(t - 2, 0))

    def q_map(t):
        b, p, i = stage1(t)
        return p, b * nq + i

    def k_map(t):
        b, p, _ = stage1(t)
        return b, p if per_pair else 0

    def v_map(t):
        b, p, _ = stage2(t)
        return p if per_pair else 0, b

    def out_map(t):
        b, p, i = stage3(t)
        return b * nq + i, p

    in_specs = list(prefix_specs) + [
        pl.BlockSpec((LANES, Q_TILE), q_map),
        pl.BlockSpec((seq, k_lanes), k_map),
        pl.BlockSpec((LANES, seq), v_map),
        pl.BlockSpec((Q_TILE, LANES), out_map)]
    tail = ()
    if band_arr is not None:
        in_specs.append(pl.BlockSpec((1, BAND_CHUNKS, K_CHUNK, Q_TILE),
                                     lambda t: (stage1(t)[1], 0, 0, 0)))
        tail = (band_arr,)
    scores = pltpu.VMEM((n_chunks, 2, K_CHUNK, Q_TILE), jnp.float32)
    per_key_group = pltpu.VMEM((2, SUBLANES, Q_TILE), jnp.float32)
    scratch = [pltpu.VMEM((2, LANES, Q_TILE), jnp.bfloat16),
               scores, scores,
               per_key_group, per_key_group,
               pltpu.VMEM((2, v_rows, Q_TILE), jnp.float32),
               per_key_group]
    return pl.pallas_call(
        functools.partial(kernel_fn, n_chunks=n_chunks, nq=nq, n_tiles=n_tiles),
        grid=(n_tiles + 2,),
        in_specs=in_specs,
        out_specs=pl.BlockSpec((Q_TILE, LANES), out_map),
        out_shape=jax.ShapeDtypeStruct((m, n_pairs * LANES), jnp.bfloat16),
        scratch_shapes=scratch,
        compiler_params=_cparams(1),
        name=name,
    )(*operands, q_arr, k_arr, v_arr, gate_arr, *tail)


def _attn_diff_call(rel_bias, lam_params, subln_g, qa, ka, va, sa, band, lam_init, batch, seq,
                    layer):
    n_chunks = seq // K_CHUNK
    assert n_chunks <= BIAS_SLOTS and n_chunks & (n_chunks - 1) == 0
    prefix = [pl.BlockSpec(memory_space=pltpu.SMEM),
              pl.BlockSpec((None,) + lam_params.shape[1:], lambda t: (layer, 0, 0)),
              pl.BlockSpec((None, 1, A_V_HEAD), lambda t: (layer, 0, 0))]
    return _attn_call(functools.partial(_attn_diff_kernel, lam_init=lam_init),
                      (rel_bias, lam_params, subln_g), prefix, qa, ka, va, sa, band,
                      n_pairs=A_HEADS, per_pair=True, batch=batch, seq=seq, name="attn_diff")


def _attn_gqa_call(qb, kb, vb, sb, batch, seq):
    return _attn_call(_attn_gqa_kernel, (), [], qb, kb, vb, sb, None,
                      n_pairs=B_Q_HEADS // 2, per_pair=False, batch=batch, seq=seq,
                      name="attn_gqa")


def _out_kernel(ya_ref, yb_ref, wa_ref, wb_ref, x_ref, g_ref, o_ref):
    f32 = jnp.float32
    y = (jnp.dot(ya_ref[...], wa_ref[...], preferred_element_type=f32)
         + jnp.dot(yb_ref[...], wb_ref[...], preferred_element_type=f32))
    ms = jnp.mean(y * y, axis=-1, keepdims=True)
    o_ref[...] = x_ref[...] + y * lax.rsqrt(ms + EPS) * g_ref[...]


def _out_call(ya, yb, wa, wb, xf, g_post, layer):
    m = xf.shape[0]
    row = lambda i: (i, 0)
    return pl.pallas_call(
        _out_kernel,
        grid=(m // ROW_TILE,),
        in_specs=[pl.BlockSpec((ROW_TILE, A_WIDTH), row),
                  pl.BlockSpec((ROW_TILE, B_WIDTH), row),
                  pl.BlockSpec((None, A_WIDTH, D_MODEL), lambda i: (layer, 0, 0)),
                  pl.BlockSpec((None, B_WIDTH, D_MODEL), lambda i: (layer, 0, 0)),
                  pl.BlockSpec((ROW_TILE, D_MODEL), row),
                  pl.BlockSpec((None, 1, D_MODEL), lambda i: (layer, 0, 0))],
        out_specs=pl.BlockSpec((ROW_TILE, D_MODEL), row),
        out_shape=jax.ShapeDtypeStruct((m, D_MODEL), jnp.float32),
        compiler_params=_cparams(1),
        name="out_proj",
    )(ya, yb, wa, wb, xf, g_post)


def _layer_weights(w_in_l, w_out_l, q_norm_g_l, k_norm_g_l):
    bf16 = jnp.bfloat16
    d = w_in_l.shape[0]
    per_group = B_Q_HEADS // B_KV_HEADS
    half = B_HEAD // 2
    bq = w_in_l[:, _BQ:_BK].reshape(d, B_KV_HEADS, per_group, half, 2)
    bq = bq.transpose(0, 2, 1, 4, 3).reshape(d, B_WIDTH)
    bk = w_in_l[:, _BK:_BV].reshape(d, B_KV_HEADS, half, 2)
    bk = bk.transpose(0, 1, 3, 2).reshape(d, B_KV_HEADS * B_HEAD)
    bg = w_in_l[:, _BG:].reshape(d, B_KV_HEADS, per_group, B_HEAD)
    bg = bg.transpose(0, 2, 1, 3).reshape(d, B_WIDTH)
    wn = jnp.concatenate([w_in_l[:, _AK:_AV], w_in_l[:, _AG:_BQ], bg], axis=1).astype(bf16)
    wt = jnp.concatenate([w_in_l[:, _AQ:_AK], w_in_l[:, _AV:_AG], bq, bk, w_in_l[:, _BV:_BG]],
                         axis=1).astype(bf16).T
    wa = w_out_l[:A_WIDTH].astype(bf16)
    wb = w_out_l[A_WIDTH:].reshape(B_KV_HEADS, per_group, B_HEAD, -1)
    wb = wb.transpose(1, 0, 2, 3).reshape(B_WIDTH, -1).astype(bf16)
    qg = q_norm_g_l.reshape(half, 2).T.reshape(B_HEAD, 1)
    kg = k_norm_g_l.reshape(half, 2).T.reshape(B_HEAD, 1)
    return wn, wt, wa, wb, qg, kg


def kernel(x, rel_bias, pre_norm_g, w_in, diff_lambda, diff_subln_g, q_norm_g, k_norm_g,
           w_out, post_norm_g):
    batch, seq, d_model = x.shape
    xf = x.reshape(batch * seq, d_model)
    band = _band_call(rel_bias)
    cos_t, sin_t = _rope_tables_t(seq)
    wn, wt, wa, wb, qg, kg = jax.vmap(_layer_weights)(w_in, w_out, q_norm_g, k_norm_g)
    g_pre = pre_norm_g.reshape(DEPTH, 1, d_model)
    g_post = post_norm_g.reshape(DEPTH, 1, d_model)
    g_subln = diff_subln_g.reshape(DEPTH, 1, A_V_HEAD)
    for l in range(DEPTH):
        lam_init = 0.8 - 0.6 * math.exp(-0.3 * l)
        qa, ka, va, sa, qb, kb, vb, sb = _proj_call(
            xf, g_pre, wn, wt, cos_t, sin_t, qg, kg, seq, l)
        ya = _attn_diff_call(rel_bias, diff_lambda, g_subln, qa, ka, va, sa, band, lam_init,
                             batch, seq, l)
        yb = _attn_gqa_call(qb, kb, vb, sb, batch, seq)
        xf = _out_call(ya, yb, wa, wb, xf, g_post, l)
    return xf.reshape(batch, seq, d_model)
```

```python
import functools
import math

import jax
import jax.numpy as jnp
from jax import lax
from jax.experimental import pallas as pl
from jax.experimental.pallas import tpu as pltpu

D_MODEL = 1024
DEPTH = 2
A_WIDTH = 512
B_WIDTH = 512
A_HEADS = 4
A_V_HEAD = 128
A_QK_HEAD = 64
B_HEAD = 64
B_Q_HEADS = 8
B_KV_HEADS = 2
GRID_W = 64
ROPE_THETA = 10000.0
NUM_BUCKETS = 32
MAX_DISTANCE = 128
EPS = 1e-6
LOG2E = math.log2(math.e)
Q_SCALE = (A_QK_HEAD ** -0.5) * LOG2E

LANES = 128
SUBLANES = 8
ROW_TILE = 1024
Q_TILE = 256
K_CHUNK = 256
BAND_CHUNKS = 3
BIAS_SLOTS = 16
BIAS_TERMS = 3
VMEM_LIMIT = 48 * 1024 * 1024

_AQ, _AK, _AV, _AG, _BQ, _BK, _BV, _BG = 0, 512, 1024, 1536, 2048, 2560, 2688, 2816
_T_AQ, _T_AV, _T_BQ, _T_BK, _T_BV, _T_END = 0, 512, 1024, 1536, 1664, 1792
_N_AK, _N_AG, _N_BG, _N_END = 0, 512, 1024, 1536


def _t5_bucket(rel):
    nb = NUM_BUCKETS // 2
    max_exact = nb // 2
    n = jnp.abs(rel)
    large = max_exact + (jnp.log(jnp.maximum(n, 1).astype(jnp.float32) / max_exact)
                         / math.log(MAX_DISTANCE / max_exact) * (nb - max_exact)).astype(jnp.int32)
    large = jnp.minimum(large, nb - 1)
    return jnp.where(rel > 0, nb, 0) + jnp.where(n < max_exact, n, large)


def _rope_tables_t(n):
    rows = n // GRID_W
    axis_dim = B_HEAD // 2
    inv = ROPE_THETA ** (-jnp.arange(0, axis_dim, 2, dtype=jnp.float32) / axis_dim)
    row_ang = (jnp.arange(rows, dtype=jnp.float32)[:, None] * inv).T
    col_ang = (jnp.arange(GRID_W, dtype=jnp.float32)[:, None] * inv).T

    def expand(fn):
        by_row = jnp.repeat(fn(row_ang), GRID_W, axis=1)
        by_col = jnp.tile(fn(col_ang), (1, rows))
        return jnp.concatenate([by_row, by_col], axis=0)

    return expand(jnp.cos), expand(jnp.sin)


def _cparams(n_axes):
    return pltpu.CompilerParams(dimension_semantics=("arbitrary",) * n_axes,
                                vmem_limit_bytes=VMEM_LIMIT)


def _bias_slot_mask(slot_index, chunk):
    in_terms = jnp.right_shift(jnp.bitwise_and(slot_index, A_QK_HEAD - 1), 4) < BIAS_TERMS
    return jnp.logical_and(in_terms, jnp.bitwise_and(slot_index, BIAS_SLOTS - 1) == chunk)


BAND_TABLE = 2 * Q_TILE


def _band_kernel(rb_ref, idx_ref, out_ref):
    h = pl.program_id(0)
    for d in range(BAND_CHUNKS):
        idx = jnp.broadcast_to(idx_ref[d], (SUBLANES, BAND_TABLE))
        vals = jnp.zeros(idx.shape, jnp.float32)
        for b in range(NUM_BUCKETS):
            vals = jnp.where(idx == b, rb_ref[b, h] * LOG2E, vals)

        for j0 in range(0, K_CHUNK, SUBLANES):
            rolled = pltpu.roll(vals, (j0 + Q_TILE + 1) % BAND_TABLE, 1, stride=1, stride_axis=0)
            out_ref[0, d, j0:j0 + SUBLANES, :] = rolled[:, :Q_TILE]


def _band_call(rel_bias):
    d = jnp.arange(BAND_CHUNKS, dtype=jnp.int32)[:, None, None]
    c = jnp.arange(BAND_TABLE, dtype=jnp.int32)[None, None, :]
    idx = _t5_bucket((d - 1) * K_CHUNK + (Q_TILE - 1) - c)
    idx = jnp.take(jnp.arange(NUM_BUCKETS, dtype=jnp.int32), idx)
    return pl.pallas_call(
        _band_kernel,
        grid=(A_HEADS,),
        in_specs=[pl.BlockSpec(memory_space=pltpu.SMEM),
                  pl.BlockSpec((BAND_CHUNKS, 1, BAND_TABLE), lambda h: (0, 0, 0))],
        out_specs=pl.BlockSpec((1, BAND_CHUNKS, K_CHUNK, Q_TILE), lambda h: (h, 0, 0, 0)),
        out_shape=jax.ShapeDtypeStruct((A_HEADS, BAND_CHUNKS, K_CHUNK, Q_TILE), jnp.float32),
        compiler_params=_cparams(1),
        name="bias_band",
    )(rel_bias, idx)


def _proj_kernel(x_ref, g_ref, wn_ref, wt_ref, cos_ref, sin_ref, qg_ref, kg_ref,
                 qa_ref, ka_ref, va_ref, sa_ref, qb_ref, kb_ref, vb_ref, sb_ref,
                 *, tiles_per_seq):
    f32, bf16 = jnp.float32, jnp.bfloat16
    x = x_ref[...]
    ms = jnp.mean(x * x, axis=-1, keepdims=True)
    h = (x * lax.rsqrt(ms + EPS) * g_ref[...]).astype(bf16)

    def nat(lo, hi):
        return jnp.dot(h, wn_ref[:, lo:hi], preferred_element_type=f32)

    def silu(v):
        return v * jax.nn.sigmoid(v)

    pt = lax.dot_general(wt_ref[...], h, (((1,), (1,)), ((), ())), preferred_element_type=f32)
    qa_ref[...] = (pt[_T_AQ:_T_AV] * Q_SCALE).astype(bf16)
    va_ref[...] = pt[_T_AV:_T_BQ].astype(bf16)
    vb_ref[...] = pt[_T_BV:_T_END].astype(bf16)

    cos = cos_ref[...]
    sin = sin_ref[...]
    half = B_HEAD // 2

    def norm_rope(xt, g):
        msq = jnp.mean(xt * xt, axis=0, keepdims=True)
        y = xt * lax.rsqrt(msq + EPS) * g
        e, o = y[:half], y[half:]
        return jnp.concatenate([e * cos - o * sin, e * sin + o * cos], axis=0)

    qg = qg_ref[...]
    qn = jnp.concatenate([norm_rope(pt[_T_BQ + i * B_HEAD:_T_BQ + (i + 1) * B_HEAD], qg)
                          for i in range(B_Q_HEADS)], axis=0) * Q_SCALE
    qb_ref[...] = qn.astype(bf16)
    kg = kg_ref[...]
    kn = jnp.concatenate([norm_rope(pt[_T_BK + i * B_HEAD:_T_BK + (i + 1) * B_HEAD], kg)
                          for i in range(B_KV_HEADS)], axis=0)
    kb_ref[...] = kn.T.astype(bf16)

    sa_ref[...] = silu(nat(_N_AG, _N_BG)).astype(bf16)
    sb_ref[...] = silu(nat(_N_BG, _N_END)).astype(bf16)

    ak = nat(_N_AK, _N_AG)
    lane = lax.broadcasted_iota(jnp.int32, (ROW_TILE, LANES), 1)
    row = lax.broadcasted_iota(jnp.int32, (ROW_TILE, LANES), 0)
    chunk = ((pl.program_id(0) % tiles_per_seq) * (ROW_TILE // K_CHUNK)
             + jnp.right_shift(row, K_CHUNK.bit_length() - 1))
    hot = jnp.where(_bias_slot_mask(lane, chunk), 1.0, 0.0)
    low = lane < A_QK_HEAD
    for hd in range(A_HEADS):
        akh = ak[:, hd * LANES:(hd + 1) * LANES]
        ka_ref[:, 2 * hd * LANES:(2 * hd + 1) * LANES] = jnp.where(low, akh, hot).astype(bf16)
        ka_ref[:, (2 * hd + 1) * LANES:(2 * hd + 2) * LANES] = jnp.where(low, hot, akh).astype(bf16)


def _proj_call(xf, g_pre, wn, wt, cos_t, sin_t, qg, kg, seq, layer):
    m = xf.shape[0]
    tiles_per_seq = seq // ROW_TILE
    bf16 = jnp.bfloat16
    row = lambda i: (i, 0)
    col = lambda i: (0, i)
    of_layer = lambda i: (layer, 0, 0)
    pos = lambda i: (0, i % tiles_per_seq)
    return pl.pallas_call(
        functools.partial(_proj_kernel, tiles_per_seq=tiles_per_seq),
        grid=(m // ROW_TILE,),
        in_specs=[pl.BlockSpec((ROW_TILE, D_MODEL), row),
                  pl.BlockSpec((None, 1, D_MODEL), of_layer),
                  pl.BlockSpec((None,) + wn.shape[1:], of_layer),
                  pl.BlockSpec((None,) + wt.shape[1:], of_layer),
                  pl.BlockSpec((B_HEAD // 2, ROW_TILE), pos),
                  pl.BlockSpec((B_HEAD // 2, ROW_TILE), pos),
                  pl.BlockSpec((None, B_HEAD, 1), of_layer),
                  pl.BlockSpec((None, B_HEAD, 1), of_layer)],
        out_specs=[pl.BlockSpec((A_WIDTH, ROW_TILE), col),
                   pl.BlockSpec((ROW_TILE, 2 * A_WIDTH), row),
                   pl.BlockSpec((A_WIDTH, ROW_TILE), col),
                   pl.BlockSpec((ROW_TILE, A_WIDTH), row),
                   pl.BlockSpec((B_WIDTH, ROW_TILE), col),
                   pl.BlockSpec((ROW_TILE, B_KV_HEADS * B_HEAD), row),
                   pl.BlockSpec((B_KV_HEADS * B_HEAD, ROW_TILE), col),
                   pl.BlockSpec((ROW_TILE, B_WIDTH), row)],
        out_shape=[jax.ShapeDtypeStruct((A_WIDTH, m), bf16),
                   jax.ShapeDtypeStruct((m, 2 * A_WIDTH), bf16),
                   jax.ShapeDtypeStruct((A_WIDTH, m), bf16),
                   jax.ShapeDtypeStruct((m, A_WIDTH), bf16),
                   jax.ShapeDtypeStruct((B_WIDTH, m), bf16),
                   jax.ShapeDtypeStruct((m, B_KV_HEADS * B_HEAD), bf16),
                   jax.ShapeDtypeStruct((B_KV_HEADS * B_HEAD, m), bf16),
                   jax.ShapeDtypeStruct((m, B_WIDTH), bf16)],
        compiler_params=_cparams(1),
        name="in_proj",
    )(xf, g_pre, wn, wt, cos_t, sin_t, qg, kg)


def _fold_rows(x, op):
    parts = [x[g * SUBLANES:(g + 1) * SUBLANES] for g in range(x.shape[0] // SUBLANES)]
    while len(parts) > 1:
        parts = [op(parts[i], parts[i + 1]) for i in range(0, len(parts), 2)]
    return parts[0]


def _pipeline_tiles(n_tiles, nq):
    t = pl.program_id(0)
    tile1 = jnp.minimum(t, n_tiles - 1)
    return t, tile1, tile1 % nq


def _fill_first_steps(t, s_odd, m_odd, acc_ref, sum_ref):
    @pl.when(t == 0)
    def _():
        s_odd[...] = jnp.zeros(s_odd.shape, jnp.float32)
        m_odd[...] = jnp.zeros(m_odd.shape, jnp.float32)
        acc_ref[...] = jnp.zeros(acc_ref.shape, jnp.float32)
        sum_ref[...] = jnp.ones(sum_ref.shape, jnp.float32)


def _by_parity(t, step, s_even, m_even, s_odd, m_odd):
    @pl.when(t % 2 == 0)
    def _():
        step(s_even, m_even, s_odd, m_odd)

    @pl.when(t % 2 == 1)
    def _():
        step(s_odd, m_odd, s_even, m_even)


def _attn_gqa_kernel(q_ref, k_ref, v_ref, gate_ref, o_ref,
                     w_ref, s_even, s_odd, m_even, m_odd, acc_ref, sum_ref,
                     *, n_chunks, nq, n_tiles):
    f32, bf16 = jnp.float32, jnp.bfloat16
    t, _, _ = _pipeline_tiles(n_tiles, nq)
    _fill_first_steps(t, s_odd, m_odd, acc_ref, sum_ref)

    def step(s1_ref, m1_ref, s2_ref, m2_ref):
        top = lax.broadcasted_iota(jnp.int32, (LANES, Q_TILE), 0) < B_HEAD
        qt = q_ref[...].astype(f32)
        w_ref[0] = jnp.where(top, qt, 0.0).astype(bf16)
        w_ref[1] = jnp.where(top, 0.0, qt).astype(bf16)

        ot = jnp.concatenate([acc_ref[st] / jnp.sum(sum_ref[st], axis=0, keepdims=True)
                              for st in range(2)], axis=0)
        o_ref[...] = (ot.T * gate_ref[...].astype(f32)).astype(o_ref.dtype)

        m_row = [jnp.max(m2_ref[st], axis=0, keepdims=True) for st in range(2)]

        m_acc = [None, None]
        l_acc = [None, None]
        o_acc = [None, None]
        for c in range(n_chunks):
            keys = slice(c * K_CHUNK, (c + 1) * K_CHUNK)
            for st in range(2):
                s = jnp.dot(k_ref[keys, :], w_ref[st], preferred_element_type=f32)
                s1_ref[c, st] = s
                cm = _fold_rows(s, jnp.maximum)
                m_acc[st] = cm if c == 0 else jnp.maximum(m_acc[st], cm)
            for st in range(2):
                p = jnp.exp2(s2_ref[c, st] - m_row[st])
                ps = _fold_rows(p, jnp.add)
                l_acc[st] = ps if c == 0 else l_acc[st] + ps
                part = jnp.dot(v_ref[st * B_HEAD:(st + 1) * B_HEAD, keys], p.astype(bf16),
                               preferred_element_type=f32)
                o_acc[st] = part if c == 0 else o_acc[st] + part
        for st in range(2):
            m1_ref[st] = m_acc[st]
            acc_ref[st] = o_acc[st]
            sum_ref[st] = l_acc[st]

    _by_parity(t, step, s_even, m_even, s_odd, m_odd)


def _attn_diff_kernel(rb_ref, lamp_ref, g_ref, q_ref, k_ref, v_ref, gate_ref, band_ref, o_ref,
                      w_ref, s_even, s_odd, m_even, m_odd, acc_ref, sum_ref,
                      *, lam_init, n_chunks, nq, n_tiles):
    f32, bf16 = jnp.float32, jnp.bfloat16
    t, tile1, qi1 = _pipeline_tiles(n_tiles, nq)
    head1 = (tile1 // nq) % A_HEADS
    _fill_first_steps(t, s_odd, m_odd, acc_ref, sum_ref)

    def step(s1_ref, m1_ref, s2_ref, m2_ref):
        row = lax.broadcasted_iota(jnp.int32, (LANES, Q_TILE), 0)
        top = row < A_QK_HEAD
        slot = jnp.bitwise_and(row, BIAS_SLOTS - 1)
        term = jnp.right_shift(jnp.bitwise_and(row, A_QK_HEAD - 1), 4)
        left = rb_ref[NUM_BUCKETS // 2 - 1, head1] * LOG2E
        right = rb_ref[NUM_BUCKETS - 1, head1] * LOG2E
        const = jnp.where(jnp.abs(slot - qi1) <= 1, 0.0, jnp.where(slot < qi1, left, right))
        t0 = const.astype(bf16).astype(f32)
        t1 = (const - t0).astype(bf16).astype(f32)
        t2 = (const - t0 - t1).astype(bf16).astype(f32)
        bias_rows = jnp.where(term == 0, t0, jnp.where(term == 1, t1,
                                                      jnp.where(term == 2, t2, 0.0)))
        qt = q_ref[...].astype(f32)
        w_ref[0] = jnp.where(top, qt, bias_rows).astype(bf16)
        w_ref[1] = jnp.where(top, bias_rows, qt).astype(bf16)

        l_row = [jnp.sum(sum_ref[st], axis=0, keepdims=True) for st in range(2)]
        lp = lamp_ref[...]
        lam = (jnp.exp(jnp.sum(lp[0:1] * lp[1:2], axis=-1, keepdims=True))
               - jnp.exp(jnp.sum(lp[2:3] * lp[3:4], axis=-1, keepdims=True)) + lam_init)
        o = (acc_ref[0] / l_row[0] - lam * (acc_ref[1] / l_row[1])).T
        ms = jnp.mean(o * o, axis=-1, keepdims=True)
        o = (o * lax.rsqrt(ms + EPS) * g_ref[...]) * (1.0 - lam_init)
        o_ref[...] = (o * gate_ref[...].astype(f32)).astype(o_ref.dtype)

        m_row = [jnp.max(m2_ref[st], axis=0, keepdims=True) for st in range(2)]

        m_acc = [None, None]
        l_acc = [None, None]
        o_acc = [None, None]
        for i in range(n_chunks):
            c1 = qi1 - 1 + i
            j1 = jnp.bitwise_and(c1, n_chunks - 1)
            key0 = pl.multiple_of(j1 * K_CHUNK, K_CHUNK)
            if i < BAND_CHUNKS:
                in_range = jnp.logical_and(c1 >= 0, c1 < n_chunks)
                band = jnp.where(in_range, band_ref[0, i], 0.0)
            for st in range(2):
                s = jnp.dot(k_ref[pl.ds(key0, K_CHUNK), st * LANES:(st + 1) * LANES], w_ref[st],
                            preferred_element_type=f32)
                if i < BAND_CHUNKS:
                    s = s + band
                s1_ref[j1, st] = s
                cm = _fold_rows(s, jnp.maximum)
                m_acc[st] = cm if i == 0 else jnp.maximum(m_acc[st], cm)
            for st in range(2):
                p = jnp.exp2(s2_ref[i, st] - m_row[st])
                ps = _fold_rows(p, jnp.add)
                l_acc[st] = ps if i == 0 else l_acc[st] + ps
                part = jnp.dot(v_ref[:, i * K_CHUNK:(i + 1) * K_CHUNK], p.astype(bf16),
                               preferred_element_type=f32)
                o_acc[st] = part if i == 0 else o_acc[st] + part
        for st in range(2):
            m1_ref[st] = m_acc[st]
            acc_ref[st] = o_acc[st]
            sum_ref[st] = l_acc[st]

    _by_parity(t, step, s_even, m_even, s_odd, m_odd)


def _attn_call(kernel_fn, operands, prefix_specs, q_arr, k_arr, v_arr, gate_arr, band_arr,
               *, n_pairs, per_pair, batch, seq, name):
    m = gate_arr.shape[0]
    nq = seq // Q_TILE
    n_chunks = seq // K_CHUNK
    n_tiles = batch * n_pairs * nq
    k_lanes = k_arr.shape[1] // (n_pairs if per_pair else 1)
    v_rows = LANES if per_pair else B_HEAD

    def decode(tile):
        return tile // (n_pairs * nq), (tile // nq) % n_pairs, tile % nq

    def stage1(t):
        return decode(jnp.minimum(t, n_tiles - 1))

    def stage2(t):
        return decode(jnp.clip(t - 1, 0, n_tiles - 1))

    def stage3(t):
        return decode(jnp.maximum(t - 2, 0))

    def q_map(t):
        b, p, i = stage1(t)
        return p, b * nq + i

    def k_map(t):
        b, p, _ = stage1(t)
        return b, p if per_pair else 0

    def v_map(t):
        b, p, _ = stage2(t)
        return p if per_pair else 0, b

    def out_map(t):
        b, p, i = stage3(t)
        return b * nq + i, p

    in_specs = list(prefix_specs) + [
        pl.BlockSpec((LANES, Q_TILE), q_map),
        pl.BlockSpec((seq, k_lanes), k_map),
        pl.BlockSpec((LANES, seq), v_map),
        pl.BlockSpec((Q_TILE, LANES), out_map)]
    tail = ()
    if band_arr is not None:
        in_specs.append(pl.BlockSpec((1, BAND_CHUNKS, K_CHUNK, Q_TILE),
                                     lambda t: (stage1(t)[1], 0, 0, 0)))
        tail = (band_arr,)
    scores = pltpu.VMEM((n_chunks, 2, K_CHUNK, Q_TILE), jnp.float32)
    per_key_group = pltpu.VMEM((2, SUBLANES, Q_TILE), jnp.float32)
    scratch = [pltpu.VMEM((2, LANES, Q_TILE), jnp.bfloat16),
               scores, scores,
               per_key_group, per_key_group,
               pltpu.VMEM((2, v_rows, Q_TILE), jnp.float32),
               per_key_group]
    return pl.pallas_call(
        functools.partial(kernel_fn, n_chunks=n_chunks, nq=nq, n_tiles=n_tiles),
        grid=(n_tiles + 2,),
        in_specs=in_specs,
        out_specs=pl.BlockSpec((Q_TILE, LANES), out_map),
        out_shape=jax.ShapeDtypeStruct((m, n_pairs * LANES), jnp.bfloat16),
        scratch_shapes=scratch,
        compiler_params=_cparams(1),
        name=name,
    )(*operands, q_arr, k_arr, v_arr, gate_arr, *tail)


def _attn_diff_call(rel_bias, lam_params, subln_g, qa, ka, va, sa, band, lam_init, batch, seq):
    n_chunks = seq // K_CHUNK
    assert n_chunks <= BIAS_SLOTS and n_chunks & (n_chunks - 1) == 0
    prefix = [pl.BlockSpec(memory_space=pltpu.SMEM),
              pl.BlockSpec((4, A_QK_HEAD), lambda t: (0, 0)),
              pl.BlockSpec((1, A_V_HEAD), lambda t: (0, 0))]
    return _attn_call(functools.partial(_attn_diff_kernel, lam_init=lam_init),
                      (rel_bias, lam_params, subln_g), prefix, qa, ka, va, sa, band,
                      n_pairs=A_HEADS, per_pair=True, batch=batch, seq=seq, name="attn_diff")


def _attn_gqa_call(qb, kb, vb, sb, batch, seq):
    return _attn_call(_attn_gqa_kernel, (), [], qb, kb, vb, sb, None,
                      n_pairs=B_Q_HEADS // 2, per_pair=False, batch=batch, seq=seq,
                      name="attn_gqa")


def _out_kernel(ya_ref, yb_ref, wa_ref, wb_ref, x_ref, g_ref, o_ref):
    f32 = jnp.float32
    y = (jnp.dot(ya_ref[...], wa_ref[...], preferred_element_type=f32)
         + jnp.dot(yb_ref[...], wb_ref[...], preferred_element_type=f32))
    ms = jnp.mean(y * y, axis=-1, keepdims=True)
    o_ref[...] = x_ref[...] + y * lax.rsqrt(ms + EPS) * g_ref[...]


def _out_call(ya, yb, wa, wb, xf, g_post, layer):
    m = xf.shape[0]
    row = lambda i: (i, 0)
    return pl.pallas_call(
        _out_kernel,
        grid=(m // ROW_TILE,),
        in_specs=[pl.BlockSpec((ROW_TILE, A_WIDTH), row),
                  pl.BlockSpec((ROW_TILE, B_WIDTH), row),
                  pl.BlockSpec((None, A_WIDTH, D_MODEL), lambda i: (layer, 0, 0)),
                  pl.BlockSpec((None, B_WIDTH, D_MODEL), lambda i: (layer, 0, 0)),
                  pl.BlockSpec((ROW_TILE, D_MODEL), row),
                  pl.BlockSpec((None, 1, D_MODEL), lambda i: (layer, 0, 0))],
        out_specs=pl.BlockSpec((ROW_TILE, D_MODEL), row),
        out_shape=jax.ShapeDtypeStruct((m, D_MODEL), jnp.float32),
        compiler_params=_cparams(1),
        name="out_proj",
    )(ya, yb, wa, wb, xf, g_post)


def _layer_weights(w_in_l, w_out_l, q_norm_g_l, k_norm_g_l):
    bf16 = jnp.bfloat16
    d = w_in_l.shape[0]
    per_group = B_Q_HEADS // B_KV_HEADS
    half = B_HEAD // 2
    bq = w_in_l[:, _BQ:_BK].reshape(d, B_KV_HEADS, per_group, half, 2)
    bq = bq.transpose(0, 2, 1, 4, 3).reshape(d, B_WIDTH)
    bk = w_in_l[:, _BK:_BV].reshape(d, B_KV_HEADS, half, 2)
    bk = bk.transpose(0, 1, 3, 2).reshape(d, B_KV_HEADS * B_HEAD)
    bg = w_in_l[:, _BG:].reshape(d, B_KV_HEADS, per_group, B_HEAD)
    bg = bg.transpose(0, 2, 1, 3).reshape(d, B_WIDTH)
    wn = jnp.concatenate([w_in_l[:, _AK:_AV], w_in_l[:, _AG:_BQ], bg], axis=1).astype(bf16)
    wt = jnp.concatenate([w_in_l[:, _AQ:_AK], w_in_l[:, _AV:_AG], bq, bk, w_in_l[:, _BV:_BG]],
                         axis=1).astype(bf16).T
    wa = w_out_l[:A_WIDTH].astype(bf16)
    wb = w_out_l[A_WIDTH:].reshape(B_KV_HEADS, per_group, B_HEAD, -1)
    wb = wb.transpose(1, 0, 2, 3).reshape(B_WIDTH, -1).astype(bf16)
    qg = q_norm_g_l.reshape(half, 2).T.reshape(B_HEAD, 1)
    kg = k_norm_g_l.reshape(half, 2).T.reshape(B_HEAD, 1)
    return wn, wt, wa, wb, qg, kg


def kernel(x, rel_bias, pre_norm_g, w_in, diff_lambda, diff_subln_g, q_norm_g, k_norm_g,
           w_out, post_norm_g):
    batch, seq, d_model = x.shape
    xf = x.reshape(batch * seq, d_model)
    band = _band_call(rel_bias)
    cos_t, sin_t = _rope_tables_t(seq)
    wn, wt, wa, wb, qg, kg = jax.vmap(_layer_weights)(w_in, w_out, q_norm_g, k_norm_g)
    g_pre = pre_norm_g.reshape(DEPTH, 1, d_model)
    g_post = post_norm_g.reshape(DEPTH, 1, d_model)
    for l in range(DEPTH):
        lam_init = 0.8 - 0.6 * math.exp(-0.3 * l)
        qa, ka, va, sa, qb, kb, vb, sb = _proj_call(
            xf, g_pre, wn, wt, cos_t, sin_t, qg, kg, seq, l)
        ya = _attn_diff_call(rel_bias, diff_lambda[l], diff_subln_g[l].reshape(1, A_V_HEAD),
                             qa, ka, va, sa, band, lam_init, batch, seq)
        yb = _attn_gqa_call(qb, kb, vb, sb, batch, seq)
        xf = _out_call(ya, yb, wa, wb, xf, g_post, l)
    return xf.reshape(batch, seq, d_model)
```

```python
import functools
import math

import jax
import jax.numpy as jnp
from jax import lax
from jax.experimental import pallas as pl
from jax.experimental.pallas import tpu as pltpu

D_MODEL = 1024
DEPTH = 2
A_WIDTH = 512
B_WIDTH = 512
A_HEADS = 4
A_V_HEAD = 128
A_QK_HEAD = 64
B_HEAD = 64
B_Q_HEADS = 8
B_KV_HEADS = 2
GRID_W = 64
ROPE_THETA = 10000.0
NUM_BUCKETS = 32
MAX_DISTANCE = 128
EPS = 1e-6
LOG2E = math.log2(math.e)
Q_SCALE = (A_QK_HEAD ** -0.5) * LOG2E

LANES = 128
SUBLANES = 8
ROW_TILE = 1024
Q_TILE = 256
K_CHUNK = 256
BAND_CHUNKS = 3
BIAS_SLOTS = 16
BIAS_TERMS = 3
VMEM_LIMIT = 62 * 1024 * 1024

_AQ, _AK, _AV, _AG, _BQ, _BK, _BV, _BG = 0, 512, 1024, 1536, 2048, 2560, 2688, 2816
_T_AQ, _T_AV, _T_BQ, _T_BK, _T_BV, _T_END = 0, 512, 1024, 1536, 1664, 1792
_N_AK, _N_AG, _N_BG, _N_END = 0, 512, 1024, 1536


def _t5_bucket(rel):
    nb = NUM_BUCKETS // 2
    max_exact = nb // 2
    n = jnp.abs(rel)
    large = max_exact + (jnp.log(jnp.maximum(n, 1).astype(jnp.float32) / max_exact)
                         / math.log(MAX_DISTANCE / max_exact) * (nb - max_exact)).astype(jnp.int32)
    large = jnp.minimum(large, nb - 1)
    return jnp.where(rel > 0, nb, 0) + jnp.where(n < max_exact, n, large)


def _rope_tables_t(n):
    rows = n // GRID_W
    axis_dim = B_HEAD // 2
    inv = ROPE_THETA ** (-jnp.arange(0, axis_dim, 2, dtype=jnp.float32) / axis_dim)
    row_ang = (jnp.arange(rows, dtype=jnp.float32)[:, None] * inv).T
    col_ang = (jnp.arange(GRID_W, dtype=jnp.float32)[:, None] * inv).T

    def expand(fn):
        by_row = jnp.repeat(fn(row_ang), GRID_W, axis=1)
        by_col = jnp.tile(fn(col_ang), (1, rows))
        return jnp.concatenate([by_row, by_col], axis=0)

    return expand(jnp.cos), expand(jnp.sin)


def _cparams(n_axes):
    return pltpu.CompilerParams(dimension_semantics=("arbitrary",) * n_axes,
                                vmem_limit_bytes=VMEM_LIMIT)


def _bias_slot_mask(slot_index, chunk):
    in_terms = jnp.right_shift(jnp.bitwise_and(slot_index, A_QK_HEAD - 1), 4) < BIAS_TERMS
    return jnp.logical_and(in_terms, jnp.bitwise_and(slot_index, BIAS_SLOTS - 1) == chunk)


BAND_TABLE = 2 * Q_TILE


def _band_kernel(rb_ref, idx_ref, out_ref):
    h = pl.program_id(0)
    for d in range(BAND_CHUNKS):
        idx = jnp.broadcast_to(idx_ref[d], (SUBLANES, BAND_TABLE))
        vals = jnp.zeros(idx.shape, jnp.float32)
        for b in range(NUM_BUCKETS):
            vals = jnp.where(idx == b, rb_ref[b, h] * LOG2E, vals)

        for j0 in range(0, K_CHUNK, SUBLANES):
            rolled = pltpu.roll(vals, (j0 + Q_TILE + 1) % BAND_TABLE, 1, stride=1, stride_axis=0)
            out_ref[0, d, j0:j0 + SUBLANES, :] = rolled[:, :Q_TILE]


def _band_call(rel_bias):
    d = jnp.arange(BAND_CHUNKS, dtype=jnp.int32)[:, None, None]
    c = jnp.arange(BAND_TABLE, dtype=jnp.int32)[None, None, :]
    idx = _t5_bucket((d - 1) * K_CHUNK + (Q_TILE - 1) - c)
    idx = jnp.take(jnp.arange(NUM_BUCKETS, dtype=jnp.int32), idx)
    return pl.pallas_call(
        _band_kernel,
        grid=(A_HEADS,),
        in_specs=[pl.BlockSpec(memory_space=pltpu.SMEM),
                  pl.BlockSpec((BAND_CHUNKS, 1, BAND_TABLE), lambda h: (0, 0, 0))],
        out_specs=pl.BlockSpec((1, BAND_CHUNKS, K_CHUNK, Q_TILE), lambda h: (h, 0, 0, 0)),
        out_shape=jax.ShapeDtypeStruct((A_HEADS, BAND_CHUNKS, K_CHUNK, Q_TILE), jnp.float32),
        compiler_params=_cparams(1),
        name="bias_band",
    )(rel_bias, idx)


def _proj_kernel(x_ref, g_ref, wn_ref, wt_ref, cos_ref, sin_ref, qg_ref, kg_ref,
                 qa_ref, ka_ref, va_ref, sa_ref, qb_ref, kb_ref, vb_ref, sb_ref,
                 *, tiles_per_seq):
    f32, bf16 = jnp.float32, jnp.bfloat16
    x = x_ref[...]
    ms = jnp.mean(x * x, axis=-1, keepdims=True)
    h = (x * lax.rsqrt(ms + EPS) * g_ref[...]).astype(bf16)

    def nat(lo, hi):
        return jnp.dot(h, wn_ref[:, lo:hi], preferred_element_type=f32)

    def silu(v):
        return v * jax.nn.sigmoid(v)

    pt = lax.dot_general(wt_ref[...], h, (((1,), (1,)), ((), ())), preferred_element_type=f32)
    qa_ref[...] = (pt[_T_AQ:_T_AV] * Q_SCALE).astype(bf16)
    va_ref[...] = pt[_T_AV:_T_BQ].astype(bf16)
    vb_ref[...] = pt[_T_BV:_T_END].astype(bf16)

    cos = cos_ref[...]
    sin = sin_ref[...]
    half = B_HEAD // 2

    def norm_rope(xt, g):
        msq = jnp.mean(xt * xt, axis=0, keepdims=True)
        y = xt * lax.rsqrt(msq + EPS) * g
        e, o = y[:half], y[half:]
        return jnp.concatenate([e * cos - o * sin, e * sin + o * cos], axis=0)

    qg = qg_ref[...]
    qn = jnp.concatenate([norm_rope(pt[_T_BQ + i * B_HEAD:_T_BQ + (i + 1) * B_HEAD], qg)
                          for i in range(B_Q_HEADS)], axis=0) * Q_SCALE
    qb_ref[...] = qn.astype(bf16)
    kg = kg_ref[...]
    kn = jnp.concatenate([norm_rope(pt[_T_BK + i * B_HEAD:_T_BK + (i + 1) * B_HEAD], kg)
                          for i in range(B_KV_HEADS)], axis=0)
    kb_ref[...] = kn.T.astype(bf16)

    sa_ref[...] = silu(nat(_N_AG, _N_BG)).astype(bf16)
    sb_ref[...] = silu(nat(_N_BG, _N_END)).astype(bf16)

    ak = nat(_N_AK, _N_AG)
    lane = lax.broadcasted_iota(jnp.int32, (ROW_TILE, LANES), 1)
    row = lax.broadcasted_iota(jnp.int32, (ROW_TILE, LANES), 0)
    chunk = ((pl.program_id(0) % tiles_per_seq) * (ROW_TILE // K_CHUNK)
             + jnp.right_shift(row, K_CHUNK.bit_length() - 1))
    hot = jnp.where(_bias_slot_mask(lane, chunk), 1.0, 0.0)
    low = lane < A_QK_HEAD
    for hd in range(A_HEADS):
        akh = ak[:, hd * LANES:(hd + 1) * LANES]
        ka_ref[:, 2 * hd * LANES:(2 * hd + 1) * LANES] = jnp.where(low, akh, hot).astype(bf16)
        ka_ref[:, (2 * hd + 1) * LANES:(2 * hd + 2) * LANES] = jnp.where(low, hot, akh).astype(bf16)


def _proj_call(xf, g_pre, wn, wt, cos_t, sin_t, qg, kg, seq, layer):
    m = xf.shape[0]
    tiles_per_seq = seq // ROW_TILE
    bf16 = jnp.bfloat16
    row = lambda i: (i, 0)
    col = lambda i: (0, i)
    const = lambda i: (0, 0)
    of_layer = lambda i: (layer, 0, 0)
    pos = lambda i: (0, i % tiles_per_seq)
    return pl.pallas_call(
        functools.partial(_proj_kernel, tiles_per_seq=tiles_per_seq),
        grid=(m // ROW_TILE,),
        in_specs=[pl.BlockSpec((ROW_TILE, D_MODEL), row),
                  pl.BlockSpec((1, D_MODEL), const),
                  pl.BlockSpec((None,) + wn.shape[1:], of_layer),
                  pl.BlockSpec((None,) + wt.shape[1:], of_layer),
                  pl.BlockSpec((B_HEAD // 2, ROW_TILE), pos),
                  pl.BlockSpec((B_HEAD // 2, ROW_TILE), pos),
                  pl.BlockSpec((None, B_HEAD, 1), of_layer),
                  pl.BlockSpec((None, B_HEAD, 1), of_layer)],
        out_specs=[pl.BlockSpec((A_WIDTH, ROW_TILE), col),
                   pl.BlockSpec((ROW_TILE, 2 * A_WIDTH), row),
                   pl.BlockSpec((A_WIDTH, ROW_TILE), col),
                   pl.BlockSpec((ROW_TILE, A_WIDTH), row),
                   pl.BlockSpec((B_WIDTH, ROW_TILE), col),
                   pl.BlockSpec((ROW_TILE, B_KV_HEADS * B_HEAD), row),
                   pl.BlockSpec((B_KV_HEADS * B_HEAD, ROW_TILE), col),
                   pl.BlockSpec((ROW_TILE, B_WIDTH), row)],
        out_shape=[jax.ShapeDtypeStruct((A_WIDTH, m), bf16),
                   jax.ShapeDtypeStruct((m, 2 * A_WIDTH), bf16),
                   jax.ShapeDtypeStruct((A_WIDTH, m), bf16),
                   jax.ShapeDtypeStruct((m, A_WIDTH), bf16),
                   jax.ShapeDtypeStruct((B_WIDTH, m), bf16),
                   jax.ShapeDtypeStruct((m, B_KV_HEADS * B_HEAD), bf16),
                   jax.ShapeDtypeStruct((B_KV_HEADS * B_HEAD, m), bf16),
                   jax.ShapeDtypeStruct((m, B_WIDTH), bf16)],
        compiler_params=_cparams(1),
        name="in_proj",
    )(xf, g_pre, wn, wt, cos_t, sin_t, qg, kg)


def _fold_rows(x, op):
    parts = [x[g * SUBLANES:(g + 1) * SUBLANES] for g in range(x.shape[0] // SUBLANES)]
    while len(parts) > 1:
        parts = [op(parts[i], parts[i + 1]) for i in range(0, len(parts), 2)]
    return parts[0]


def _pipeline_tiles(n_tiles, nq):
    t = pl.program_id(0)
    tile1 = jnp.minimum(t, n_tiles - 1)
    return t, tile1, tile1 % nq


def _fill_first_steps(t, s_odd, m_odd, acc_ref, sum_ref):
    @pl.when(t == 0)
    def _():
        s_odd[...] = jnp.zeros(s_odd.shape, jnp.float32)
        m_odd[...] = jnp.zeros(m_odd.shape, jnp.float32)
        acc_ref[...] = jnp.zeros(acc_ref.shape, jnp.float32)
        sum_ref[...] = jnp.ones(sum_ref.shape, jnp.float32)


def _by_parity(t, step, s_even, m_even, s_odd, m_odd):
    @pl.when(t % 2 == 0)
    def _():
        step(s_even, m_even, s_odd, m_odd)

    @pl.when(t % 2 == 1)
    def _():
        step(s_odd, m_odd, s_even, m_even)


def _attn_gqa_kernel(q_ref, k_ref, v_ref, gate_ref, o_ref,
                     w_ref, s_even, s_odd, m_even, m_odd, acc_ref, sum_ref,
                     *, n_chunks, nq, n_tiles):
    f32, bf16 = jnp.float32, jnp.bfloat16
    t, _, _ = _pipeline_tiles(n_tiles, nq)
    _fill_first_steps(t, s_odd, m_odd, acc_ref, sum_ref)

    def step(s1_ref, m1_ref, s2_ref, m2_ref):
        top = lax.broadcasted_iota(jnp.int32, (LANES, Q_TILE), 0) < B_HEAD
        qt = q_ref[...].astype(f32)
        w_ref[0] = jnp.where(top, qt, 0.0).astype(bf16)
        w_ref[1] = jnp.where(top, 0.0, qt).astype(bf16)

        ot = jnp.concatenate([acc_ref[st] / jnp.sum(sum_ref[st], axis=0, keepdims=True)
                              for st in range(2)], axis=0)
        o_ref[...] = (ot.T * gate_ref[...].astype(f32)).astype(o_ref.dtype)

        m_row = [jnp.max(m2_ref[st], axis=0, keepdims=True) for st in range(2)]

        m_acc = [None, None]
        l_acc = [None, None]
        o_acc = [None, None]
        for c in range(n_chunks):
            keys = slice(c * K_CHUNK, (c + 1) * K_CHUNK)
            for st in range(2):
                s = jnp.dot(k_ref[keys, :], w_ref[st], preferred_element_type=f32)
                s1_ref[c, st] = s
                cm = _fold_rows(s, jnp.maximum)
                m_acc[st] = cm if c == 0 else jnp.maximum(m_acc[st], cm)
            for st in range(2):
                p = jnp.exp2(s2_ref[c, st] - m_row[st])
                ps = _fold_rows(p, jnp.add)
                l_acc[st] = ps if c == 0 else l_acc[st] + ps
                part = jnp.dot(v_ref[st * B_HEAD:(st + 1) * B_HEAD, keys], p.astype(bf16),
                               preferred_element_type=f32)
                o_acc[st] = part if c == 0 else o_acc[st] + part
        for st in range(2):
            m1_ref[st] = m_acc[st]
            acc_ref[st] = o_acc[st]
            sum_ref[st] = l_acc[st]

    _by_parity(t, step, s_even, m_even, s_odd, m_odd)


def _attn_diff_kernel(rb_ref, lamp_ref, g_ref, q_ref, k_ref, v_ref, gate_ref, band_ref, o_ref,
                      w_ref, s_even, s_odd, m_even, m_odd, acc_ref, sum_ref,
                      *, lam_init, n_chunks, nq, n_tiles):
    f32, bf16 = jnp.float32, jnp.bfloat16
    t, tile1, qi1 = _pipeline_tiles(n_tiles, nq)
    head1 = (tile1 // nq) % A_HEADS
    _fill_first_steps(t, s_odd, m_odd, acc_ref, sum_ref)

    def step(s1_ref, m1_ref, s2_ref, m2_ref):
        row = lax.broadcasted_iota(jnp.int32, (LANES, Q_TILE), 0)
        top = row < A_QK_HEAD
        slot = jnp.bitwise_and(row, BIAS_SLOTS - 1)
        term = jnp.right_shift(jnp.bitwise_and(row, A_QK_HEAD - 1), 4)
        left = rb_ref[NUM_BUCKETS // 2 - 1, head1] * LOG2E
        right = rb_ref[NUM_BUCKETS - 1, head1] * LOG2E
        const = jnp.where(jnp.abs(slot - qi1) <= 1, 0.0, jnp.where(slot < qi1, left, right))
        t0 = const.astype(bf16).astype(f32)
        t1 = (const - t0).astype(bf16).astype(f32)
        t2 = (const - t0 - t1).astype(bf16).astype(f32)
        bias_rows = jnp.where(term == 0, t0, jnp.where(term == 1, t1,
                                                      jnp.where(term == 2, t2, 0.0)))
        qt = q_ref[...].astype(f32)
        w_ref[0] = jnp.where(top, qt, bias_rows).astype(bf16)
        w_ref[1] = jnp.where(top, bias_rows, qt).astype(bf16)

        l_row = [jnp.sum(sum_ref[st], axis=0, keepdims=True) for st in range(2)]
        lp = lamp_ref[...]
        lam = (jnp.exp(jnp.sum(lp[0:1] * lp[1:2], axis=-1, keepdims=True))
               - jnp.exp(jnp.sum(lp[2:3] * lp[3:4], axis=-1, keepdims=True)) + lam_init)
        o = (acc_ref[0] / l_row[0] - lam * (acc_ref[1] / l_row[1])).T
        ms = jnp.mean(o * o, axis=-1, keepdims=True)
        o = (o * lax.rsqrt(ms + EPS) * g_ref[...]) * (1.0 - lam_init)
        o_ref[...] = (o * gate_ref[...].astype(f32)).astype(o_ref.dtype)

        m_row = [jnp.max(m2_ref[st], axis=0, keepdims=True) for st in range(2)]

        m_acc = [None, None]
        l_acc = [None, None]
        o_acc = [None, None]
        for i in range(n_chunks):
            c1 = qi1 - 1 + i
            j1 = jnp.bitwise_and(c1, n_chunks - 1)
            key0 = pl.multiple_of(j1 * K_CHUNK, K_CHUNK)
            if i < BAND_CHUNKS:
                in_range = jnp.logical_and(c1 >= 0, c1 < n_chunks)
                band = jnp.where(in_range, band_ref[head1, i], 0.0)
            for st in range(2):
                s = jnp.dot(k_ref[pl.ds(key0, K_CHUNK), st * LANES:(st + 1) * LANES], w_ref[st],
                            preferred_element_type=f32)
                if i < BAND_CHUNKS:
                    s = s + band
                s1_ref[j1, st] = s
                cm = _fold_rows(s, jnp.maximum)
                m_acc[st] = cm if i == 0 else jnp.maximum(m_acc[st], cm)
            for st in range(2):
                p = jnp.exp2(s2_ref[i, st] - m_row[st])
                ps = _fold_rows(p, jnp.add)
                l_acc[st] = ps if i == 0 else l_acc[st] + ps
                part = jnp.dot(v_ref[:, i * K_CHUNK:(i + 1) * K_CHUNK], p.astype(bf16),
                               preferred_element_type=f32)
                o_acc[st] = part if i == 0 else o_acc[st] + part
        for st in range(2):
            m1_ref[st] = m_acc[st]
            acc_ref[st] = o_acc[st]
            sum_ref[st] = l_acc[st]

    _by_parity(t, step, s_even, m_even, s_odd, m_odd)


def _attn_call(kernel_fn, operands, prefix_specs, q_arr, k_arr, v_arr, gate_arr, band_arr,
               *, n_pairs, per_pair, batch, seq, name):
    m = gate_arr.shape[0]
    nq = seq // Q_TILE
    n_chunks = seq // K_CHUNK
    n_tiles = batch * n_pairs * nq
    k_lanes = k_arr.shape[1] // (n_pairs if per_pair else 1)
    v_rows = LANES if per_pair else B_HEAD

    def decode(tile):
        return tile // (n_pairs * nq), (tile // nq) % n_pairs, tile % nq

    def stage1(t):
        return decode(jnp.minimum(t, n_tiles - 1))

    def stage2(t):
        return decode(jnp.clip(t - 1, 0, n_tiles - 1))

    def stage3(t):
        return decode(jnp.maximum(t - 2, 0))

    def q_map(t):
        b, p, i = stage1(t)
        return p, b * nq + i

    def k_map(t):
        b, p, _ = stage1(t)
        return b, p if per_pair else 0

    def v_map(t):
        b, p, _ = stage2(t)
        return p if per_pair else 0, b

    def out_map(t):
        b, p, i = stage3(t)
        return b * nq + i, p

    in_specs = list(prefix_specs) + [
        pl.BlockSpec((LANES, Q_TILE), q_map),
        pl.BlockSpec((seq, k_lanes), k_map),
        pl.BlockSpec((LANES, seq), v_map),
        pl.BlockSpec((Q_TILE, LANES), out_map)]
    tail = ()
    if band_arr is not None:
        in_specs.append(pl.BlockSpec(memory_space=pltpu.VMEM))
        tail = (band_arr,)
    scores = pltpu.VMEM((n_chunks, 2, K_CHUNK, Q_TILE), jnp.float32)
    per_key_group = pltpu.VMEM((2, SUBLANES, Q_TILE), jnp.float32)
    scratch = [pltpu.VMEM((2, LANES, Q_TILE), jnp.bfloat16),
               scores, scores,
               per_key_group, per_key_group,
               pltpu.VMEM((2, v_rows, Q_TILE), jnp.float32),
               per_key_group]
    return pl.pallas_call(
        functools.partial(kernel_fn, n_chunks=n_chunks, nq=nq, n_tiles=n_tiles),
        grid=(n_tiles + 2,),
        in_specs=in_specs,
        out_specs=pl.BlockSpec((Q_TILE, LANES), out_map),
        out_shape=jax.ShapeDtypeStruct((m, n_pairs * LANES), jnp.bfloat16),
        scratch_shapes=scratch,
        compiler_params=_cparams(1),
        name=name,
    )(*operands, q_arr, k_arr, v_arr, gate_arr, *tail)


def _attn_diff_call(rel_bias, lam_params, subln_g, qa, ka, va, sa, band, lam_init, batch, seq):
    n_chunks = seq // K_CHUNK
    assert n_chunks <= BIAS_SLOTS and n_chunks & (n_chunks - 1) == 0
    prefix = [pl.BlockSpec(memory_space=pltpu.SMEM),
              pl.BlockSpec((4, A_QK_HEAD), lambda t: (0, 0)),
              pl.BlockSpec((1, A_V_HEAD), lambda t: (0, 0))]
    return _attn_call(functools.partial(_attn_diff_kernel, lam_init=lam_init),
                      (rel_bias, lam_params, subln_g), prefix, qa, ka, va, sa, band,
                      n_pairs=A_HEADS, per_pair=True, batch=batch, seq=seq, name="attn_diff")


def _attn_gqa_call(qb, kb, vb, sb, batch, seq):
    return _attn_call(_attn_gqa_kernel, (), [], qb, kb, vb, sb, None,
                      n_pairs=B_Q_HEADS // 2, per_pair=False, batch=batch, seq=seq,
                      name="attn_gqa")


def _out_kernel(ya_ref, yb_ref, wa_ref, wb_ref, x_ref, g_ref, o_ref):
    f32 = jnp.float32
    y = (jnp.dot(ya_ref[...], wa_ref[...], preferred_element_type=f32)
         + jnp.dot(yb_ref[...], wb_ref[...], preferred_element_type=f32))
    ms = jnp.mean(y * y, axis=-1, keepdims=True)
    o_ref[...] = x_ref[...] + y * lax.rsqrt(ms + EPS) * g_ref[...]


def _out_call(ya, yb, wa, wb, xf, g_post, layer):
    m = xf.shape[0]
    row = lambda i: (i, 0)
    const = lambda i: (0, 0)
    return pl.pallas_call(
        _out_kernel,
        grid=(m // ROW_TILE,),
        in_specs=[pl.BlockSpec((ROW_TILE, A_WIDTH), row),
                  pl.BlockSpec((ROW_TILE, B_WIDTH), row),
                  pl.BlockSpec((None, A_WIDTH, D_MODEL), lambda i: (layer, 0, 0)),
                  pl.BlockSpec((None, B_WIDTH, D_MODEL), lambda i: (layer, 0, 0)),
                  pl.BlockSpec((ROW_TILE, D_MODEL), row),
                  pl.BlockSpec((1, D_MODEL), const)],
        out_specs=pl.BlockSpec((ROW_TILE, D_MODEL), row),
        out_shape=jax.ShapeDtypeStruct((m, D_MODEL), jnp.float32),
        compiler_params=_cparams(1),
        name="out_proj",
    )(ya, yb, wa, wb, xf, g_post)


def _layer_weights(w_in_l, w_out_l, q_norm_g_l, k_norm_g_l):
    bf16 = jnp.bfloat16
    d = w_in_l.shape[0]
    per_group = B_Q_HEADS // B_KV_HEADS
    half = B_HEAD // 2
    bq = w_in_l[:, _BQ:_BK].reshape(d, B_KV_HEADS, per_group, half, 2)
    bq = bq.transpose(0, 2, 1, 4, 3).reshape(d, B_WIDTH)
    bk = w_in_l[:, _BK:_BV].reshape(d, B_KV_HEADS, half, 2)
    bk = bk.transpose(0, 1, 3, 2).reshape(d, B_KV_HEADS * B_HEAD)
    bg = w_in_l[:, _BG:].reshape(d, B_KV_HEADS, per_group, B_HEAD)
    bg = bg.transpose(0, 2, 1, 3).reshape(d, B_WIDTH)
    wn = jnp.concatenate([w_in_l[:, _AK:_AV], w_in_l[:, _AG:_BQ], bg], axis=1).astype(bf16)
    wt = jnp.concatenate([w_in_l[:, _AQ:_AK], w_in_l[:, _AV:_AG], bq, bk, w_in_l[:, _BV:_BG]],
                         axis=1).astype(bf16).T
    wa = w_out_l[:A_WIDTH].astype(bf16)
    wb = w_out_l[A_WIDTH:].reshape(B_KV_HEADS, per_group, B_HEAD, -1)
    wb = wb.transpose(1, 0, 2, 3).reshape(B_WIDTH, -1).astype(bf16)
    qg = q_norm_g_l.reshape(half, 2).T.reshape(B_HEAD, 1)
    kg = k_norm_g_l.reshape(half, 2).T.reshape(B_HEAD, 1)
    return wn, wt, wa, wb, qg, kg


def kernel(x, rel_bias, pre_norm_g, w_in, diff_lambda, diff_subln_g, q_norm_g, k_norm_g,
           w_out, post_norm_g):
    batch, seq, d_model = x.shape
    xf = x.reshape(batch * seq, d_model)
    band = _band_call(rel_bias)
    cos_t, sin_t = _rope_tables_t(seq)
    wn, wt, wa, wb, qg, kg = jax.vmap(_layer_weights)(w_in, w_out, q_norm_g, k_norm_g)
    for l in range(DEPTH):
        lam_init = 0.8 - 0.6 * math.exp(-0.3 * l)
        qa, ka, va, sa, qb, kb, vb, sb = _proj_call(
            xf, pre_norm_g[l].reshape(1, d_model), wn, wt, cos_t, sin_t, qg, kg, seq, l)
        ya = _attn_diff_call(rel_bias, diff_lambda[l], diff_subln_g[l].reshape(1, A_V_HEAD),
                             qa, ka, va, sa, band, lam_init, batch, seq)
        yb = _attn_gqa_call(qb, kb, vb, sb, batch, seq)
        xf = _out_call(ya, yb, wa, wb, xf, post_norm_g[l].reshape(1, d_model), l)
    return xf.reshape(batch, seq, d_model)
```

```python
import functools
import math

import jax
import jax.numpy as jnp
from jax import lax
from jax.experimental import pallas as pl
from jax.experimental.pallas import tpu as pltpu

D_MODEL = 1024
DEPTH = 2
A_WIDTH = 512
B_WIDTH = 512
A_HEADS = 4
A_V_HEAD = 128
A_QK_HEAD = 64
B_HEAD = 64
B_Q_HEADS = 8
B_KV_HEADS = 2
GRID_W = 64
ROPE_THETA = 10000.0
NUM_BUCKETS = 32
MAX_DISTANCE = 128
EPS = 1e-6
LOG2E = math.log2(math.e)
Q_SCALE = (A_QK_HEAD ** -0.5) * LOG2E

LANES = 128
SUBLANES = 8
ROW_TILE = 1024
Q_TILE = 256
K_CHUNK = 256
BAND_CHUNKS = 3
BIAS_SLOTS = 16
BIAS_SLOT_BITS = BIAS_SLOTS.bit_length() - 1
BIAS_TERMS = 3
VMEM_LIMIT = 48 * 1024 * 1024

_AQ, _AK, _AV, _AG, _BQ, _BK, _BV, _BG = 0, 512, 1024, 1536, 2048, 2560, 2688, 2816
_T_AQ, _T_AV, _T_BQ, _T_BK, _T_BV, _T_END = 0, 512, 1024, 1536, 1664, 1792
_N_AK, _N_AG, _N_BG, _N_END = 0, 512, 1024, 1536


def _t5_bucket(rel):
    nb = NUM_BUCKETS // 2
    max_exact = nb // 2
    n = jnp.abs(rel)
    large = max_exact + (jnp.log(jnp.maximum(n, 1).astype(jnp.float32) / max_exact)
                         / math.log(MAX_DISTANCE / max_exact) * (nb - max_exact)).astype(jnp.int32)
    large = jnp.minimum(large, nb - 1)
    return jnp.where(rel > 0, nb, 0) + jnp.where(n < max_exact, n, large)


def _rope_tables_t(n):
    rows = n // GRID_W
    axis_dim = B_HEAD // 2
    inv = ROPE_THETA ** (-jnp.arange(0, axis_dim, 2, dtype=jnp.float32) / axis_dim)
    row_ang = (jnp.arange(rows, dtype=jnp.float32)[:, None] * inv).T
    col_ang = (jnp.arange(GRID_W, dtype=jnp.float32)[:, None] * inv).T

    def expand(fn):
        by_row = jnp.repeat(fn(row_ang), GRID_W, axis=1)
        by_col = jnp.tile(fn(col_ang), (1, rows))
        return jnp.concatenate([by_row, by_col], axis=0)

    return expand(jnp.cos), expand(jnp.sin)


def _cparams(n_axes):
    return pltpu.CompilerParams(dimension_semantics=("arbitrary",) * n_axes,
                                vmem_limit_bytes=VMEM_LIMIT)


def _bias_slot_mask(slot_index, chunk):
    term = jnp.right_shift(jnp.bitwise_and(slot_index, A_QK_HEAD - 1), BIAS_SLOT_BITS)
    in_terms = term < BIAS_TERMS
    return jnp.logical_and(in_terms, jnp.bitwise_and(slot_index, BIAS_SLOTS - 1) == chunk)


BAND_TABLE = 2 * Q_TILE


def _band_kernel(rb_ref, idx_ref, out_ref):
    h = pl.program_id(0)
    for d in range(BAND_CHUNKS):
        idx = jnp.broadcast_to(idx_ref[d], (SUBLANES, BAND_TABLE))
        vals = jnp.zeros(idx.shape, jnp.float32)
        for b in range(NUM_BUCKETS):
            vals = jnp.where(idx == b, rb_ref[b, h] * LOG2E, vals)

        for j0 in range(0, K_CHUNK, SUBLANES):
            rolled = pltpu.roll(vals, (j0 + Q_TILE + 1) % BAND_TABLE, 1, stride=1, stride_axis=0)
            out_ref[0, d, j0:j0 + SUBLANES, :] = rolled[:, :Q_TILE]


def _band_call(rel_bias):
    d = jnp.arange(BAND_CHUNKS, dtype=jnp.int32)[:, None, None]
    c = jnp.arange(BAND_TABLE, dtype=jnp.int32)[None, None, :]
    idx = _t5_bucket((d - 1) * K_CHUNK + (Q_TILE - 1) - c)
    idx = jnp.take(jnp.arange(NUM_BUCKETS, dtype=jnp.int32), idx)
    return pl.pallas_call(
        _band_kernel,
        grid=(A_HEADS,),
        in_specs=[pl.BlockSpec(memory_space=pltpu.SMEM),
                  pl.BlockSpec((BAND_CHUNKS, 1, BAND_TABLE), lambda h: (0, 0, 0))],
        out_specs=pl.BlockSpec((1, BAND_CHUNKS, K_CHUNK, Q_TILE), lambda h: (h, 0, 0, 0)),
        out_shape=jax.ShapeDtypeStruct((A_HEADS, BAND_CHUNKS, K_CHUNK, Q_TILE), jnp.float32),
        compiler_params=_cparams(1),
        name="bias_band",
    )(rel_bias, idx)


def _proj_kernel(x_ref, g_ref, wn_ref, wt_ref, cos_ref, sin_ref, qg_ref, kg_ref,
                 qa_ref, ka_ref, va_ref, sa_ref, qb_ref, kb_ref, vb_ref, sb_ref,
                 *, tiles_per_seq):
    f32, bf16 = jnp.float32, jnp.bfloat16
    x = x_ref[...]
    ms = jnp.mean(x * x, axis=-1, keepdims=True)
    h = (x * lax.rsqrt(ms + EPS) * g_ref[...]).astype(bf16)

    def nat(lo, hi):
        return jnp.dot(h, wn_ref[:, lo:hi], preferred_element_type=f32)

    def silu(v):
        return v * jax.nn.sigmoid(v)

    pt = lax.dot_general(wt_ref[...], h, (((1,), (1,)), ((), ())), preferred_element_type=f32)
    qa_ref[...] = (pt[_T_AQ:_T_AV] * Q_SCALE).astype(bf16)
    va_ref[...] = pt[_T_AV:_T_BQ].astype(bf16)
    vb_ref[...] = pt[_T_BV:_T_END].astype(bf16)

    cos = cos_ref[...]
    sin = sin_ref[...]
    half = B_HEAD // 2

    def norm_rope(xt, g):
        msq = jnp.mean(xt * xt, axis=0, keepdims=True)
        y = xt * lax.rsqrt(msq + EPS) * g
        e, o = y[:half], y[half:]
        return jnp.concatenate([e * cos - o * sin, e * sin + o * cos], axis=0)

    qg = qg_ref[...]
    qn = jnp.concatenate([norm_rope(pt[_T_BQ + i * B_HEAD:_T_BQ + (i + 1) * B_HEAD], qg)
                          for i in range(B_Q_HEADS)], axis=0) * Q_SCALE
    qb_ref[...] = qn.astype(bf16)
    kg = kg_ref[...]
    kn = jnp.concatenate([norm_rope(pt[_T_BK + i * B_HEAD:_T_BK + (i + 1) * B_HEAD], kg)
                          for i in range(B_KV_HEADS)], axis=0)
    kb_ref[...] = kn.T.astype(bf16)

    sa_ref[...] = silu(nat(_N_AG, _N_BG)).astype(bf16)
    sb_ref[...] = silu(nat(_N_BG, _N_END)).astype(bf16)

    ak = nat(_N_AK, _N_AG)
    lane = lax.broadcasted_iota(jnp.int32, (ROW_TILE, LANES), 1)
    row = lax.broadcasted_iota(jnp.int32, (ROW_TILE, LANES), 0)
    chunk = ((pl.program_id(0) % tiles_per_seq) * (ROW_TILE // K_CHUNK)
             + jnp.right_shift(row, K_CHUNK.bit_length() - 1))
    hot = jnp.where(_bias_slot_mask(lane, chunk), 1.0, 0.0)
    low = lane < A_QK_HEAD
    for hd in range(A_HEADS):
        akh = ak[:, hd * LANES:(hd + 1) * LANES]
        ka_ref[:, 2 * hd * LANES:(2 * hd + 1) * LANES] = jnp.where(low, akh, hot).astype(bf16)
        ka_ref[:, (2 * hd + 1) * LANES:(2 * hd + 2) * LANES] = jnp.where(low, hot, akh).astype(bf16)


def _proj_call(xf, g_pre, wn, wt, cos_t, sin_t, qg, kg, seq, layer):
    m = xf.shape[0]
    tiles_per_seq = seq // ROW_TILE
    bf16 = jnp.bfloat16
    row = lambda i: (i, 0)
    col = lambda i: (0, i)
    const = lambda i: (0, 0)
    of_layer = lambda i: (layer, 0, 0)
    pos = lambda i: (0, i % tiles_per_seq)
    return pl.pallas_call(
        functools.partial(_proj_kernel, tiles_per_seq=tiles_per_seq),
        grid=(m // ROW_TILE,),
        in_specs=[pl.BlockSpec((ROW_TILE, D_MODEL), row),
                  pl.BlockSpec((1, D_MODEL), const),
                  pl.BlockSpec((None,) + wn.shape[1:], of_layer),
                  pl.BlockSpec((None,) + wt.shape[1:], of_layer),
                  pl.BlockSpec((B_HEAD // 2, ROW_TILE), pos),
                  pl.BlockSpec((B_HEAD // 2, ROW_TILE), pos),
                  pl.BlockSpec((None, B_HEAD, 1), of_layer),
                  pl.BlockSpec((None, B_HEAD, 1), of_layer)],
        out_specs=[pl.BlockSpec((A_WIDTH, ROW_TILE), col),
                   pl.BlockSpec((ROW_TILE, 2 * A_WIDTH), row),
                   pl.BlockSpec((A_WIDTH, ROW_TILE), col),
                   pl.BlockSpec((ROW_TILE, A_WIDTH), row),
                   pl.BlockSpec((B_WIDTH, ROW_TILE), col),
                   pl.BlockSpec((ROW_TILE, B_KV_HEADS * B_HEAD), row),
                   pl.BlockSpec((B_KV_HEADS * B_HEAD, ROW_TILE), col),
                   pl.BlockSpec((ROW_TILE, B_WIDTH), row)],
        out_shape=[jax.ShapeDtypeStruct((A_WIDTH, m), bf16),
                   jax.ShapeDtypeStruct((m, 2 * A_WIDTH), bf16),
                   jax.ShapeDtypeStruct((A_WIDTH, m), bf16),
                   jax.ShapeDtypeStruct((m, A_WIDTH), bf16),
                   jax.ShapeDtypeStruct((B_WIDTH, m), bf16),
                   jax.ShapeDtypeStruct((m, B_KV_HEADS * B_HEAD), bf16),
                   jax.ShapeDtypeStruct((B_KV_HEADS * B_HEAD, m), bf16),
                   jax.ShapeDtypeStruct((m, B_WIDTH), bf16)],
        compiler_params=_cparams(1),
        name="in_proj",
    )(xf, g_pre, wn, wt, cos_t, sin_t, qg, kg)


def _fold_rows(x, op):
    parts = [x[g * SUBLANES:(g + 1) * SUBLANES] for g in range(x.shape[0] // SUBLANES)]
    while len(parts) > 1:
        parts = [op(parts[i], parts[i + 1]) for i in range(0, len(parts), 2)]
    return parts[0]


def _pipeline_tiles(n_tiles, nq):
    t = pl.program_id(0)
    tile1 = jnp.minimum(t, n_tiles - 1)
    return t, tile1, tile1 % nq


def _fill_first_steps(t, s_odd, m_odd, acc_ref, sum_ref):
    @pl.when(t == 0)
    def _():
        s_odd[...] = jnp.zeros(s_odd.shape, jnp.float32)
        m_odd[...] = jnp.zeros(m_odd.shape, jnp.float32)
        acc_ref[...] = jnp.zeros(acc_ref.shape, jnp.float32)
        sum_ref[...] = jnp.ones(sum_ref.shape, jnp.float32)


def _by_parity(t, step, s_even, m_even, s_odd, m_odd):
    @pl.when(t % 2 == 0)
    def _():
        step(s_even, m_even, s_odd, m_odd)

    @pl.when(t % 2 == 1)
    def _():
        step(s_odd, m_odd, s_even, m_even)


def _attn_gqa_kernel(q_ref, k_ref, v_ref, gate_ref, o_ref,
                     w_ref, s_even, s_odd, m_even, m_odd, acc_ref, sum_ref,
                     *, n_chunks, nq, n_tiles):
    f32, bf16 = jnp.float32, jnp.bfloat16
    t, _, _ = _pipeline_tiles(n_tiles, nq)
    _fill_first_steps(t, s_odd, m_odd, acc_ref, sum_ref)

    def step(s1_ref, m1_ref, s2_ref, m2_ref):
        top = lax.broadcasted_iota(jnp.int32, (LANES, Q_TILE), 0) < B_HEAD
        qt = q_ref[...].astype(f32)
        w_ref[0] = jnp.where(top, qt, 0.0).astype(bf16)
        w_ref[1] = jnp.where(top, 0.0, qt).astype(bf16)

        ot = jnp.concatenate([acc_ref[st] / jnp.sum(sum_ref[st], axis=0, keepdims=True)
                              for st in range(2)], axis=0)
        o_ref[...] = (ot.T * gate_ref[...].astype(f32)).astype(o_ref.dtype)

        m_row = [jnp.max(m2_ref[st], axis=0, keepdims=True) for st in range(2)]

        m_acc = [None, None]
        l_acc = [None, None]
        o_acc = [None, None]
        for c in range(n_chunks):
            keys = slice(c * K_CHUNK, (c + 1) * K_CHUNK)
            for st in range(2):
                s = jnp.dot(k_ref[keys, :], w_ref[st], preferred_element_type=f32)
                s1_ref[c, st] = s
                cm = _fold_rows(s, jnp.maximum)
                m_acc[st] = cm if c == 0 else jnp.maximum(m_acc[st], cm)
            for st in range(2):
                p = jnp.exp2(s2_ref[c, st] - m_row[st])
                ps = _fold_rows(p, jnp.add)
                l_acc[st] = ps if c == 0 else l_acc[st] + ps
                part = jnp.dot(v_ref[st * B_HEAD:(st + 1) * B_HEAD, keys], p.astype(bf16),
                               preferred_element_type=f32)
                o_acc[st] = part if c == 0 else o_acc[st] + part
        for st in range(2):
            m1_ref[st] = m_acc[st]
            acc_ref[st] = o_acc[st]
            sum_ref[st] = l_acc[st]

    _by_parity(t, step, s_even, m_even, s_odd, m_odd)


def _attn_diff_kernel(rb_ref, lamp_ref, g_ref, q_ref, k_ref, v_ref, gate_ref, band_ref, o_ref,
                      w_ref, s_even, s_odd, m_even, m_odd, acc_ref, sum_ref,
                      *, lam_init, n_chunks, nq, n_tiles):
    f32, bf16 = jnp.float32, jnp.bfloat16
    t, tile1, qi1 = _pipeline_tiles(n_tiles, nq)
    head1 = (tile1 // nq) % A_HEADS
    _fill_first_steps(t, s_odd, m_odd, acc_ref, sum_ref)

    def step(s1_ref, m1_ref, s2_ref, m2_ref):
        row = lax.broadcasted_iota(jnp.int32, (LANES, Q_TILE), 0)
        top = row < A_QK_HEAD
        slot = jnp.bitwise_and(row, BIAS_SLOTS - 1)
        term = jnp.right_shift(jnp.bitwise_and(row, A_QK_HEAD - 1), BIAS_SLOT_BITS)
        left = rb_ref[NUM_BUCKETS // 2 - 1, head1] * LOG2E
        right = rb_ref[NUM_BUCKETS - 1, head1] * LOG2E
        const = jnp.where(jnp.abs(slot - qi1) <= 1, 0.0, jnp.where(slot < qi1, left, right))
        t0 = const.astype(bf16).astype(f32)
        t1 = (const - t0).astype(bf16).astype(f32)
        t2 = (const - t0 - t1).astype(bf16).astype(f32)
        bias_rows = jnp.where(term == 0, t0, jnp.where(term == 1, t1,
                                                      jnp.where(term == 2, t2, 0.0)))
        qt = q_ref[...].astype(f32)
        w_ref[0] = jnp.where(top, qt, bias_rows).astype(bf16)
        w_ref[1] = jnp.where(top, bias_rows, qt).astype(bf16)

        l_row = [jnp.sum(sum_ref[st], axis=0, keepdims=True) for st in range(2)]
        lp = lamp_ref[...]
        lam = (jnp.exp(jnp.sum(lp[0:1] * lp[1:2], axis=-1, keepdims=True))
               - jnp.exp(jnp.sum(lp[2:3] * lp[3:4], axis=-1, keepdims=True)) + lam_init)
        o = (acc_ref[0] / l_row[0] - lam * (acc_ref[1] / l_row[1])).T
        ms = jnp.mean(o * o, axis=-1, keepdims=True)
        o = (o * lax.rsqrt(ms + EPS) * g_ref[...]) * (1.0 - lam_init)
        o_ref[...] = (o * gate_ref[...].astype(f32)).astype(o_ref.dtype)

        m_row = [jnp.max(m2_ref[st], axis=0, keepdims=True) for st in range(2)]

        m_acc = [None, None]
        l_acc = [None, None]
        o_acc = [None, None]
        for i in range(n_chunks):
            c1 = qi1 - 1 + i
            j1 = jnp.bitwise_and(c1, n_chunks - 1)
            key0 = pl.multiple_of(j1 * K_CHUNK, K_CHUNK)
            if i < BAND_CHUNKS:
                in_range = jnp.logical_and(c1 >= 0, c1 < n_chunks)
                band = jnp.where(in_range, band_ref[0, i], 0.0)
            for st in range(2):
                s = jnp.dot(k_ref[pl.ds(key0, K_CHUNK), st * LANES:(st + 1) * LANES], w_ref[st],
                            preferred_element_type=f32)
                if i < BAND_CHUNKS:
                    s = s + band
                s1_ref[j1, st] = s
                cm = _fold_rows(s, jnp.maximum)
                m_acc[st] = cm if i == 0 else jnp.maximum(m_acc[st], cm)
            for st in range(2):
                p = jnp.exp2(s2_ref[i, st] - m_row[st])
                ps = _fold_rows(p, jnp.add)
                l_acc[st] = ps if i == 0 else l_acc[st] + ps
                part = jnp.dot(v_ref[:, i * K_CHUNK:(i + 1) * K_CHUNK], p.astype(bf16),
                               preferred_element_type=f32)
                o_acc[st] = part if i == 0 else o_acc[st] + part
        for st in range(2):
            m1_ref[st] = m_acc[st]
            acc_ref[st] = o_acc[st]
            sum_ref[st] = l_acc[st]

    _by_parity(t, step, s_even, m_even, s_odd, m_odd)


def _attn_call(kernel_fn, operands, prefix_specs, q_arr, k_arr, v_arr, gate_arr, band_arr,
               *, n_pairs, per_pair, batch, seq, name):
    m = gate_arr.shape[0]
    nq = seq // Q_TILE
    n_chunks = seq // K_CHUNK
    n_tiles = batch * n_pairs * nq
    k_lanes = k_arr.shape[1] // (n_pairs if per_pair else 1)
    v_rows = LANES if per_pair else B_HEAD

    def decode(tile):
        return tile // (n_pairs * nq), (tile // nq) % n_pairs, tile % nq

    def stage1(t):
        return decode(jnp.minimum(t, n_tiles - 1))

    def stage2(t):
        return decode(jnp.clip(t - 1, 0, n_tiles - 1))

    def stage3(t):
        return decode(jnp.maximum(t - 2, 0))

    def q_map(t):
        b, p, i = stage1(t)
        return p, b * nq + i

    def k_map(t):
        b, p, _ = stage1(t)
        return b, p if per_pair else 0

    def v_map(t):
        b, p, _ = stage2(t)
        return p if per_pair else 0, b

    def out_map(t):
        b, p, i = stage3(t)
        return b * nq + i, p

    in_specs = list(prefix_specs) + [
        pl.BlockSpec((LANES, Q_TILE), q_map),
        pl.BlockSpec((seq, k_lanes), k_map),
        pl.BlockSpec((LANES, seq), v_map),
        pl.BlockSpec((Q_TILE, LANES), out_map)]
    tail = ()
    if band_arr is not None:
        in_specs.append(pl.BlockSpec((1, BAND_CHUNKS, K_CHUNK, Q_TILE),
                                     lambda t: (stage1(t)[1], 0, 0, 0)))
        tail = (band_arr,)
    scores = pltpu.VMEM((n_chunks, 2, K_CHUNK, Q_TILE), jnp.float32)
    per_key_group = pltpu.VMEM((2, SUBLANES, Q_TILE), jnp.float32)
    scratch = [pltpu.VMEM((2, LANES, Q_TILE), jnp.bfloat16),
               scores, scores,
               per_key_group, per_key_group,
               pltpu.VMEM((2, v_rows, Q_TILE), jnp.float32),
               per_key_group]
    return pl.pallas_call(
        functools.partial(kernel_fn, n_chunks=n_chunks, nq=nq, n_tiles=n_tiles),
        grid=(n_tiles + 2,),
        in_specs=in_specs,
        out_specs=pl.BlockSpec((Q_TILE, LANES), out_map),
        out_shape=jax.ShapeDtypeStruct((m, n_pairs * LANES), jnp.bfloat16),
        scratch_shapes=scratch,
        compiler_params=_cparams(1),
        name=name,
    )(*operands, q_arr, k_arr, v_arr, gate_arr, *tail)


def _attn_diff_call(rel_bias, lam_params, subln_g, qa, ka, va, sa, band, lam_init, batch, seq):
    n_chunks = seq // K_CHUNK
    assert n_chunks <= BIAS_SLOTS and n_chunks & (n_chunks - 1) == 0
    prefix = [pl.BlockSpec(memory_space=pltpu.SMEM),
              pl.BlockSpec((4, A_QK_HEAD), lambda t: (0, 0)),
              pl.BlockSpec((1, A_V_HEAD), lambda t: (0, 0))]
    return _attn_call(functools.partial(_attn_diff_kernel, lam_init=lam_init),
                      (rel_bias, lam_params, subln_g), prefix, qa, ka, va, sa, band,
                      n_pairs=A_HEADS, per_pair=True, batch=batch, seq=seq, name="attn_diff")


def _attn_gqa_call(qb, kb, vb, sb, batch, seq):
    return _attn_call(_attn_gqa_kernel, (), [], qb, kb, vb, sb, None,
                      n_pairs=B_Q_HEADS // 2, per_pair=False, batch=batch, seq=seq,
                      name="attn_gqa")


def _out_kernel(ya_ref, yb_ref, wa_ref, wb_ref, x_ref, g_ref, o_ref):
    f32 = jnp.float32
    y = (jnp.dot(ya_ref[...], wa_ref[...], preferred_element_type=f32)
         + jnp.dot(yb_ref[...], wb_ref[...], preferred_element_type=f32))
    ms = jnp.mean(y * y, axis=-1, keepdims=True)
    o_ref[...] = x_ref[...] + y * lax.rsqrt(ms + EPS) * g_ref[...]


def _out_call(ya, yb, wa, wb, xf, g_post, layer):
    m = xf.shape[0]
    row = lambda i: (i, 0)
    const = lambda i: (0, 0)
    return pl.pallas_call(
        _out_kernel,
        grid=(m // ROW_TILE,),
        in_specs=[pl.BlockSpec((ROW_TILE, A_WIDTH), row),
                  pl.BlockSpec((ROW_TILE, B_WIDTH), row),
                  pl.BlockSpec((None, A_WIDTH, D_MODEL), lambda i: (layer, 0, 0)),
                  pl.BlockSpec((None, B_WIDTH, D_MODEL), lambda i: (layer, 0, 0)),
                  pl.BlockSpec((ROW_TILE, D_MODEL), row),
                  pl.BlockSpec((1, D_MODEL), const)],
        out_specs=pl.BlockSpec((ROW_TILE, D_MODEL), row),
        out_shape=jax.ShapeDtypeStruct((m, D_MODEL), jnp.float32),
        compiler_params=_cparams(1),
        name="out_proj",
    )(ya, yb, wa, wb, xf, g_post)


def _layer_weights(w_in_l, w_out_l, q_norm_g_l, k_norm_g_l):
    bf16 = jnp.bfloat16
    d = w_in_l.shape[0]
    per_group = B_Q_HEADS // B_KV_HEADS
    half = B_HEAD // 2
    bq = w_in_l[:, _BQ:_BK].reshape(d, B_KV_HEADS, per_group, half, 2)
    bq = bq.transpose(0, 2, 1, 4, 3).reshape(d, B_WIDTH)
    bk = w_in_l[:, _BK:_BV].reshape(d, B_KV_HEADS, half, 2)
    bk = bk.transpose(0, 1, 3, 2).reshape(d, B_KV_HEADS * B_HEAD)
    bg = w_in_l[:, _BG:].reshape(d, B_KV_HEADS, per_group, B_HEAD)
    bg = bg.transpose(0, 2, 1, 3).reshape(d, B_WIDTH)
    wn = jnp.concatenate([w_in_l[:, _AK:_AV], w_in_l[:, _AG:_BQ], bg], axis=1).astype(bf16)
    wt = jnp.concatenate([w_in_l[:, _AQ:_AK], w_in_l[:, _AV:_AG], bq, bk, w_in_l[:, _BV:_BG]],
                         axis=1).astype(bf16).T
    wa = w_out_l[:A_WIDTH].astype(bf16)
    wb = w_out_l[A_WIDTH:].reshape(B_KV_HEADS, per_group, B_HEAD, -1)
    wb = wb.transpose(1, 0, 2, 3).reshape(B_WIDTH, -1).astype(bf16)
    qg = q_norm_g_l.reshape(half, 2).T.reshape(B_HEAD, 1)
    kg = k_norm_g_l.reshape(half, 2).T.reshape(B_HEAD, 1)
    return wn, wt, wa, wb, qg, kg


def kernel(x, rel_bias, pre_norm_g, w_in, diff_lambda, diff_subln_g, q_norm_g, k_norm_g,
           w_out, post_norm_g):
    batch, seq, d_model = x.shape
    xf = x.reshape(batch * seq, d_model)
    band = _band_call(rel_bias)
    cos_t, sin_t = _rope_tables_t(seq)
    wn, wt, wa, wb, qg, kg = jax.vmap(_layer_weights)(w_in, w_out, q_norm_g, k_norm_g)
    for l in range(DEPTH):
        lam_init = 0.8 - 0.6 * math.exp(-0.3 * l)
        qa, ka, va, sa, qb, kb, vb, sb = _proj_call(
            xf, pre_norm_g[l].reshape(1, d_model), wn, wt, cos_t, sin_t, qg, kg, seq, l)
        ya = _attn_diff_call(rel_bias, diff_lambda[l], diff_subln_g[l].reshape(1, A_V_HEAD),
                             qa, ka, va, sa, band, lam_init, batch, seq)
        yb = _attn_gqa_call(qb, kb, vb, sb, batch, seq)
        xf = _out_call(ya, yb, wa, wb, xf, post_norm_g[l].reshape(1, d_model), l)
    return xf.reshape(batch, seq, d_model)
```

```python
import functools
import math

import jax
import jax.numpy as jnp
from jax import lax
from jax.experimental import pallas as pl
from jax.experimental.pallas import tpu as pltpu

D_MODEL = 1024
DEPTH = 2
A_WIDTH = 512
B_WIDTH = 512
A_HEADS = 4
A_V_HEAD = 128
A_QK_HEAD = 64
B_HEAD = 64
B_Q_HEADS = 8
B_KV_HEADS = 2
GRID_W = 64
ROPE_THETA = 10000.0
NUM_BUCKETS = 32
MAX_DISTANCE = 128
EPS = 1e-6
LOG2E = math.log2(math.e)
Q_SCALE = (A_QK_HEAD ** -0.5) * LOG2E

LANES = 128
SUBLANES = 8
ROW_TILE = 1024
Q_TILE = 256
K_CHUNK = 256
BAND_CHUNKS = 3
BIAS_SLOTS = 16
BIAS_SLOT_BITS = BIAS_SLOTS.bit_length() - 1
BIAS_TERMS = 3
VMEM_LIMIT = 48 * 1024 * 1024

_AQ, _AK, _AV, _AG, _BQ, _BK, _BV, _BG = 0, 512, 1024, 1536, 2048, 2560, 2688, 2816
_T_AQ, _T_AV, _T_BQ, _T_BK, _T_BV, _T_END = 0, 512, 1024, 1536, 1664, 1792
_N_AK, _N_AG, _N_BG, _N_END = 0, 512, 1024, 1536


def _t5_bucket(rel):
    nb = NUM_BUCKETS // 2
    max_exact = nb // 2
    n = jnp.abs(rel)
    large = max_exact + (jnp.log(jnp.maximum(n, 1).astype(jnp.float32) / max_exact)
                         / math.log(MAX_DISTANCE / max_exact) * (nb - max_exact)).astype(jnp.int32)
    large = jnp.minimum(large, nb - 1)
    return jnp.where(rel > 0, nb, 0) + jnp.where(n < max_exact, n, large)


def _rope_tables_t(n):
    rows = n // GRID_W
    axis_dim = B_HEAD // 2
    inv = ROPE_THETA ** (-jnp.arange(0, axis_dim, 2, dtype=jnp.float32) / axis_dim)
    row_ang = (jnp.arange(rows, dtype=jnp.float32)[:, None] * inv).T
    col_ang = (jnp.arange(GRID_W, dtype=jnp.float32)[:, None] * inv).T

    def expand(fn):
        by_row = jnp.repeat(fn(row_ang), GRID_W, axis=1)
        by_col = jnp.tile(fn(col_ang), (1, rows))
        return jnp.concatenate([by_row, by_col], axis=0)

    return expand(jnp.cos), expand(jnp.sin)


def _cparams(n_axes):
    return pltpu.CompilerParams(dimension_semantics=("arbitrary",) * n_axes,
                                vmem_limit_bytes=VMEM_LIMIT)


def _bias_slot_mask(slot_index, chunk):
    term = jnp.right_shift(jnp.bitwise_and(slot_index, A_QK_HEAD - 1), BIAS_SLOT_BITS)
    in_terms = term < BIAS_TERMS
    return jnp.logical_and(in_terms, jnp.bitwise_and(slot_index, BIAS_SLOTS - 1) == chunk)


BAND_TABLE = 2 * Q_TILE


def _band_kernel(rb_ref, idx_ref, out_ref):
    h = pl.program_id(0)
    for d in range(BAND_CHUNKS):
        idx = jnp.broadcast_to(idx_ref[d], (SUBLANES, BAND_TABLE))
        vals = jnp.zeros(idx.shape, jnp.float32)
        for b in range(NUM_BUCKETS):
            vals = jnp.where(idx == b, rb_ref[b, h] * LOG2E, vals)

        for j0 in range(0, K_CHUNK, SUBLANES):
            rolled = pltpu.roll(vals, (j0 + Q_TILE + 1) % BAND_TABLE, 1, stride=1, stride_axis=0)
            out_ref[0, d, j0:j0 + SUBLANES, :] = rolled[:, :Q_TILE]


def _band_call(rel_bias):
    d = jnp.arange(BAND_CHUNKS, dtype=jnp.int32)[:, None, None]
    c = jnp.arange(BAND_TABLE, dtype=jnp.int32)[None, None, :]
    idx = _t5_bucket((d - 1) * K_CHUNK + (Q_TILE - 1) - c)
    idx = jnp.take(jnp.arange(NUM_BUCKETS, dtype=jnp.int32), idx)
    return pl.pallas_call(
        _band_kernel,
        grid=(A_HEADS,),
        in_specs=[pl.BlockSpec(memory_space=pltpu.SMEM),
                  pl.BlockSpec((BAND_CHUNKS, 1, BAND_TABLE), lambda h: (0, 0, 0))],
        out_specs=pl.BlockSpec((1, BAND_CHUNKS, K_CHUNK, Q_TILE), lambda h: (h, 0, 0, 0)),
        out_shape=jax.ShapeDtypeStruct((A_HEADS, BAND_CHUNKS, K_CHUNK, Q_TILE), jnp.float32),
        compiler_params=_cparams(1),
        name="bias_band",
    )(rel_bias, idx)


def _proj_kernel(x_ref, g_ref, wn_ref, wt_ref, cos_ref, sin_ref, qg_ref, kg_ref,
                 qa_ref, ka_ref, va_ref, sa_ref, qb_ref, kb_ref, vb_ref, sb_ref,
                 *, tiles_per_seq):
    f32, bf16 = jnp.float32, jnp.bfloat16
    x = x_ref[...]
    ms = jnp.mean(x * x, axis=-1, keepdims=True)
    h = (x * lax.rsqrt(ms + EPS) * g_ref[...]).astype(bf16)

    def nat(lo, hi):
        return jnp.dot(h, wn_ref[:, lo:hi], preferred_element_type=f32)

    def silu(v):
        return v * jax.nn.sigmoid(v)

    pt = lax.dot_general(wt_ref[...], h, (((1,), (1,)), ((), ())), preferred_element_type=f32)
    qa_ref[...] = (pt[_T_AQ:_T_AV] * Q_SCALE).astype(bf16)
    va_ref[...] = pt[_T_AV:_T_BQ].astype(bf16)
    vb_ref[...] = pt[_T_BV:_T_END].astype(bf16)

    cos = cos_ref[...]
    sin = sin_ref[...]
    half = B_HEAD // 2

    def norm_rope(xt, g):
        msq = jnp.mean(xt * xt, axis=0, keepdims=True)
        y = xt * lax.rsqrt(msq + EPS) * g
        e, o = y[:half], y[half:]
        return jnp.concatenate([e * cos - o * sin, e * sin + o * cos], axis=0)

    qg = qg_ref[...]
    qn = jnp.concatenate([norm_rope(pt[_T_BQ + i * B_HEAD:_T_BQ + (i + 1) * B_HEAD], qg)
                          for i in range(B_Q_HEADS)], axis=0) * Q_SCALE
    qb_ref[...] = qn.astype(bf16)
    kg = kg_ref[...]
    kn = jnp.concatenate([norm_rope(pt[_T_BK + i * B_HEAD:_T_BK + (i + 1) * B_HEAD], kg)
                          for i in range(B_KV_HEADS)], axis=0)
    kb_ref[...] = kn.T.astype(bf16)

    sa_ref[...] = silu(nat(_N_AG, _N_BG)).astype(bf16)
    sb_ref[...] = silu(nat(_N_BG, _N_END)).astype(bf16)

    ak = nat(_N_AK, _N_AG)
    lane = lax.broadcasted_iota(jnp.int32, (ROW_TILE, LANES), 1)
    row = lax.broadcasted_iota(jnp.int32, (ROW_TILE, LANES), 0)
    chunk = ((pl.program_id(0) % tiles_per_seq) * (ROW_TILE // K_CHUNK)
             + jnp.right_shift(row, K_CHUNK.bit_length() - 1))
    hot = jnp.where(_bias_slot_mask(lane, chunk), 1.0, 0.0)
    low = lane < A_QK_HEAD
    for hd in range(A_HEADS):
        akh = ak[:, hd * LANES:(hd + 1) * LANES]
        ka_ref[:, 2 * hd * LANES:(2 * hd + 1) * LANES] = jnp.where(low, akh, hot).astype(bf16)
        ka_ref[:, (2 * hd + 1) * LANES:(2 * hd + 2) * LANES] = jnp.where(low, hot, akh).astype(bf16)


def _proj_call(xf, g_pre, wn, wt, cos_t, sin_t, qg, kg, seq, layer):
    m = xf.shape[0]
    tiles_per_seq = seq // ROW_TILE
    bf16 = jnp.bfloat16
    row = lambda i: (i, 0)
    col = lambda i: (0, i)
    of_layer = lambda i: (layer, 0, 0)
    pos = lambda i: (0, i % tiles_per_seq)
    return pl.pallas_call(
        functools.partial(_proj_kernel, tiles_per_seq=tiles_per_seq),
        grid=(m // ROW_TILE,),
        in_specs=[pl.BlockSpec((ROW_TILE, D_MODEL), row),
                  pl.BlockSpec((None, 1, D_MODEL), of_layer),
                  pl.BlockSpec((None,) + wn.shape[1:], of_layer),
                  pl.BlockSpec((None,) + wt.shape[1:], of_layer),
                  pl.BlockSpec((B_HEAD // 2, ROW_TILE), pos),
                  pl.BlockSpec((B_HEAD // 2, ROW_TILE), pos),
                  pl.BlockSpec((None, B_HEAD, 1), of_layer),
                  pl.BlockSpec((None, B_HEAD, 1), of_layer)],
        out_specs=[pl.BlockSpec((A_WIDTH, ROW_TILE), col),
                   pl.BlockSpec((ROW_TILE, 2 * A_WIDTH), row),
                   pl.BlockSpec((A_WIDTH, ROW_TILE), col),
                   pl.BlockSpec((ROW_TILE, A_WIDTH), row),
                   pl.BlockSpec((B_WIDTH, ROW_TILE), col),
                   pl.BlockSpec((ROW_TILE, B_KV_HEADS * B_HEAD), row),
                   pl.BlockSpec((B_KV_HEADS * B_HEAD, ROW_TILE), col),
                   pl.BlockSpec((ROW_TILE, B_WIDTH), row)],
        out_shape=[jax.ShapeDtypeStruct((A_WIDTH, m), bf16),
                   jax.ShapeDtypeStruct((m, 2 * A_WIDTH), bf16),
                   jax.ShapeDtypeStruct((A_WIDTH, m), bf16),
                   jax.ShapeDtypeStruct((m, A_WIDTH), bf16),
                   jax.ShapeDtypeStruct((B_WIDTH, m), bf16),
                   jax.ShapeDtypeStruct((m, B_KV_HEADS * B_HEAD), bf16),
                   jax.ShapeDtypeStruct((B_KV_HEADS * B_HEAD, m), bf16),
                   jax.ShapeDtypeStruct((m, B_WIDTH), bf16)],
        compiler_params=_cparams(1),
        name="in_proj",
    )(xf, g_pre, wn, wt, cos_t, sin_t, qg, kg)


def _fold_rows(x, op):
    parts = [x[g * SUBLANES:(g + 1) * SUBLANES] for g in range(x.shape[0] // SUBLANES)]
    while len(parts) > 1:
        parts = [op(parts[i], parts[i + 1]) for i in range(0, len(parts), 2)]
    return parts[0]


def _pipeline_tiles(n_tiles, nq):
    t = pl.program_id(0)
    tile1 = jnp.minimum(t, n_tiles - 1)
    return t, tile1, tile1 % nq


def _fill_first_steps(t, s_odd, m_odd, acc_ref, sum_ref):
    @pl.when(t == 0)
    def _():
        s_odd[...] = jnp.zeros(s_odd.shape, jnp.float32)
        m_odd[...] = jnp.zeros(m_odd.shape, jnp.float32)
        acc_ref[...] = jnp.zeros(acc_ref.shape, jnp.float32)
        sum_ref[...] = jnp.ones(sum_ref.shape, jnp.float32)


def _by_parity(t, step, s_even, m_even, s_odd, m_odd):
    @pl.when(t % 2 == 0)
    def _():
        step(s_even, m_even, s_odd, m_odd)

    @pl.when(t % 2 == 1)
    def _():
        step(s_odd, m_odd, s_even, m_even)


def _attn_gqa_kernel(q_ref, k_ref, v_ref, gate_ref, o_ref,
                     w_ref, s_even, s_odd, m_even, m_odd, acc_ref, sum_ref,
                     *, n_chunks, nq, n_tiles):
    f32, bf16 = jnp.float32, jnp.bfloat16
    t, _, _ = _pipeline_tiles(n_tiles, nq)
    _fill_first_steps(t, s_odd, m_odd, acc_ref, sum_ref)

    def step(s1_ref, m1_ref, s2_ref, m2_ref):
        top = lax.broadcasted_iota(jnp.int32, (LANES, Q_TILE), 0) < B_HEAD
        qt = q_ref[...].astype(f32)
        w_ref[0] = jnp.where(top, qt, 0.0).astype(bf16)
        w_ref[1] = jnp.where(top, 0.0, qt).astype(bf16)

        ot = jnp.concatenate([acc_ref[st] / jnp.sum(sum_ref[st], axis=0, keepdims=True)
                              for st in range(2)], axis=0)
        o_ref[...] = (ot.T * gate_ref[...].astype(f32)).astype(o_ref.dtype)

        m_row = [jnp.max(m2_ref[st], axis=0, keepdims=True) for st in range(2)]

        m_acc = [None, None]
        l_acc = [None, None]
        o_acc = [None, None]
        for c in range(n_chunks):
            keys = slice(c * K_CHUNK, (c + 1) * K_CHUNK)
            for st in range(2):
                s = jnp.dot(k_ref[keys, :], w_ref[st], preferred_element_type=f32)
                s1_ref[c, st] = s
                cm = _fold_rows(s, jnp.maximum)
                m_acc[st] = cm if c == 0 else jnp.maximum(m_acc[st], cm)
            for st in range(2):
                p = jnp.exp2(s2_ref[c, st] - m_row[st])
                ps = _fold_rows(p, jnp.add)
                l_acc[st] = ps if c == 0 else l_acc[st] + ps
                part = jnp.dot(v_ref[st * B_HEAD:(st + 1) * B_HEAD, keys], p.astype(bf16),
                               preferred_element_type=f32)
                o_acc[st] = part if c == 0 else o_acc[st] + part
        for st in range(2):
            m1_ref[st] = m_acc[st]
            acc_ref[st] = o_acc[st]
            sum_ref[st] = l_acc[st]

    _by_parity(t, step, s_even, m_even, s_odd, m_odd)


def _attn_diff_kernel(rb_ref, lamp_ref, g_ref, q_ref, k_ref, v_ref, gate_ref, band_ref, o_ref,
                      w_ref, s_even, s_odd, m_even, m_odd, acc_ref, sum_ref,
                      *, lam_init, n_chunks, nq, n_tiles):
    f32, bf16 = jnp.float32, jnp.bfloat16
    t, tile1, qi1 = _pipeline_tiles(n_tiles, nq)
    head1 = (tile1 // nq) % A_HEADS
    _fill_first_steps(t, s_odd, m_odd, acc_ref, sum_ref)

    def step(s1_ref, m1_ref, s2_ref, m2_ref):
        row = lax.broadcasted_iota(jnp.int32, (LANES, LANES), 0)
        slot = jnp.bitwise_and(row, BIAS_SLOTS - 1)
        term = jnp.right_shift(jnp.bitwise_and(row, A_QK_HEAD - 1), BIAS_SLOT_BITS)
        left = rb_ref[NUM_BUCKETS // 2 - 1, head1] * LOG2E
        right = rb_ref[NUM_BUCKETS - 1, head1] * LOG2E
        const = jnp.where(jnp.abs(slot - qi1) <= 1, 0.0, jnp.where(slot < qi1, left, right))
        t0 = const.astype(bf16).astype(f32)
        t1 = (const - t0).astype(bf16).astype(f32)
        t2 = (const - t0 - t1).astype(bf16).astype(f32)
        bias_tile = jnp.where(term == 0, t0, jnp.where(term == 1, t1,
                                                      jnp.where(term == 2, t2, 0.0)))
        bias_rows = jnp.concatenate([bias_tile] * (Q_TILE // LANES), axis=1)
        top = lax.broadcasted_iota(jnp.int32, (LANES, Q_TILE), 0) < A_QK_HEAD
        qt = q_ref[...].astype(f32)
        w_ref[0] = jnp.where(top, qt, bias_rows).astype(bf16)
        w_ref[1] = jnp.where(top, bias_rows, qt).astype(bf16)

        l_row = [jnp.sum(sum_ref[st], axis=0, keepdims=True) for st in range(2)]
        lp = lamp_ref[...]
        lam = (jnp.exp(jnp.sum(lp[0:1] * lp[1:2], axis=-1, keepdims=True))
               - jnp.exp(jnp.sum(lp[2:3] * lp[3:4], axis=-1, keepdims=True)) + lam_init)
        o = (acc_ref[0] / l_row[0] - lam * (acc_ref[1] / l_row[1])).T
        ms = jnp.mean(o * o, axis=-1, keepdims=True)
        o = (o * lax.rsqrt(ms + EPS) * g_ref[...]) * (1.0 - lam_init)
        o_ref[...] = (o * gate_ref[...].astype(f32)).astype(o_ref.dtype)

        m_row = [jnp.max(m2_ref[st], axis=0, keepdims=True) for st in range(2)]

        m_acc = [None, None]
        l_acc = [None, None]
        o_acc = [None, None]
        for i in range(n_chunks):
            c1 = qi1 - 1 + i
            j1 = jnp.bitwise_and(c1, n_chunks - 1)
            key0 = pl.multiple_of(j1 * K_CHUNK, K_CHUNK)
            if i < BAND_CHUNKS:
                in_range = jnp.logical_and(c1 >= 0, c1 < n_chunks)
                band = jnp.where(in_range, band_ref[0, i], 0.0)
            for st in range(2):
                s = jnp.dot(k_ref[pl.ds(key0, K_CHUNK), st * LANES:(st + 1) * LANES], w_ref[st],
                            preferred_element_type=f32)
                if i < BAND_CHUNKS:
                    s = s + band
                s1_ref[j1, st] = s
                cm = _fold_rows(s, jnp.maximum)
                m_acc[st] = cm if i == 0 else jnp.maximum(m_acc[st], cm)
            for st in range(2):
                p = jnp.exp2(s2_ref[i, st] - m_row[st])
                ps = _fold_rows(p, jnp.add)
                l_acc[st] = ps if i == 0 else l_acc[st] + ps
                part = jnp.dot(v_ref[:, i * K_CHUNK:(i + 1) * K_CHUNK], p.astype(bf16),
                               preferred_element_type=f32)
                o_acc[st] = part if i == 0 else o_acc[st] + part
        for st in range(2):
            m1_ref[st] = m_acc[st]
            acc_ref[st] = o_acc[st]
            sum_ref[st] = l_acc[st]

    _by_parity(t, step, s_even, m_even, s_odd, m_odd)


def _attn_call(kernel_fn, operands, prefix_specs, q_arr, k_arr, v_arr, gate_arr, band_arr,
               *, n_pairs, per_pair, batch, seq, name):
    m = gate_arr.shape[0]
    nq = seq // Q_TILE
    n_chunks = seq // K_CHUNK
    n_tiles = batch * n_pairs * nq
    k_lanes = k_arr.shape[1] // (n_pairs if per_pair else 1)
    v_rows = LANES if per_pair else B_HEAD

    def decode(tile):
        return tile // (n_pairs * nq), (tile // nq) % n_pairs, tile % nq

    def stage1(t):
        return decode(jnp.minimum(t, n_tiles - 1))

    def stage2(t):
        return decode(jnp.clip(t - 1, 0, n_tiles - 1))

    def stage3(t):
        return decode(jnp.maximum(t - 2, 0))

    def q_map(t):
        b, p, i = stage1(t)
        return p, b * nq + i

    def k_map(t):
        b, p, _ = stage1(t)
        return b, p if per_pair else 0

    def v_map(t):
        b, p, _ = stage2(t)
        return p if per_pair else 0, b

    def out_map(t):
        b, p, i = stage3(t)
        return b * nq + i, p

    in_specs = list(prefix_specs) + [
        pl.BlockSpec((LANES, Q_TILE), q_map),
        pl.BlockSpec((seq, k_lanes), k_map),
        pl.BlockSpec((LANES, seq), v_map),
        pl.BlockSpec((Q_TILE, LANES), out_map)]
    tail = ()
    if band_arr is not None:
        in_specs.append(pl.BlockSpec((1, BAND_CHUNKS, K_CHUNK, Q_TILE),
                                     lambda t: (stage1(t)[1], 0, 0, 0)))
        tail = (band_arr,)
    scores = pltpu.VMEM((n_chunks, 2, K_CHUNK, Q_TILE), jnp.float32)
    per_key_group = pltpu.VMEM((2, SUBLANES, Q_TILE), jnp.float32)
    scratch = [pltpu.VMEM((2, LANES, Q_TILE), jnp.bfloat16),
               scores, scores,
               per_key_group, per_key_group,
               pltpu.VMEM((2, v_rows, Q_TILE), jnp.float32),
               per_key_group]
    return pl.pallas_call(
        functools.partial(kernel_fn, n_chunks=n_chunks, nq=nq, n_tiles=n_tiles),
        grid=(n_tiles + 2,),
        in_specs=in_specs,
        out_specs=pl.BlockSpec((Q_TILE, LANES), out_map),
        out_shape=jax.ShapeDtypeStruct((m, n_pairs * LANES), jnp.bfloat16),
        scratch_shapes=scratch,
        compiler_params=_cparams(1),
        name=name,
    )(*operands, q_arr, k_arr, v_arr, gate_arr, *tail)


def _attn_diff_call(rel_bias, lam_params, subln_g, qa, ka, va, sa, band, lam_init, batch, seq,
                    layer):
    n_chunks = seq // K_CHUNK
    assert n_chunks <= BIAS_SLOTS and n_chunks & (n_chunks - 1) == 0
    prefix = [pl.BlockSpec(memory_space=pltpu.SMEM),
              pl.BlockSpec((None,) + lam_params.shape[1:], lambda t: (layer, 0, 0)),
              pl.BlockSpec((None, 1, A_V_HEAD), lambda t: (layer, 0, 0))]
    return _attn_call(functools.partial(_attn_diff_kernel, lam_init=lam_init),
                      (rel_bias, lam_params, subln_g), prefix, qa, ka, va, sa, band,
                      n_pairs=A_HEADS, per_pair=True, batch=batch, seq=seq, name="attn_diff")


def _attn_gqa_call(qb, kb, vb, sb, batch, seq):
    return _attn_call(_attn_gqa_kernel, (), [], qb, kb, vb, sb, None,
                      n_pairs=B_Q_HEADS // 2, per_pair=False, batch=batch, seq=seq,
                      name="attn_gqa")


def _out_kernel(ya_ref, yb_ref, wa_ref, wb_ref, x_ref, g_ref, o_ref):
    f32 = jnp.float32
    y = (jnp.dot(ya_ref[...], wa_ref[...], preferred_element_type=f32)
         + jnp.dot(yb_ref[...], wb_ref[...], preferred_element_type=f32))
    ms = jnp.mean(y * y, axis=-1, keepdims=True)
    o_ref[...] = x_ref[...] + y * lax.rsqrt(ms + EPS) * g_ref[...]


def _out_call(ya, yb, wa, wb, xf, g_post, layer):
    m = xf.shape[0]
    row = lambda i: (i, 0)
    return pl.pallas_call(
        _out_kernel,
        grid=(m // ROW_TILE,),
        in_specs=[pl.BlockSpec((ROW_TILE, A_WIDTH), row),
                  pl.BlockSpec((ROW_TILE, B_WIDTH), row),
                  pl.BlockSpec((None, A_WIDTH, D_MODEL), lambda i: (layer, 0, 0)),
                  pl.BlockSpec((None, B_WIDTH, D_MODEL), lambda i: (layer, 0, 0)),
                  pl.BlockSpec((ROW_TILE, D_MODEL), row),
                  pl.BlockSpec((None, 1, D_MODEL), lambda i: (layer, 0, 0))],
        out_specs=pl.BlockSpec((ROW_TILE, D_MODEL), row),
        out_shape=jax.ShapeDtypeStruct((m, D_MODEL), jnp.float32),
        compiler_params=_cparams(1),
        name="out_proj",
    )(ya, yb, wa, wb, xf, g_post)


def _layer_weights(w_in_l, w_out_l, q_norm_g_l, k_norm_g_l):
    bf16 = jnp.bfloat16
    d = w_in_l.shape[0]
    per_group = B_Q_HEADS // B_KV_HEADS
    half = B_HEAD // 2
    bq = w_in_l[:, _BQ:_BK].reshape(d, B_KV_HEADS, per_group, half, 2)
    bq = bq.transpose(0, 2, 1, 4, 3).reshape(d, B_WIDTH)
    bk = w_in_l[:, _BK:_BV].reshape(d, B_KV_HEADS, half, 2)
    bk = bk.transpose(0, 1, 3, 2).reshape(d, B_KV_HEADS * B_HEAD)
    bg = w_in_l[:, _BG:].reshape(d, B_KV_HEADS, per_group, B_HEAD)
    bg = bg.transpose(0, 2, 1, 3).reshape(d, B_WIDTH)
    wn = jnp.concatenate([w_in_l[:, _AK:_AV], w_in_l[:, _AG:_BQ], bg], axis=1).astype(bf16)
    wt = jnp.concatenate([w_in_l[:, _AQ:_AK], w_in_l[:, _AV:_AG], bq, bk, w_in_l[:, _BV:_BG]],
                         axis=1).astype(bf16).T
    wa = w_out_l[:A_WIDTH].astype(bf16)
    wb = w_out_l[A_WIDTH:].reshape(B_KV_HEADS, per_group, B_HEAD, -1)
    wb = wb.transpose(1, 0, 2, 3).reshape(B_WIDTH, -1).astype(bf16)
    qg = q_norm_g_l.reshape(half, 2).T.reshape(B_HEAD, 1)
    kg = k_norm_g_l.reshape(half, 2).T.reshape(B_HEAD, 1)
    return wn, wt, wa, wb, qg, kg


def kernel(x, rel_bias, pre_norm_g, w_in, diff_lambda, diff_subln_g, q_norm_g, k_norm_g,
           w_out, post_norm_g):
    batch, seq, d_model = x.shape
    xf = x.reshape(batch * seq, d_model)
    band = _band_call(rel_bias)
    cos_t, sin_t = _rope_tables_t(seq)
    wn, wt, wa, wb, qg, kg = jax.vmap(_layer_weights)(w_in, w_out, q_norm_g, k_norm_g)
    xf, band, cos_t, sin_t, wn, wt, wa, wb, qg, kg = lax.optimization_barrier(
        (xf, band, cos_t, sin_t, wn, wt, wa, wb, qg, kg))
    g_pre = pre_norm_g.reshape(DEPTH, 1, d_model)
    g_post = post_norm_g.reshape(DEPTH, 1, d_model)
    g_subln = diff_subln_g.reshape(DEPTH, 1, A_V_HEAD)
    for l in range(DEPTH):
        lam_init = 0.8 - 0.6 * math.exp(-0.3 * l)
        qa, ka, va, sa, qb, kb, vb, sb = _proj_call(
            xf, g_pre, wn, wt, cos_t, sin_t, qg, kg, seq, l)
        ya = _attn_diff_call(rel_bias, diff_lambda, g_subln, qa, ka, va, sa, band, lam_init,
                             batch, seq, l)
        yb = _attn_gqa_call(qb, kb, vb, sb, batch, seq)
        xf = _out_call(ya, yb, wa, wb, xf, g_post, l)
    return xf.reshape(batch, seq, d_model)
```

```python
import functools
import math

import jax
import jax.numpy as jnp
from jax import lax
from jax.experimental import pallas as pl
from jax.experimental.pallas import tpu as pltpu

D_MODEL = 1024
DEPTH = 2
A_WIDTH = 512
B_WIDTH = 512
A_HEADS = 4
A_V_HEAD = 128
A_QK_HEAD = 64
B_HEAD = 64
B_Q_HEADS = 8
B_KV_HEADS = 2
GRID_W = 64
ROPE_THETA = 10000.0
NUM_BUCKETS = 32
MAX_DISTANCE = 128
EPS = 1e-6
LOG2E = math.log2(math.e)
Q_SCALE = (A_QK_HEAD ** -0.5) * LOG2E

LANES = 128
SUBLANES = 8
ROW_TILE = 1024
Q_TILE = 256
K_CHUNK = 256
BAND_CHUNKS = 3
BIAS_SLOTS = 16
BIAS_SLOT_BITS = BIAS_SLOTS.bit_length() - 1
BIAS_TERMS = 3
VMEM_LIMIT = 48 * 1024 * 1024
VMEM_LIMIT_ALL = 62 * 1024 * 1024

_AQ, _AK, _AV, _AG, _BQ, _BK, _BV, _BG = 0, 512, 1024, 1536, 2048, 2560, 2688, 2816
_T_AQ, _T_AV, _T_BQ, _T_BK, _T_BV, _T_END = 0, 512, 1024, 1536, 1664, 1792
_N_AK, _N_AG, _N_BG, _N_END = 0, 512, 1024, 1536


def _t5_bucket(rel):
    nb = NUM_BUCKETS // 2
    max_exact = nb // 2
    n = jnp.abs(rel)
    large = max_exact + (jnp.log(jnp.maximum(n, 1).astype(jnp.float32) / max_exact)
                         / math.log(MAX_DISTANCE / max_exact) * (nb - max_exact)).astype(jnp.int32)
    large = jnp.minimum(large, nb - 1)
    return jnp.where(rel > 0, nb, 0) + jnp.where(n < max_exact, n, large)


def _rope_tables_t(n):
    rows = n // GRID_W
    axis_dim = B_HEAD // 2
    inv = ROPE_THETA ** (-jnp.arange(0, axis_dim, 2, dtype=jnp.float32) / axis_dim)
    row_ang = (jnp.arange(rows, dtype=jnp.float32)[:, None] * inv).T
    col_ang = (jnp.arange(GRID_W, dtype=jnp.float32)[:, None] * inv).T

    def expand(fn):
        by_row = jnp.repeat(fn(row_ang), GRID_W, axis=1)
        by_col = jnp.tile(fn(col_ang), (1, rows))
        return jnp.concatenate([by_row, by_col], axis=0)

    return expand(jnp.cos), expand(jnp.sin)


def _cparams(n_axes, vmem_limit=VMEM_LIMIT):
    return pltpu.CompilerParams(dimension_semantics=("arbitrary",) * n_axes,
                                vmem_limit_bytes=vmem_limit)


def _bias_slot_mask(slot_index, chunk):
    term = jnp.right_shift(jnp.bitwise_and(slot_index, A_QK_HEAD - 1), BIAS_SLOT_BITS)
    in_terms = term < BIAS_TERMS
    return jnp.logical_and(in_terms, jnp.bitwise_and(slot_index, BIAS_SLOTS - 1) == chunk)


BAND_TABLE = 2 * Q_TILE


def _band_kernel(rb_ref, idx_ref, out_ref):
    h = pl.program_id(0)
    for d in range(BAND_CHUNKS):
        idx = jnp.broadcast_to(idx_ref[d], (SUBLANES, BAND_TABLE))
        vals = jnp.zeros(idx.shape, jnp.float32)
        for b in range(NUM_BUCKETS):
            vals = jnp.where(idx == b, rb_ref[b, h] * LOG2E, vals)

        for j0 in range(0, K_CHUNK, SUBLANES):
            rolled = pltpu.roll(vals, (j0 + Q_TILE + 1) % BAND_TABLE, 1, stride=1, stride_axis=0)
            out_ref[0, d, j0:j0 + SUBLANES, :] = rolled[:, :Q_TILE]


def _band_call(rel_bias):
    d = jnp.arange(BAND_CHUNKS, dtype=jnp.int32)[:, None, None]
    c = jnp.arange(BAND_TABLE, dtype=jnp.int32)[None, None, :]
    idx = _t5_bucket((d - 1) * K_CHUNK + (Q_TILE - 1) - c)
    idx = jnp.take(jnp.arange(NUM_BUCKETS, dtype=jnp.int32), idx)
    return pl.pallas_call(
        _band_kernel,
        grid=(A_HEADS,),
        in_specs=[pl.BlockSpec(memory_space=pltpu.SMEM),
                  pl.BlockSpec((BAND_CHUNKS, 1, BAND_TABLE), lambda h: (0, 0, 0))],
        out_specs=pl.BlockSpec((1, BAND_CHUNKS, K_CHUNK, Q_TILE), lambda h: (h, 0, 0, 0)),
        out_shape=jax.ShapeDtypeStruct((A_HEADS, BAND_CHUNKS, K_CHUNK, Q_TILE), jnp.float32),
        compiler_params=_cparams(1),
        name="bias_band",
    )(rel_bias, idx)


def _proj_kernel(x_ref, g_ref, wn_ref, wt_ref, cos_ref, sin_ref, qg_ref, kg_ref,
                 qa_ref, ka_ref, va_ref, sa_ref, qb_ref, kb_ref, vb_ref, sb_ref,
                 *, tiles_per_seq):
    f32, bf16 = jnp.float32, jnp.bfloat16
    x = x_ref[...]
    ms = jnp.mean(x * x, axis=-1, keepdims=True)
    h = (x * lax.rsqrt(ms + EPS) * g_ref[...]).astype(bf16)

    def nat(lo, hi):
        return jnp.dot(h, wn_ref[:, lo:hi], preferred_element_type=f32)

    def silu(v):
        return v * jax.nn.sigmoid(v)

    pt = lax.dot_general(wt_ref[...], h, (((1,), (1,)), ((), ())), preferred_element_type=f32)
    qa_ref[...] = (pt[_T_AQ:_T_AV] * Q_SCALE).astype(bf16)
    va_ref[...] = pt[_T_AV:_T_BQ].astype(bf16)
    vb_ref[...] = pt[_T_BV:_T_END].astype(bf16)

    cos = cos_ref[...]
    sin = sin_ref[...]
    half = B_HEAD // 2

    def norm_rope(xt, g):
        msq = jnp.mean(xt * xt, axis=0, keepdims=True)
        y = xt * lax.rsqrt(msq + EPS) * g
        e, o = y[:half], y[half:]
        return jnp.concatenate([e * cos - o * sin, e * sin + o * cos], axis=0)

    qg = qg_ref[...]
    qn = jnp.concatenate([norm_rope(pt[_T_BQ + i * B_HEAD:_T_BQ + (i + 1) * B_HEAD], qg)
                          for i in range(B_Q_HEADS)], axis=0) * Q_SCALE
    qb_ref[...] = qn.astype(bf16)
    kg = kg_ref[...]
    kn = jnp.concatenate([norm_rope(pt[_T_BK + i * B_HEAD:_T_BK + (i + 1) * B_HEAD], kg)
                          for i in range(B_KV_HEADS)], axis=0)
    kb_ref[...] = kn.T.astype(bf16)

    sa_ref[...] = silu(nat(_N_AG, _N_BG)).astype(bf16)
    sb_ref[...] = silu(nat(_N_BG, _N_END)).astype(bf16)

    ak = nat(_N_AK, _N_AG)
    lane = lax.broadcasted_iota(jnp.int32, (ROW_TILE, LANES), 1)
    row = lax.broadcasted_iota(jnp.int32, (ROW_TILE, LANES), 0)
    chunk = ((pl.program_id(0) % tiles_per_seq) * (ROW_TILE // K_CHUNK)
             + jnp.right_shift(row, K_CHUNK.bit_length() - 1))
    hot = jnp.where(_bias_slot_mask(lane, chunk), 1.0, 0.0)
    low = lane < A_QK_HEAD
    for hd in range(A_HEADS):
        akh = ak[:, hd * LANES:(hd + 1) * LANES]
        ka_ref[:, 2 * hd * LANES:(2 * hd + 1) * LANES] = jnp.where(low, akh, hot).astype(bf16)
        ka_ref[:, (2 * hd + 1) * LANES:(2 * hd + 2) * LANES] = jnp.where(low, hot, akh).astype(bf16)


def _proj_call(xf, g_pre, wn, wt, cos_t, sin_t, qg, kg, seq, layer):
    m = xf.shape[0]
    tiles_per_seq = seq // ROW_TILE
    bf16 = jnp.bfloat16
    row = lambda i: (i, 0)
    col = lambda i: (0, i)
    of_layer = lambda i: (layer, 0, 0)
    pos = lambda i: (0, i % tiles_per_seq)
    return pl.pallas_call(
        functools.partial(_proj_kernel, tiles_per_seq=tiles_per_seq),
        grid=(m // ROW_TILE,),
        in_specs=[pl.BlockSpec((ROW_TILE, D_MODEL), row),
                  pl.BlockSpec((None, 1, D_MODEL), of_layer),
                  pl.BlockSpec((None,) + wn.shape[1:], of_layer),
                  pl.BlockSpec((None,) + wt.shape[1:], of_layer),
                  pl.BlockSpec((B_HEAD // 2, ROW_TILE), pos),
                  pl.BlockSpec((B_HEAD // 2, ROW_TILE), pos),
                  pl.BlockSpec((None, B_HEAD, 1), of_layer),
                  pl.BlockSpec((None, B_HEAD, 1), of_layer)],
        out_specs=[pl.BlockSpec((A_WIDTH, ROW_TILE), col),
                   pl.BlockSpec((ROW_TILE, 2 * A_WIDTH), row),
                   pl.BlockSpec((A_WIDTH, ROW_TILE), col),
                   pl.BlockSpec((ROW_TILE, A_WIDTH), row),
                   pl.BlockSpec((B_WIDTH, ROW_TILE), col),
                   pl.BlockSpec((ROW_TILE, B_KV_HEADS * B_HEAD), row),
                   pl.BlockSpec((B_KV_HEADS * B_HEAD, ROW_TILE), col),
                   pl.BlockSpec((ROW_TILE, B_WIDTH), row)],
        out_shape=[jax.ShapeDtypeStruct((A_WIDTH, m), bf16),
                   jax.ShapeDtypeStruct((m, 2 * A_WIDTH), bf16),
                   jax.ShapeDtypeStruct((A_WIDTH, m), bf16),
                   jax.ShapeDtypeStruct((m, A_WIDTH), bf16),
                   jax.ShapeDtypeStruct((B_WIDTH, m), bf16),
                   jax.ShapeDtypeStruct((m, B_KV_HEADS * B_HEAD), bf16),
                   jax.ShapeDtypeStruct((B_KV_HEADS * B_HEAD, m), bf16),
                   jax.ShapeDtypeStruct((m, B_WIDTH), bf16)],
        compiler_params=_cparams(1),
        name="in_proj",
    )(xf, g_pre, wn, wt, cos_t, sin_t, qg, kg)


def _fold_rows(x, op):
    parts = [x[g * SUBLANES:(g + 1) * SUBLANES] for g in range(x.shape[0] // SUBLANES)]
    while len(parts) > 1:
        parts = [op(parts[i], parts[i + 1]) for i in range(0, len(parts), 2)]
    return parts[0]


def _pipeline_tiles(n_tiles, nq):
    t = pl.program_id(0)
    tile1 = jnp.minimum(t, n_tiles - 1)
    return t, tile1, tile1 % nq


def _fill_first_steps(t, s_odd, m_odd, acc_ref, sum_ref):
    @pl.when(t == 0)
    def _():
        s_odd[...] = jnp.zeros(s_odd.shape, jnp.float32)
        m_odd[...] = jnp.zeros(m_odd.shape, jnp.float32)
        acc_ref[...] = jnp.zeros(acc_ref.shape, jnp.float32)
        sum_ref[...] = jnp.ones(sum_ref.shape, jnp.float32)


def _by_parity(t, step, s_even, m_even, s_odd, m_odd):
    @pl.when(t % 2 == 0)
    def _():
        step(s_even, m_even, s_odd, m_odd)

    @pl.when(t % 2 == 1)
    def _():
        step(s_odd, m_odd, s_even, m_even)


def _attn_gqa_kernel(q_ref, k_ref, v_ref, gate_ref, o_ref,
                     w_ref, s_even, s_odd, m_even, m_odd, acc_ref, sum_ref,
                     *, n_chunks, nq, n_tiles):
    f32, bf16 = jnp.float32, jnp.bfloat16
    t, _, _ = _pipeline_tiles(n_tiles, nq)
    _fill_first_steps(t, s_odd, m_odd, acc_ref, sum_ref)

    def step(s1_ref, m1_ref, s2_ref, m2_ref):
        top = lax.broadcasted_iota(jnp.int32, (LANES, Q_TILE), 0) < B_HEAD
        qt = q_ref[...].astype(f32)
        w_ref[0] = jnp.where(top, qt, 0.0).astype(bf16)
        w_ref[1] = jnp.where(top, 0.0, qt).astype(bf16)

        ot = jnp.concatenate([acc_ref[st] / jnp.sum(sum_ref[st], axis=0, keepdims=True)
                              for st in range(2)], axis=0)
        o_ref[...] = (ot.T * gate_ref[...].astype(f32)).astype(o_ref.dtype)

        m_row = [jnp.max(m2_ref[st], axis=0, keepdims=True) for st in range(2)]

        m_acc = [None, None]
        l_acc = [None, None]
        o_acc = [None, None]
        for c in range(n_chunks):
            keys = slice(c * K_CHUNK, (c + 1) * K_CHUNK)
            for st in range(2):
                s = jnp.dot(k_ref[keys, :], w_ref[st], preferred_element_type=f32)
                s1_ref[c, st] = s
                cm = _fold_rows(s, jnp.maximum)
                m_acc[st] = cm if c == 0 else jnp.maximum(m_acc[st], cm)
            for st in range(2):
                p = jnp.exp2(s2_ref[c, st] - m_row[st])
                ps = _fold_rows(p, jnp.add)
                l_acc[st] = ps if c == 0 else l_acc[st] + ps
                part = jnp.dot(v_ref[st * B_HEAD:(st + 1) * B_HEAD, keys], p.astype(bf16),
                               preferred_element_type=f32)
                o_acc[st] = part if c == 0 else o_acc[st] + part
        for st in range(2):
            m1_ref[st] = m_acc[st]
            acc_ref[st] = o_acc[st]
            sum_ref[st] = l_acc[st]

    _by_parity(t, step, s_even, m_even, s_odd, m_odd)


def _attn_diff_kernel(rb_ref, lamp_ref, g_ref, q_ref, k_ref, v_ref, gate_ref, band_ref, o_ref,
                      w_ref, s_even, s_odd, m_even, m_odd, acc_ref, sum_ref,
                      *, lam_init, n_chunks, nq, n_tiles):
    f32, bf16 = jnp.float32, jnp.bfloat16
    t, tile1, qi1 = _pipeline_tiles(n_tiles, nq)
    head1 = (tile1 // nq) % A_HEADS
    _fill_first_steps(t, s_odd, m_odd, acc_ref, sum_ref)

    def step(s1_ref, m1_ref, s2_ref, m2_ref):
        row = lax.broadcasted_iota(jnp.int32, (LANES, LANES), 0)
        slot = jnp.bitwise_and(row, BIAS_SLOTS - 1)
        term = jnp.right_shift(jnp.bitwise_and(row, A_QK_HEAD - 1), BIAS_SLOT_BITS)
        left = rb_ref[NUM_BUCKETS // 2 - 1, head1] * LOG2E
        right = rb_ref[NUM_BUCKETS - 1, head1] * LOG2E
        const = jnp.where(jnp.abs(slot - qi1) <= 1, 0.0, jnp.where(slot < qi1, left, right))
        t0 = const.astype(bf16).astype(f32)
        t1 = (const - t0).astype(bf16).astype(f32)
        t2 = (const - t0 - t1).astype(bf16).astype(f32)
        bias_tile = jnp.where(term == 0, t0, jnp.where(term == 1, t1,
                                                      jnp.where(term == 2, t2, 0.0)))
        bias_rows = jnp.concatenate([bias_tile] * (Q_TILE // LANES), axis=1)
        top = lax.broadcasted_iota(jnp.int32, (LANES, Q_TILE), 0) < A_QK_HEAD
        qt = q_ref[...].astype(f32)
        w_ref[0] = jnp.where(top, qt, bias_rows).astype(bf16)
        w_ref[1] = jnp.where(top, bias_rows, qt).astype(bf16)

        l_row = [jnp.sum(sum_ref[st], axis=0, keepdims=True) for st in range(2)]
        lp = lamp_ref[...]
        lam = (jnp.exp(jnp.sum(lp[0:1] * lp[1:2], axis=-1, keepdims=True))
               - jnp.exp(jnp.sum(lp[2:3] * lp[3:4], axis=-1, keepdims=True)) + lam_init)
        o = (acc_ref[0] / l_row[0] - lam * (acc_ref[1] / l_row[1])).T
        ms = jnp.mean(o * o, axis=-1, keepdims=True)
        o = (o * lax.rsqrt(ms + EPS) * g_ref[...]) * (1.0 - lam_init)
        o_ref[...] = (o * gate_ref[...].astype(f32)).astype(o_ref.dtype)

        m_row = [jnp.max(m2_ref[st], axis=0, keepdims=True) for st in range(2)]

        m_acc = [None, None]
        l_acc = [None, None]
        o_acc = [None, None]
        for i in range(n_chunks):
            c1 = qi1 - 1 + i
            j1 = jnp.bitwise_and(c1, n_chunks - 1)
            key0 = pl.multiple_of(j1 * K_CHUNK, K_CHUNK)
            if i < BAND_CHUNKS:
                in_range = jnp.logical_and(c1 >= 0, c1 < n_chunks)
                band = jnp.where(in_range, band_ref[0, i], 0.0)
            for st in range(2):
                s = jnp.dot(k_ref[pl.ds(key0, K_CHUNK), st * LANES:(st + 1) * LANES], w_ref[st],
                            preferred_element_type=f32)
                if i < BAND_CHUNKS:
                    s = s + band
                s1_ref[j1, st] = s
                cm = _fold_rows(s, jnp.maximum)
                m_acc[st] = cm if i == 0 else jnp.maximum(m_acc[st], cm)
            for st in range(2):
                p = jnp.exp2(s2_ref[i, st] - m_row[st])
                ps = _fold_rows(p, jnp.add)
                l_acc[st] = ps if i == 0 else l_acc[st] + ps
                part = jnp.dot(v_ref[:, i * K_CHUNK:(i + 1) * K_CHUNK], p.astype(bf16),
                               preferred_element_type=f32)
                o_acc[st] = part if i == 0 else o_acc[st] + part
        for st in range(2):
            m1_ref[st] = m_acc[st]
            acc_ref[st] = o_acc[st]
            sum_ref[st] = l_acc[st]

    _by_parity(t, step, s_even, m_even, s_odd, m_odd)


def _attn_call(kernel_fn, operands, prefix_specs, q_arr, k_arr, v_arr, gate_arr, band_arr,
               *, n_pairs, per_pair, batch, seq, name):
    m = gate_arr.shape[0]
    nq = seq // Q_TILE
    n_chunks = seq // K_CHUNK
    n_tiles = batch * n_pairs * nq
    k_lanes = k_arr.shape[1] // (n_pairs if per_pair else 1)
    v_rows = LANES if per_pair else B_HEAD

    def decode(tile):
        return tile // (n_pairs * nq), (tile // nq) % n_pairs, tile % nq

    def stage1(t):
        return decode(jnp.minimum(t, n_tiles - 1))

    def stage2(t):
        return decode(jnp.clip(t - 1, 0, n_tiles - 1))

    def stage3(t):
        return decode(jnp.maximum(t - 2, 0))

    def q_map(t):
        b, p, i = stage1(t)
        return p, b * nq + i

    def k_map(t):
        b, p, _ = stage1(t)
        return b, p if per_pair else 0

    def v_map(t):
        b, p, _ = stage2(t)
        return p if per_pair else 0, b

    def out_map(t):
        b, p, i = stage3(t)
        return b * nq + i, p

    in_specs = list(prefix_specs) + [
        pl.BlockSpec((LANES, Q_TILE), q_map),
        pl.BlockSpec((seq, k_lanes), k_map),
        pl.BlockSpec((LANES, seq), v_map),
        pl.BlockSpec((Q_TILE, LANES), out_map)]
    tail = ()
    if band_arr is not None:
        in_specs.append(pl.BlockSpec((1, BAND_CHUNKS, K_CHUNK, Q_TILE),
                                     lambda t: (stage1(t)[1], 0, 0, 0)))
        tail = (band_arr,)
    scores = pltpu.VMEM((n_chunks, 2, K_CHUNK, Q_TILE), jnp.float32)
    per_key_group = pltpu.VMEM((2, SUBLANES, Q_TILE), jnp.float32)
    scratch = [pltpu.VMEM((2, LANES, Q_TILE), jnp.bfloat16),
               scores, scores,
               per_key_group, per_key_group,
               pltpu.VMEM((2, v_rows, Q_TILE), jnp.float32),
               per_key_group]
    return pl.pallas_call(
        functools.partial(kernel_fn, n_chunks=n_chunks, nq=nq, n_tiles=n_tiles),
        grid=(n_tiles + 2,),
        in_specs=in_specs,
        out_specs=pl.BlockSpec((Q_TILE, LANES), out_map),
        out_shape=jax.ShapeDtypeStruct((m, n_pairs * LANES), jnp.bfloat16),
        scratch_shapes=scratch,
        compiler_params=_cparams(1, VMEM_LIMIT if per_pair else VMEM_LIMIT_ALL),
        name=name,
    )(*operands, q_arr, k_arr, v_arr, gate_arr, *tail)


def _attn_diff_call(rel_bias, lam_params, subln_g, qa, ka, va, sa, band, lam_init, batch, seq,
                    layer):
    n_chunks = seq // K_CHUNK
    assert n_chunks <= BIAS_SLOTS and n_chunks & (n_chunks - 1) == 0
    prefix = [pl.BlockSpec(memory_space=pltpu.SMEM),
              pl.BlockSpec((None,) + lam_params.shape[1:], lambda t: (layer, 0, 0)),
              pl.BlockSpec((None, 1, A_V_HEAD), lambda t: (layer, 0, 0))]
    return _attn_call(functools.partial(_attn_diff_kernel, lam_init=lam_init),
                      (rel_bias, lam_params, subln_g), prefix, qa, ka, va, sa, band,
                      n_pairs=A_HEADS, per_pair=True, batch=batch, seq=seq, name="attn_diff")


def _attn_gqa_call(qb, kb, vb, sb, batch, seq):
    return _attn_call(_attn_gqa_kernel, (), [], qb, kb, vb, sb, None,
                      n_pairs=B_Q_HEADS // 2, per_pair=False, batch=batch, seq=seq,
                      name="attn_gqa")


def _out_kernel(ya_ref, yb_ref, wa_ref, wb_ref, x_ref, g_ref, o_ref):
    f32 = jnp.float32
    y = (jnp.dot(ya_ref[...], wa_ref[...], preferred_element_type=f32)
         + jnp.dot(yb_ref[...], wb_ref[...], preferred_element_type=f32))
    ms = jnp.mean(y * y, axis=-1, keepdims=True)
    o_ref[...] = x_ref[...] + y * lax.rsqrt(ms + EPS) * g_ref[...]


def _out_call(ya, yb, wa, wb, xf, g_post, layer):
    m = xf.shape[0]
    row = lambda i: (i, 0)
    return pl.pallas_call(
        _out_kernel,
        grid=(m // ROW_TILE,),
        in_specs=[pl.BlockSpec((ROW_TILE, A_WIDTH), row),
                  pl.BlockSpec((ROW_TILE, B_WIDTH), row),
                  pl.BlockSpec((None, A_WIDTH, D_MODEL), lambda i: (layer, 0, 0)),
                  pl.BlockSpec((None, B_WIDTH, D_MODEL), lambda i: (layer, 0, 0)),
                  pl.BlockSpec((ROW_TILE, D_MODEL), row),
                  pl.BlockSpec((None, 1, D_MODEL), lambda i: (layer, 0, 0))],
        out_specs=pl.BlockSpec((ROW_TILE, D_MODEL), row),
        out_shape=jax.ShapeDtypeStruct((m, D_MODEL), jnp.float32),
        compiler_params=_cparams(1),
        name="out_proj",
    )(ya, yb, wa, wb, xf, g_post)


def _layer_weights(w_in_l, w_out_l, q_norm_g_l, k_norm_g_l):
    bf16 = jnp.bfloat16
    d = w_in_l.shape[0]
    per_group = B_Q_HEADS // B_KV_HEADS
    half = B_HEAD // 2
    bq = w_in_l[:, _BQ:_BK].reshape(d, B_KV_HEADS, per_group, half, 2)
    bq = bq.transpose(0, 2, 1, 4, 3).reshape(d, B_WIDTH)
    bk = w_in_l[:, _BK:_BV].reshape(d, B_KV_HEADS, half, 2)
    bk = bk.transpose(0, 1, 3, 2).reshape(d, B_KV_HEADS * B_HEAD)
    bg = w_in_l[:, _BG:].reshape(d, B_KV_HEADS, per_group, B_HEAD)
    bg = bg.transpose(0, 2, 1, 3).reshape(d, B_WIDTH)
    wn = jnp.concatenate([w_in_l[:, _AK:_AV], w_in_l[:, _AG:_BQ], bg], axis=1).astype(bf16)
    wt = jnp.concatenate([w_in_l[:, _AQ:_AK], w_in_l[:, _AV:_AG], bq, bk, w_in_l[:, _BV:_BG]],
                         axis=1).astype(bf16).T
    wa = w_out_l[:A_WIDTH].astype(bf16)
    wb = w_out_l[A_WIDTH:].reshape(B_KV_HEADS, per_group, B_HEAD, -1)
    wb = wb.transpose(1, 0, 2, 3).reshape(B_WIDTH, -1).astype(bf16)
    qg = q_norm_g_l.reshape(half, 2).T.reshape(B_HEAD, 1)
    kg = k_norm_g_l.reshape(half, 2).T.reshape(B_HEAD, 1)
    return wn, wt, wa, wb, qg, kg


def kernel(x, rel_bias, pre_norm_g, w_in, diff_lambda, diff_subln_g, q_norm_g, k_norm_g,
           w_out, post_norm_g):
    batch, seq, d_model = x.shape
    xf = x.reshape(batch * seq, d_model)
    band = _band_call(rel_bias)
    cos_t, sin_t = _rope_tables_t(seq)
    wn, wt, wa, wb, qg, kg = jax.vmap(_layer_weights)(w_in, w_out, q_norm_g, k_norm_g)
    xf, band, cos_t, sin_t, wn, wt, wa, wb, qg, kg = lax.optimization_barrier(
        (xf, band, cos_t, sin_t, wn, wt, wa, wb, qg, kg))
    g_pre = pre_norm_g.reshape(DEPTH, 1, d_model)
    g_post = post_norm_g.reshape(DEPTH, 1, d_model)
    g_subln = diff_subln_g.reshape(DEPTH, 1, A_V_HEAD)
    for l in range(DEPTH):
        lam_init = 0.8 - 0.6 * math.exp(-0.3 * l)
        qa, ka, va, sa, qb, kb, vb, sb = _proj_call(
            xf, g_pre, wn, wt, cos_t, sin_t, qg, kg, seq, l)
        ya = _attn_diff_call(rel_bias, diff_lambda, g_subln, qa, ka, va, sa, band, lam_init,
                             batch, seq, l)
        yb = _attn_gqa_call(qb, kb, vb, sb, batch, seq)
        xf = _out_call(ya, yb, wa, wb, xf, g_post, l)
    return xf.reshape(batch, seq, d_model)
```

```python
import functools
import math

import jax
import jax.numpy as jnp
from jax import lax
from jax.experimental import pallas as pl
from jax.experimental.pallas import tpu as pltpu

D_MODEL = 1024
DEPTH = 2
A_WIDTH = 512
B_WIDTH = 512
A_HEADS = 4
A_V_HEAD = 128
A_QK_HEAD = 64
B_HEAD = 64
B_Q_HEADS = 8
B_KV_HEADS = 2
GRID_W = 64
ROPE_THETA = 10000.0
NUM_BUCKETS = 32
MAX_DISTANCE = 128
EPS = 1e-6
LOG2E = math.log2(math.e)
Q_SCALE = (A_QK_HEAD ** -0.5) * LOG2E

LANES = 128
SUBLANES = 8
ROW_TILE = 1024
Q_TILE = 256
K_CHUNK = 256
BAND_CHUNKS = 3
BIAS_SLOTS = 16
BIAS_TERMS = 3
VMEM_LIMIT = 48 * 1024 * 1024

_AQ, _AK, _AV, _AG, _BQ, _BK, _BV, _BG = 0, 512, 1024, 1536, 2048, 2560, 2688, 2816
_T_AQ, _T_AV, _T_BQ, _T_BK, _T_BV, _T_END = 0, 512, 1024, 1536, 1664, 1792
_N_AK, _N_AG, _N_BG, _N_END = 0, 512, 1024, 1536


def _t5_bucket(rel):
    nb = NUM_BUCKETS // 2
    max_exact = nb // 2
    n = jnp.abs(rel)
    large = max_exact + (jnp.log(jnp.maximum(n, 1).astype(jnp.float32) / max_exact)
                         / math.log(MAX_DISTANCE / max_exact) * (nb - max_exact)).astype(jnp.int32)
    large = jnp.minimum(large, nb - 1)
    return jnp.where(rel > 0, nb, 0) + jnp.where(n < max_exact, n, large)


def _rope_tables_t(n):
    rows = n // GRID_W
    axis_dim = B_HEAD // 2
    inv = ROPE_THETA ** (-jnp.arange(0, axis_dim, 2, dtype=jnp.float32) / axis_dim)
    row_ang = (jnp.arange(rows, dtype=jnp.float32)[:, None] * inv).T
    col_ang = (jnp.arange(GRID_W, dtype=jnp.float32)[:, None] * inv).T

    def expand(fn):
        by_row = jnp.repeat(fn(row_ang), GRID_W, axis=1)
        by_col = jnp.tile(fn(col_ang), (1, rows))
        return jnp.concatenate([by_row, by_col], axis=0)

    return expand(jnp.cos), expand(jnp.sin)


def _cparams(n_axes):
    return pltpu.CompilerParams(dimension_semantics=("arbitrary",) * n_axes,
                                vmem_limit_bytes=VMEM_LIMIT)


def _bias_slot_mask(slot_index, chunk):
    in_terms = jnp.right_shift(jnp.bitwise_and(slot_index, A_QK_HEAD - 1), 4) < BIAS_TERMS
    return jnp.logical_and(in_terms, jnp.bitwise_and(slot_index, BIAS_SLOTS - 1) == chunk)


BAND_TABLE = 2 * Q_TILE


def _band_kernel(rb_ref, idx_ref, out_ref):
    h = pl.program_id(0)
    for d in range(BAND_CHUNKS):
        idx = jnp.broadcast_to(idx_ref[d], (SUBLANES, BAND_TABLE))
        vals = jnp.zeros(idx.shape, jnp.float32)
        for b in range(NUM_BUCKETS):
            vals = jnp.where(idx == b, rb_ref[b, h] * LOG2E, vals)

        for j0 in range(0, K_CHUNK, SUBLANES):
            rolled = pltpu.roll(vals, (j0 + Q_TILE + 1) % BAND_TABLE, 1, stride=1, stride_axis=0)
            out_ref[0, d, j0:j0 + SUBLANES, :] = rolled[:, :Q_TILE]


def _band_call(rel_bias):
    d = jnp.arange(BAND_CHUNKS, dtype=jnp.int32)[:, None, None]
    c = jnp.arange(BAND_TABLE, dtype=jnp.int32)[None, None, :]
    idx = _t5_bucket((d - 1) * K_CHUNK + (Q_TILE - 1) - c)
    idx = jnp.take(jnp.arange(NUM_BUCKETS, dtype=jnp.int32), idx)
    return pl.pallas_call(
        _band_kernel,
        grid=(A_HEADS,),
        in_specs=[pl.BlockSpec(memory_space=pltpu.SMEM),
                  pl.BlockSpec((BAND_CHUNKS, 1, BAND_TABLE), lambda h: (0, 0, 0))],
        out_specs=pl.BlockSpec((1, BAND_CHUNKS, K_CHUNK, Q_TILE), lambda h: (h, 0, 0, 0)),
        out_shape=jax.ShapeDtypeStruct((A_HEADS, BAND_CHUNKS, K_CHUNK, Q_TILE), jnp.float32),
        compiler_params=_cparams(1),
        name="bias_band",
    )(rel_bias, idx)


def _proj_kernel(x_ref, g_ref, wn_ref, wt_ref, cos_ref, sin_ref, qg_ref, kg_ref,
                 qa_ref, ka_ref, va_ref, sa_ref, qb_ref, kb_ref, vb_ref, sb_ref,
                 *, tiles_per_seq):
    f32, bf16 = jnp.float32, jnp.bfloat16
    x = x_ref[...]
    ms = jnp.mean(x * x, axis=-1, keepdims=True)
    h = (x * lax.rsqrt(ms + EPS) * g_ref[...]).astype(bf16)

    def nat(lo, hi):
        return jnp.dot(h, wn_ref[:, lo:hi], preferred_element_type=f32)

    def silu(v):
        return v * jax.nn.sigmoid(v)

    pt = lax.dot_general(wt_ref[...], h, (((1,), (1,)), ((), ())), preferred_element_type=f32)
    qa_ref[...] = (pt[_T_AQ:_T_AV] * Q_SCALE).astype(bf16)
    va_ref[...] = pt[_T_AV:_T_BQ].astype(bf16)
    vb_ref[...] = pt[_T_BV:_T_END].astype(bf16)

    cos = cos_ref[...]
    sin = sin_ref[...]
    half = B_HEAD // 2

    def norm_rope(xt, g):
        msq = jnp.mean(xt * xt, axis=0, keepdims=True)
        y = xt * lax.rsqrt(msq + EPS) * g
        e, o = y[:half], y[half:]
        return jnp.concatenate([e * cos - o * sin, e * sin + o * cos], axis=0)

    qg = qg_ref[...]
    qn = jnp.concatenate([norm_rope(pt[_T_BQ + i * B_HEAD:_T_BQ + (i + 1) * B_HEAD], qg)
                          for i in range(B_Q_HEADS)], axis=0) * Q_SCALE
    first = jnp.bitwise_and(lax.broadcasted_iota(jnp.int32, qn.shape, 0), B_HEAD) == 0
    qb_ref[0] = jnp.where(first, qn, 0.0).astype(bf16)
    qb_ref[1] = jnp.where(first, 0.0, qn).astype(bf16)
    kg = kg_ref[...]
    kn = jnp.concatenate([norm_rope(pt[_T_BK + i * B_HEAD:_T_BK + (i + 1) * B_HEAD], kg)
                          for i in range(B_KV_HEADS)], axis=0)
    kb_ref[...] = kn.T.astype(bf16)

    sa_ref[...] = silu(nat(_N_AG, _N_BG)).astype(bf16)
    sb_ref[...] = silu(nat(_N_BG, _N_END)).astype(bf16)

    ak = nat(_N_AK, _N_AG)
    lane = lax.broadcasted_iota(jnp.int32, (ROW_TILE, LANES), 1)
    row = lax.broadcasted_iota(jnp.int32, (ROW_TILE, LANES), 0)
    chunk = ((pl.program_id(0) % tiles_per_seq) * (ROW_TILE // K_CHUNK)
             + jnp.right_shift(row, K_CHUNK.bit_length() - 1))
    hot = jnp.where(_bias_slot_mask(lane, chunk), 1.0, 0.0)
    low = lane < A_QK_HEAD
    for hd in range(A_HEADS):
        akh = ak[:, hd * LANES:(hd + 1) * LANES]
        ka_ref[:, 2 * hd * LANES:(2 * hd + 1) * LANES] = jnp.where(low, akh, hot).astype(bf16)
        ka_ref[:, (2 * hd + 1) * LANES:(2 * hd + 2) * LANES] = jnp.where(low, hot, akh).astype(bf16)


def _proj_call(xf, g_pre, wn, wt, cos_t, sin_t, qg, kg, seq, layer):
    m = xf.shape[0]
    tiles_per_seq = seq // ROW_TILE
    bf16 = jnp.bfloat16
    row = lambda i: (i, 0)
    col = lambda i: (0, i)
    const = lambda i: (0, 0)
    of_layer = lambda i: (layer, 0, 0)
    pos = lambda i: (0, i % tiles_per_seq)
    return pl.pallas_call(
        functools.partial(_proj_kernel, tiles_per_seq=tiles_per_seq),
        grid=(m // ROW_TILE,),
        in_specs=[pl.BlockSpec((ROW_TILE, D_MODEL), row),
                  pl.BlockSpec((1, D_MODEL), const),
                  pl.BlockSpec((None,) + wn.shape[1:], of_layer),
                  pl.BlockSpec((None,) + wt.shape[1:], of_layer),
                  pl.BlockSpec((B_HEAD // 2, ROW_TILE), pos),
                  pl.BlockSpec((B_HEAD // 2, ROW_TILE), pos),
                  pl.BlockSpec((None, B_HEAD, 1), of_layer),
                  pl.BlockSpec((None, B_HEAD, 1), of_layer)],
        out_specs=[pl.BlockSpec((A_WIDTH, ROW_TILE), col),
                   pl.BlockSpec((ROW_TILE, 2 * A_WIDTH), row),
                   pl.BlockSpec((A_WIDTH, ROW_TILE), col),
                   pl.BlockSpec((ROW_TILE, A_WIDTH), row),
                   pl.BlockSpec((2, B_WIDTH, ROW_TILE), lambda i: (0, 0, i)),
                   pl.BlockSpec((ROW_TILE, B_KV_HEADS * B_HEAD), row),
                   pl.BlockSpec((B_KV_HEADS * B_HEAD, ROW_TILE), col),
                   pl.BlockSpec((ROW_TILE, B_WIDTH), row)],
        out_shape=[jax.ShapeDtypeStruct((A_WIDTH, m), bf16),
                   jax.ShapeDtypeStruct((m, 2 * A_WIDTH), bf16),
                   jax.ShapeDtypeStruct((A_WIDTH, m), bf16),
                   jax.ShapeDtypeStruct((m, A_WIDTH), bf16),
                   jax.ShapeDtypeStruct((2, B_WIDTH, m), bf16),
                   jax.ShapeDtypeStruct((m, B_KV_HEADS * B_HEAD), bf16),
                   jax.ShapeDtypeStruct((B_KV_HEADS * B_HEAD, m), bf16),
                   jax.ShapeDtypeStruct((m, B_WIDTH), bf16)],
        compiler_params=_cparams(1),
        name="in_proj",
    )(xf, g_pre, wn, wt, cos_t, sin_t, qg, kg)


def _fold_rows(x, op):
    parts = [x[g * SUBLANES:(g + 1) * SUBLANES] for g in range(x.shape[0] // SUBLANES)]
    while len(parts) > 1:
        parts = [op(parts[i], parts[i + 1]) for i in range(0, len(parts), 2)]
    return parts[0]


def _pipeline_tiles(n_tiles, nq):
    t = pl.program_id(0)
    tile1 = jnp.minimum(t, n_tiles - 1)
    return t, tile1, tile1 % nq


def _fill_first_steps(t, s_odd, m_odd, acc_ref, sum_ref):
    @pl.when(t == 0)
    def _():
        s_odd[...] = jnp.zeros(s_odd.shape, jnp.float32)
        m_odd[...] = jnp.zeros(m_odd.shape, jnp.float32)
        acc_ref[...] = jnp.zeros(acc_ref.shape, jnp.float32)
        sum_ref[...] = jnp.ones(sum_ref.shape, jnp.float32)


def _by_parity(t, step, s_even, m_even, s_odd, m_odd):
    @pl.when(t % 2 == 0)
    def _():
        step(s_even, m_even, s_odd, m_odd)

    @pl.when(t % 2 == 1)
    def _():
        step(s_odd, m_odd, s_even, m_even)


def _attn_gqa_kernel(w_ref, k_ref, v_ref, gate_ref, o_ref,
                     s_even, s_odd, m_even, m_odd, acc_ref, sum_ref,
                     *, n_chunks, nq, n_tiles):
    f32, bf16 = jnp.float32, jnp.bfloat16
    t, _, _ = _pipeline_tiles(n_tiles, nq)
    _fill_first_steps(t, s_odd, m_odd, acc_ref, sum_ref)

    def step(s1_ref, m1_ref, s2_ref, m2_ref):
        ot = jnp.concatenate([acc_ref[st] / jnp.sum(sum_ref[st], axis=0, keepdims=True)
                              for st in range(2)], axis=0)
        o_ref[...] = (ot.T * gate_ref[...].astype(f32)).astype(o_ref.dtype)

        m_row = [jnp.max(m2_ref[st], axis=0, keepdims=True) for st in range(2)]

        m_acc = [None, None]
        l_acc = [None, None]
        o_acc = [None, None]
        for c in range(n_chunks):
            keys = slice(c * K_CHUNK, (c + 1) * K_CHUNK)
            for st in range(2):
                s = jnp.dot(k_ref[keys, :], w_ref[st], preferred_element_type=f32)
                s1_ref[c, st] = s
                cm = _fold_rows(s, jnp.maximum)
                m_acc[st] = cm if c == 0 else jnp.maximum(m_acc[st], cm)
            for st in range(2):
                p = jnp.exp2(s2_ref[c, st] - m_row[st])
                ps = _fold_rows(p, jnp.add)
                l_acc[st] = ps if c == 0 else l_acc[st] + ps
                part = jnp.dot(v_ref[st * B_HEAD:(st + 1) * B_HEAD, keys], p.astype(bf16),
                               preferred_element_type=f32)
                o_acc[st] = part if c == 0 else o_acc[st] + part
        for st in range(2):
            m1_ref[st] = m_acc[st]
            acc_ref[st] = o_acc[st]
            sum_ref[st] = l_acc[st]

    _by_parity(t, step, s_even, m_even, s_odd, m_odd)


def _attn_diff_kernel(rb_ref, lamp_ref, g_ref, q_ref, k_ref, v_ref, gate_ref, band_ref, o_ref,
                      w_ref, s_even, s_odd, m_even, m_odd, acc_ref, sum_ref,
                      *, lam_init, n_chunks, nq, n_tiles):
    f32, bf16 = jnp.float32, jnp.bfloat16
    t, tile1, qi1 = _pipeline_tiles(n_tiles, nq)
    head1 = (tile1 // nq) % A_HEADS
    _fill_first_steps(t, s_odd, m_odd, acc_ref, sum_ref)

    def step(s1_ref, m1_ref, s2_ref, m2_ref):
        row = lax.broadcasted_iota(jnp.int32, (LANES, Q_TILE), 0)
        top = row < A_QK_HEAD
        slot = jnp.bitwise_and(row, BIAS_SLOTS - 1)
        term = jnp.right_shift(jnp.bitwise_and(row, A_QK_HEAD - 1), 4)
        left = rb_ref[NUM_BUCKETS // 2 - 1, head1] * LOG2E
        right = rb_ref[NUM_BUCKETS - 1, head1] * LOG2E
        const = jnp.where(jnp.abs(slot - qi1) <= 1, 0.0, jnp.where(slot < qi1, left, right))
        t0 = const.astype(bf16).astype(f32)
        t1 = (const - t0).astype(bf16).astype(f32)
        t2 = (const - t0 - t1).astype(bf16).astype(f32)
        bias_rows = jnp.where(term == 0, t0, jnp.where(term == 1, t1,
                                                      jnp.where(term == 2, t2, 0.0)))
        qt = q_ref[...].astype(f32)
        w_ref[0] = jnp.where(top, qt, bias_rows).astype(bf16)
        w_ref[1] = jnp.where(top, bias_rows, qt).astype(bf16)

        l_row = [jnp.sum(sum_ref[st], axis=0, keepdims=True) for st in range(2)]
        lp = lamp_ref[...]
        lam = (jnp.exp(jnp.sum(lp[0:1] * lp[1:2], axis=-1, keepdims=True))
               - jnp.exp(jnp.sum(lp[2:3] * lp[3:4], axis=-1, keepdims=True)) + lam_init)
        o = (acc_ref[0] / l_row[0] - lam * (acc_ref[1] / l_row[1])).T
        ms = jnp.mean(o * o, axis=-1, keepdims=True)
        o = (o * lax.rsqrt(ms + EPS) * g_ref[...]) * (1.0 - lam_init)
        o_ref[...] = (o * gate_ref[...].astype(f32)).astype(o_ref.dtype)

        m_row = [jnp.max(m2_ref[st], axis=0, keepdims=True) for st in range(2)]

        m_acc = [None, None]
        l_acc = [None, None]
        o_acc = [None, None]
        for i in range(n_chunks):
            c1 = qi1 - 1 + i
            j1 = jnp.bitwise_and(c1, n_chunks - 1)
            key0 = pl.multiple_of(j1 * K_CHUNK, K_CHUNK)
            if i < BAND_CHUNKS:
                in_range = jnp.logical_and(c1 >= 0, c1 < n_chunks)
                band = jnp.where(in_range, band_ref[0, i], 0.0)
            for st in range(2):
                s = jnp.dot(k_ref[pl.ds(key0, K_CHUNK), st * LANES:(st + 1) * LANES], w_ref[st],
                            preferred_element_type=f32)
                if i < BAND_CHUNKS:
                    s = s + band
                s1_ref[j1, st] = s
                cm = _fold_rows(s, jnp.maximum)
                m_acc[st] = cm if i == 0 else jnp.maximum(m_acc[st], cm)
            for st in range(2):
                p = jnp.exp2(s2_ref[i, st] - m_row[st])
                ps = _fold_rows(p, jnp.add)
                l_acc[st] = ps if i == 0 else l_acc[st] + ps
                part = jnp.dot(v_ref[:, i * K_CHUNK:(i + 1) * K_CHUNK], p.astype(bf16),
                               preferred_element_type=f32)
                o_acc[st] = part if i == 0 else o_acc[st] + part
        for st in range(2):
            m1_ref[st] = m_acc[st]
            acc_ref[st] = o_acc[st]
            sum_ref[st] = l_acc[st]

    _by_parity(t, step, s_even, m_even, s_odd, m_odd)


def _attn_call(kernel_fn, operands, prefix_specs, q_arr, k_arr, v_arr, gate_arr, band_arr,
               *, n_pairs, per_pair, batch, seq, name):
    m = gate_arr.shape[0]
    nq = seq // Q_TILE
    n_chunks = seq // K_CHUNK
    n_tiles = batch * n_pairs * nq
    k_lanes = k_arr.shape[1] // (n_pairs if per_pair else 1)
    v_rows = LANES if per_pair else B_HEAD

    def decode(tile):
        return tile // (n_pairs * nq), (tile // nq) % n_pairs, tile % nq

    def stage1(t):
        return decode(jnp.minimum(t, n_tiles - 1))

    def stage2(t):
        return decode(jnp.clip(t - 1, 0, n_tiles - 1))

    def stage3(t):
        return decode(jnp.maximum(t - 2, 0))

    padded_q = q_arr.ndim == 3

    def q_map(t):
        b, p, i = stage1(t)
        return (0, p, b * nq + i) if padded_q else (p, b * nq + i)

    def k_map(t):
        b, p, _ = stage1(t)
        return b, p if per_pair else 0

    def v_map(t):
        b, p, _ = stage2(t)
        return p if per_pair else 0, b

    def out_map(t):
        b, p, i = stage3(t)
        return b * nq + i, p

    in_specs = list(prefix_specs) + [
        pl.BlockSpec((2, LANES, Q_TILE) if padded_q else (LANES, Q_TILE), q_map),
        pl.BlockSpec((seq, k_lanes), k_map),
        pl.BlockSpec((LANES, seq), v_map),
        pl.BlockSpec((Q_TILE, LANES), out_map)]
    tail = ()
    if band_arr is not None:
        in_specs.append(pl.BlockSpec((1, BAND_CHUNKS, K_CHUNK, Q_TILE),
                                     lambda t: (stage1(t)[1], 0, 0, 0)))
        tail = (band_arr,)
    scores = pltpu.VMEM((n_chunks, 2, K_CHUNK, Q_TILE), jnp.float32)
    per_key_group = pltpu.VMEM((2, SUBLANES, Q_TILE), jnp.float32)
    padded = [] if padded_q else [pltpu.VMEM((2, LANES, Q_TILE), jnp.bfloat16)]
    scratch = padded + [
               scores, scores,
               per_key_group, per_key_group,
               pltpu.VMEM((2, v_rows, Q_TILE), jnp.float32),
               per_key_group]
    return pl.pallas_call(
        functools.partial(kernel_fn, n_chunks=n_chunks, nq=nq, n_tiles=n_tiles),
        grid=(n_tiles + 2,),
        in_specs=in_specs,
        out_specs=pl.BlockSpec((Q_TILE, LANES), out_map),
        out_shape=jax.ShapeDtypeStruct((m, n_pairs * LANES), jnp.bfloat16),
        scratch_shapes=scratch,
        compiler_params=_cparams(1),
        name=name,
    )(*operands, q_arr, k_arr, v_arr, gate_arr, *tail)


def _attn_diff_call(rel_bias, lam_params, subln_g, qa, ka, va, sa, band, lam_init, batch, seq):
    n_chunks = seq // K_CHUNK
    assert n_chunks <= BIAS_SLOTS and n_chunks & (n_chunks - 1) == 0
    prefix = [pl.BlockSpec(memory_space=pltpu.SMEM),
              pl.BlockSpec((4, A_QK_HEAD), lambda t: (0, 0)),
              pl.BlockSpec((1, A_V_HEAD), lambda t: (0, 0))]
    return _attn_call(functools.partial(_attn_diff_kernel, lam_init=lam_init),
                      (rel_bias, lam_params, subln_g), prefix, qa, ka, va, sa, band,
                      n_pairs=A_HEADS, per_pair=True, batch=batch, seq=seq, name="attn_diff")


def _attn_gqa_call(qb, kb, vb, sb, batch, seq):
    return _attn_call(_attn_gqa_kernel, (), [], qb, kb, vb, sb, None,
                      n_pairs=B_Q_HEADS // 2, per_pair=False, batch=batch, seq=seq,
                      name="attn_gqa")


def _out_kernel(ya_ref, yb_ref, wa_ref, wb_ref, x_ref, g_ref, o_ref):
    f32 = jnp.float32
    y = (jnp.dot(ya_ref[...], wa_ref[...], preferred_element_type=f32)
         + jnp.dot(yb_ref[...], wb_ref[...], preferred_element_type=f32))
    ms = jnp.mean(y * y, axis=-1, keepdims=True)
    o_ref[...] = x_ref[...] + y * lax.rsqrt(ms + EPS) * g_ref[...]


def _out_call(ya, yb, wa, wb, xf, g_post, layer):
    m = xf.shape[0]
    row = lambda i: (i, 0)
    const = lambda i: (0, 0)
    return pl.pallas_call(
        _out_kernel,
        grid=(m // ROW_TILE,),
        in_specs=[pl.BlockSpec((ROW_TILE, A_WIDTH), row),
                  pl.BlockSpec((ROW_TILE, B_WIDTH), row),
                  pl.BlockSpec((None, A_WIDTH, D_MODEL), lambda i: (layer, 0, 0)),
                  pl.BlockSpec((None, B_WIDTH, D_MODEL), lambda i: (layer, 0, 0)),
                  pl.BlockSpec((ROW_TILE, D_MODEL), row),
                  pl.BlockSpec((1, D_MODEL), const)],
        out_specs=pl.BlockSpec((ROW_TILE, D_MODEL), row),
        out_shape=jax.ShapeDtypeStruct((m, D_MODEL), jnp.float32),
        compiler_params=_cparams(1),
        name="out_proj",
    )(ya, yb, wa, wb, xf, g_post)


def _layer_weights(w_in_l, w_out_l, q_norm_g_l, k_norm_g_l):
    bf16 = jnp.bfloat16
    d = w_in_l.shape[0]
    per_group = B_Q_HEADS // B_KV_HEADS
    half = B_HEAD // 2
    bq = w_in_l[:, _BQ:_BK].reshape(d, B_KV_HEADS, per_group, half, 2)
    bq = bq.transpose(0, 2, 1, 4, 3).reshape(d, B_WIDTH)
    bk = w_in_l[:, _BK:_BV].reshape(d, B_KV_HEADS, half, 2)
    bk = bk.transpose(0, 1, 3, 2).reshape(d, B_KV_HEADS * B_HEAD)
    bg = w_in_l[:, _BG:].reshape(d, B_KV_HEADS, per_group, B_HEAD)
    bg = bg.transpose(0, 2, 1, 3).reshape(d, B_WIDTH)
    wn = jnp.concatenate([w_in_l[:, _AK:_AV], w_in_l[:, _AG:_BQ], bg], axis=1).astype(bf16)
    wt = jnp.concatenate([w_in_l[:, _AQ:_AK], w_in_l[:, _AV:_AG], bq, bk, w_in_l[:, _BV:_BG]],
                         axis=1).astype(bf16).T
    wa = w_out_l[:A_WIDTH].astype(bf16)
    wb = w_out_l[A_WIDTH:].reshape(B_KV_HEADS, per_group, B_HEAD, -1)
    wb = wb.transpose(1, 0, 2, 3).reshape(B_WIDTH, -1).astype(bf16)
    qg = q_norm_g_l.reshape(half, 2).T.reshape(B_HEAD, 1)
    kg = k_norm_g_l.reshape(half, 2).T.reshape(B_HEAD, 1)
    return wn, wt, wa, wb, qg, kg


def kernel(x, rel_bias, pre_norm_g, w_in, diff_lambda, diff_subln_g, q_norm_g, k_norm_g,
           w_out, post_norm_g):
    batch, seq, d_model = x.shape
    xf = x.reshape(batch * seq, d_model)
    band = _band_call(rel_bias)
    cos_t, sin_t = _rope_tables_t(seq)
    wn, wt, wa, wb, qg, kg = jax.vmap(_layer_weights)(w_in, w_out, q_norm_g, k_norm_g)
    for l in range(DEPTH):
        lam_init = 0.8 - 0.6 * math.exp(-0.3 * l)
        qa, ka, va, sa, qb, kb, vb, sb = _proj_call(
            xf, pre_norm_g[l].reshape(1, d_model), wn, wt, cos_t, sin_t, qg, kg, seq, l)
        ya = _attn_diff_call(rel_bias, diff_lambda[l], diff_subln_g[l].reshape(1, A_V_HEAD),
                             qa, ka, va, sa, band, lam_init, batch, seq)
        yb = _attn_gqa_call(qb, kb, vb, sb, batch, seq)
        xf = _out_call(ya, yb, wa, wb, xf, post_norm_g[l].reshape(1, d_model), l)
    return xf.reshape(batch, seq, d_model)
```

```python
import functools
import math

import jax
import jax.numpy as jnp
from jax import lax
from jax.experimental import pallas as pl
from jax.experimental.pallas import tpu as pltpu

D_MODEL = 1024
DEPTH = 2
A_WIDTH = 512
B_WIDTH = 512
A_HEADS = 4
A_V_HEAD = 128
A_QK_HEAD = 64
B_HEAD = 64
B_Q_HEADS = 8
B_KV_HEADS = 2
GRID_W = 64
ROPE_THETA = 10000.0
NUM_BUCKETS = 32
MAX_DISTANCE = 128
EPS = 1e-6
LOG2E = math.log2(math.e)
Q_SCALE = (A_QK_HEAD ** -0.5) * LOG2E

LANES = 128
SUBLANES = 8
ROW_TILE = 1024
Q_TILE = 256
K_CHUNK = 256
BAND_CHUNKS = 3
BIAS_SLOTS = 16
BIAS_TERMS = 3
VMEM_LIMIT = 48 * 1024 * 1024

_AQ, _AK, _AV, _AG, _BQ, _BK, _BV, _BG = 0, 512, 1024, 1536, 2048, 2560, 2688, 2816
_T_AQ, _T_AV, _T_BQ, _T_BK, _T_BV, _T_END = 0, 512, 1024, 1536, 1664, 1792
_N_AK, _N_AG, _N_BG, _N_END = 0, 512, 1024, 1536


def _t5_bucket(rel):
    nb = NUM_BUCKETS // 2
    max_exact = nb // 2
    n = jnp.abs(rel)
    large = max_exact + (jnp.log(jnp.maximum(n, 1).astype(jnp.float32) / max_exact)
                         / math.log(MAX_DISTANCE / max_exact) * (nb - max_exact)).astype(jnp.int32)
    large = jnp.minimum(large, nb - 1)
    return jnp.where(rel > 0, nb, 0) + jnp.where(n < max_exact, n, large)


def _rope_tables_t(n):
    rows = n // GRID_W
    axis_dim = B_HEAD // 2
    inv = ROPE_THETA ** (-jnp.arange(0, axis_dim, 2, dtype=jnp.float32) / axis_dim)
    row_ang = (jnp.arange(rows, dtype=jnp.float32)[:, None] * inv).T
    col_ang = (jnp.arange(GRID_W, dtype=jnp.float32)[:, None] * inv).T

    def expand(fn):
        by_row = jnp.repeat(fn(row_ang), GRID_W, axis=1)
        by_col = jnp.tile(fn(col_ang), (1, rows))
        return jnp.concatenate([by_row, by_col], axis=0)

    return expand(jnp.cos), expand(jnp.sin)


def _cparams(n_axes):
    return pltpu.CompilerParams(dimension_semantics=("arbitrary",) * n_axes,
                                vmem_limit_bytes=VMEM_LIMIT)


def _bias_slot_mask(slot_index, chunk):
    in_terms = jnp.right_shift(jnp.bitwise_and(slot_index, A_QK_HEAD - 1), 4) < BIAS_TERMS
    return jnp.logical_and(in_terms, jnp.bitwise_and(slot_index, BIAS_SLOTS - 1) == chunk)


BAND_TABLE = 2 * Q_TILE


def _band_kernel(rb_ref, idx_ref, out_ref):
    h = pl.program_id(0)
    for d in range(BAND_CHUNKS):
        idx = jnp.broadcast_to(idx_ref[d], (SUBLANES, BAND_TABLE))
        vals = jnp.zeros(idx.shape, jnp.float32)
        for b in range(NUM_BUCKETS):
            vals = jnp.where(idx == b, rb_ref[b, h] * LOG2E, vals)

        for j0 in range(0, K_CHUNK, SUBLANES):
            rolled = pltpu.roll(vals, (j0 + Q_TILE + 1) % BAND_TABLE, 1, stride=1, stride_axis=0)
            out_ref[0, d, j0:j0 + SUBLANES, :] = rolled[:, :Q_TILE]


def _band_call(rel_bias):
    d = jnp.arange(BAND_CHUNKS, dtype=jnp.int32)[:, None, None]
    c = jnp.arange(BAND_TABLE, dtype=jnp.int32)[None, None, :]
    idx = _t5_bucket((d - 1) * K_CHUNK + (Q_TILE - 1) - c)
    idx = jnp.take(jnp.arange(NUM_BUCKETS, dtype=jnp.int32), idx)
    return pl.pallas_call(
        _band_kernel,
        grid=(A_HEADS,),
        in_specs=[pl.BlockSpec(memory_space=pltpu.SMEM),
                  pl.BlockSpec((BAND_CHUNKS, 1, BAND_TABLE), lambda h: (0, 0, 0))],
        out_specs=pl.BlockSpec((1, BAND_CHUNKS, K_CHUNK, Q_TILE), lambda h: (h, 0, 0, 0)),
        out_shape=jax.ShapeDtypeStruct((A_HEADS, BAND_CHUNKS, K_CHUNK, Q_TILE), jnp.float32),
        compiler_params=_cparams(1),
        name="bias_band",
    )(rel_bias, idx)


def _proj_kernel(x_ref, g_ref, wn_ref, wt_ref, cos_ref, sin_ref, qg_ref, kg_ref,
                 qa_ref, ka_ref, va_ref, sa_ref, qb_ref, kb_ref, vb_ref, sb_ref,
                 *, tiles_per_seq):
    f32, bf16 = jnp.float32, jnp.bfloat16
    x = x_ref[...]
    ms = jnp.mean(x * x, axis=-1, keepdims=True)
    h = (x * lax.rsqrt(ms + EPS) * g_ref[...]).astype(bf16)

    def nat(lo, hi):
        return jnp.dot(h, wn_ref[:, lo:hi], preferred_element_type=f32)

    def silu(v):
        return v * jax.nn.sigmoid(v)

    pt = lax.dot_general(wt_ref[...], h, (((1,), (1,)), ((), ())), preferred_element_type=f32)
    qa_ref[...] = (pt[_T_AQ:_T_AV] * Q_SCALE).astype(bf16)
    va_ref[...] = pt[_T_AV:_T_BQ].astype(bf16)
    vb_ref[...] = pt[_T_BV:_T_END].astype(bf16)

    cos = cos_ref[...]
    sin = sin_ref[...]
    half = B_HEAD // 2

    def norm_rope(xt, g):
        msq = jnp.mean(xt * xt, axis=0, keepdims=True)
        y = xt * lax.rsqrt(msq + EPS) * g
        e, o = y[:half], y[half:]
        return jnp.concatenate([e * cos - o * sin, e * sin + o * cos], axis=0)

    qg = qg_ref[...]
    qn = jnp.concatenate([norm_rope(pt[_T_BQ + i * B_HEAD:_T_BQ + (i + 1) * B_HEAD], qg)
                          for i in range(B_Q_HEADS)], axis=0) * Q_SCALE
    qb_ref[...] = qn.astype(bf16)
    kg = kg_ref[...]
    kn = jnp.concatenate([norm_rope(pt[_T_BK + i * B_HEAD:_T_BK + (i + 1) * B_HEAD], kg)
                          for i in range(B_KV_HEADS)], axis=0)
    kb_ref[...] = kn.T.astype(bf16)

    sa_ref[...] = silu(nat(_N_AG, _N_BG)).astype(bf16)
    sb_ref[...] = silu(nat(_N_BG, _N_END)).astype(bf16)

    ak = nat(_N_AK, _N_AG)
    lane = lax.broadcasted_iota(jnp.int32, (ROW_TILE, LANES), 1)
    row = lax.broadcasted_iota(jnp.int32, (ROW_TILE, LANES), 0)
    chunk = ((pl.program_id(0) % tiles_per_seq) * (ROW_TILE // K_CHUNK)
             + jnp.right_shift(row, K_CHUNK.bit_length() - 1))
    hot = jnp.where(_bias_slot_mask(lane, chunk), 1.0, 0.0)
    low = lane < A_QK_HEAD
    for hd in range(A_HEADS):
        akh = ak[:, hd * LANES:(hd + 1) * LANES]
        ka_ref[:, 2 * hd * LANES:(2 * hd + 1) * LANES] = jnp.where(low, akh, hot).astype(bf16)
        ka_ref[:, (2 * hd + 1) * LANES:(2 * hd + 2) * LANES] = jnp.where(low, hot, akh).astype(bf16)


def _proj_call(xf, g_pre, wn, wt, cos_t, sin_t, qg, kg, seq, layer):
    m = xf.shape[0]
    tiles_per_seq = seq // ROW_TILE
    bf16 = jnp.bfloat16
    row = lambda i: (i, 0)
    col = lambda i: (0, i)
    const = lambda i: (0, 0)
    of_layer = lambda i: (layer, 0, 0)
    pos = lambda i: (0, i % tiles_per_seq)
    return pl.pallas_call(
        functools.partial(_proj_kernel, tiles_per_seq=tiles_per_seq),
        grid=(m // ROW_TILE,),
        in_specs=[pl.BlockSpec((ROW_TILE, D_MODEL), row),
                  pl.BlockSpec((1, D_MODEL), const),
                  pl.BlockSpec((None,) + wn.shape[1:], of_layer, pipeline_mode=pl.Buffered(1)),
                  pl.BlockSpec((None,) + wt.shape[1:], of_layer, pipeline_mode=pl.Buffered(1)),
                  pl.BlockSpec((B_HEAD // 2, ROW_TILE), pos),
                  pl.BlockSpec((B_HEAD // 2, ROW_TILE), pos),
                  pl.BlockSpec((None, B_HEAD, 1), of_layer),
                  pl.BlockSpec((None, B_HEAD, 1), of_layer)],
        out_specs=[pl.BlockSpec((A_WIDTH, ROW_TILE), col),
                   pl.BlockSpec((ROW_TILE, 2 * A_WIDTH), row),
                   pl.BlockSpec((A_WIDTH, ROW_TILE), col),
                   pl.BlockSpec((ROW_TILE, A_WIDTH), row),
                   pl.BlockSpec((B_WIDTH, ROW_TILE), col),
                   pl.BlockSpec((ROW_TILE, B_KV_HEADS * B_HEAD), row),
                   pl.BlockSpec((B_KV_HEADS * B_HEAD, ROW_TILE), col),
                   pl.BlockSpec((ROW_TILE, B_WIDTH), row)],
        out_shape=[jax.ShapeDtypeStruct((A_WIDTH, m), bf16),
                   jax.ShapeDtypeStruct((m, 2 * A_WIDTH), bf16),
                   jax.ShapeDtypeStruct((A_WIDTH, m), bf16),
                   jax.ShapeDtypeStruct((m, A_WIDTH), bf16),
                   jax.ShapeDtypeStruct((B_WIDTH, m), bf16),
                   jax.ShapeDtypeStruct((m, B_KV_HEADS * B_HEAD), bf16),
                   jax.ShapeDtypeStruct((B_KV_HEADS * B_HEAD, m), bf16),
                   jax.ShapeDtypeStruct((m, B_WIDTH), bf16)],
        compiler_params=_cparams(1),
        name="in_proj",
    )(xf, g_pre, wn, wt, cos_t, sin_t, qg, kg)


def _fold_rows(x, op):
    parts = [x[g * SUBLANES:(g + 1) * SUBLANES] for g in range(x.shape[0] // SUBLANES)]
    while len(parts) > 1:
        parts = [op(parts[i], parts[i + 1]) for i in range(0, len(parts), 2)]
    return parts[0]


def _pipeline_tiles(n_tiles, nq):
    t = pl.program_id(0)
    tile1 = jnp.minimum(t, n_tiles - 1)
    return t, tile1, tile1 % nq


def _fill_first_steps(t, s_odd, m_odd, acc_ref, sum_ref):
    @pl.when(t == 0)
    def _():
        s_odd[...] = jnp.zeros(s_odd.shape, jnp.float32)
        m_odd[...] = jnp.zeros(m_odd.shape, jnp.float32)
        acc_ref[...] = jnp.zeros(acc_ref.shape, jnp.float32)
        sum_ref[...] = jnp.ones(sum_ref.shape, jnp.float32)


def _by_parity(t, step, s_even, m_even, s_odd, m_odd):
    @pl.when(t % 2 == 0)
    def _():
        step(s_even, m_even, s_odd, m_odd)

    @pl.when(t % 2 == 1)
    def _():
        step(s_odd, m_odd, s_even, m_even)


def _attn_gqa_kernel(q_ref, k_ref, v_ref, gate_ref, o_ref,
                     w_ref, s_even, s_odd, m_even, m_odd, acc_ref, sum_ref,
                     *, n_chunks, nq, n_tiles):
    f32, bf16 = jnp.float32, jnp.bfloat16
    t, _, _ = _pipeline_tiles(n_tiles, nq)
    _fill_first_steps(t, s_odd, m_odd, acc_ref, sum_ref)

    def step(s1_ref, m1_ref, s2_ref, m2_ref):
        top = lax.broadcasted_iota(jnp.int32, (LANES, Q_TILE), 0) < B_HEAD
        qt = q_ref[...].astype(f32)
        w_ref[0] = jnp.where(top, qt, 0.0).astype(bf16)
        w_ref[1] = jnp.where(top, 0.0, qt).astype(bf16)

        ot = jnp.concatenate([acc_ref[st] / jnp.sum(sum_ref[st], axis=0, keepdims=True)
                              for st in range(2)], axis=0)
        o_ref[...] = (ot.T * gate_ref[...].astype(f32)).astype(o_ref.dtype)

        m_row = [jnp.max(m2_ref[st], axis=0, keepdims=True) for st in range(2)]

        m_acc = [None, None]
        l_acc = [None, None]
        o_acc = [None, None]
        for c in range(n_chunks):
            keys = slice(c * K_CHUNK, (c + 1) * K_CHUNK)
            for st in range(2):
                s = jnp.dot(k_ref[keys, :], w_ref[st], preferred_element_type=f32)
                s1_ref[c, st] = s
                cm = _fold_rows(s, jnp.maximum)
                m_acc[st] = cm if c == 0 else jnp.maximum(m_acc[st], cm)
            for st in range(2):
                p = jnp.exp2(s2_ref[c, st] - m_row[st])
                ps = _fold_rows(p, jnp.add)
                l_acc[st] = ps if c == 0 else l_acc[st] + ps
                part = jnp.dot(v_ref[st * B_HEAD:(st + 1) * B_HEAD, keys], p.astype(bf16),
                               preferred_element_type=f32)
                o_acc[st] = part if c == 0 else o_acc[st] + part
        for st in range(2):
            m1_ref[st] = m_acc[st]
            acc_ref[st] = o_acc[st]
            sum_ref[st] = l_acc[st]

    _by_parity(t, step, s_even, m_even, s_odd, m_odd)


def _attn_diff_kernel(rb_ref, lamp_ref, g_ref, q_ref, k_ref, v_ref, gate_ref, band_ref, o_ref,
                      w_ref, s_even, s_odd, m_even, m_odd, acc_ref, sum_ref,
                      *, lam_init, n_chunks, nq, n_tiles):
    f32, bf16 = jnp.float32, jnp.bfloat16
    t, tile1, qi1 = _pipeline_tiles(n_tiles, nq)
    head1 = (tile1 // nq) % A_HEADS
    _fill_first_steps(t, s_odd, m_odd, acc_ref, sum_ref)

    def step(s1_ref, m1_ref, s2_ref, m2_ref):
        row = lax.broadcasted_iota(jnp.int32, (LANES, Q_TILE), 0)
        top = row < A_QK_HEAD
        slot = jnp.bitwise_and(row, BIAS_SLOTS - 1)
        term = jnp.right_shift(jnp.bitwise_and(row, A_QK_HEAD - 1), 4)
        left = rb_ref[NUM_BUCKETS // 2 - 1, head1] * LOG2E
        right = rb_ref[NUM_BUCKETS - 1, head1] * LOG2E
        const = jnp.where(jnp.abs(slot - qi1) <= 1, 0.0, jnp.where(slot < qi1, left, right))
        t0 = const.astype(bf16).astype(f32)
        t1 = (const - t0).astype(bf16).astype(f32)
        t2 = (const - t0 - t1).astype(bf16).astype(f32)
        bias_rows = jnp.where(term == 0, t0, jnp.where(term == 1, t1,
                                                      jnp.where(term == 2, t2, 0.0)))
        qt = q_ref[...].astype(f32)
        w_ref[0] = jnp.where(top, qt, bias_rows).astype(bf16)
        w_ref[1] = jnp.where(top, bias_rows, qt).astype(bf16)

        l_row = [jnp.sum(sum_ref[st], axis=0, keepdims=True) for st in range(2)]
        lp = lamp_ref[...]
        lam = (jnp.exp(jnp.sum(lp[0:1] * lp[1:2], axis=-1, keepdims=True))
               - jnp.exp(jnp.sum(lp[2:3] * lp[3:4], axis=-1, keepdims=True)) + lam_init)
        o = (acc_ref[0] / l_row[0] - lam * (acc_ref[1] / l_row[1])).T
        ms = jnp.mean(o * o, axis=-1, keepdims=True)
        o = (o * lax.rsqrt(ms + EPS) * g_ref[...]) * (1.0 - lam_init)
        o_ref[...] = (o * gate_ref[...].astype(f32)).astype(o_ref.dtype)

        m_row = [jnp.max(m2_ref[st], axis=0, keepdims=True) for st in range(2)]

        m_acc = [None, None]
        l_acc = [None, None]
        o_acc = [None, None]
        for i in range(n_chunks):
            c1 = qi1 - 1 + i
            j1 = jnp.bitwise_and(c1, n_chunks - 1)
            key0 = pl.multiple_of(j1 * K_CHUNK, K_CHUNK)
            if i < BAND_CHUNKS:
                in_range = jnp.logical_and(c1 >= 0, c1 < n_chunks)
                band = jnp.where(in_range, band_ref[0, i], 0.0)
            for st in range(2):
                s = jnp.dot(k_ref[pl.ds(key0, K_CHUNK), st * LANES:(st + 1) * LANES], w_ref[st],
                            preferred_element_type=f32)
                if i < BAND_CHUNKS:
                    s = s + band
                s1_ref[j1, st] = s
                cm = _fold_rows(s, jnp.maximum)
                m_acc[st] = cm if i == 0 else jnp.maximum(m_acc[st], cm)
            for st in range(2):
                p = jnp.exp2(s2_ref[i, st] - m_row[st])
                ps = _fold_rows(p, jnp.add)
                l_acc[st] = ps if i == 0 else l_acc[st] + ps
                part = jnp.dot(v_ref[:, i * K_CHUNK:(i + 1) * K_CHUNK], p.astype(bf16),
                               preferred_element_type=f32)
                o_acc[st] = part if i == 0 else o_acc[st] + part
        for st in range(2):
            m1_ref[st] = m_acc[st]
            acc_ref[st] = o_acc[st]
            sum_ref[st] = l_acc[st]

    _by_parity(t, step, s_even, m_even, s_odd, m_odd)


def _attn_call(kernel_fn, operands, prefix_specs, q_arr, k_arr, v_arr, gate_arr, band_arr,
               *, n_pairs, per_pair, batch, seq, name):
    m = gate_arr.shape[0]
    nq = seq // Q_TILE
    n_chunks = seq // K_CHUNK
    n_tiles = batch * n_pairs * nq
    k_lanes = k_arr.shape[1] // (n_pairs if per_pair else 1)
    v_rows = LANES if per_pair else B_HEAD

    def decode(tile):
        return tile // (n_pairs * nq), (tile // nq) % n_pairs, tile % nq

    def stage1(t):
        return decode(jnp.minimum(t, n_tiles - 1))

    def stage2(t):
        return decode(jnp.clip(t - 1, 0, n_tiles - 1))

    def stage3(t):
        return decode(jnp.maximum(t - 2, 0))

    def q_map(t):
        b, p, i = stage1(t)
        return p, b * nq + i

    def k_map(t):
        b, p, _ = stage1(t)
        return b, p if per_pair else 0

    def v_map(t):
        b, p, _ = stage2(t)
        return p if per_pair else 0, b

    def out_map(t):
        b, p, i = stage3(t)
        return b * nq + i, p

    in_specs = list(prefix_specs) + [
        pl.BlockSpec((LANES, Q_TILE), q_map),
        pl.BlockSpec((seq, k_lanes), k_map),
        pl.BlockSpec((LANES, seq), v_map),
        pl.BlockSpec((Q_TILE, LANES), out_map)]
    tail = ()
    if band_arr is not None:
        in_specs.append(pl.BlockSpec((1, BAND_CHUNKS, K_CHUNK, Q_TILE),
                                     lambda t: (stage1(t)[1], 0, 0, 0)))
        tail = (band_arr,)
    scores = pltpu.VMEM((n_chunks, 2, K_CHUNK, Q_TILE), jnp.float32)
    per_key_group = pltpu.VMEM((2, SUBLANES, Q_TILE), jnp.float32)
    scratch = [pltpu.VMEM((2, LANES, Q_TILE), jnp.bfloat16),
               scores, scores,
               per_key_group, per_key_group,
               pltpu.VMEM((2, v_rows, Q_TILE), jnp.float32),
               per_key_group]
    return pl.pallas_call(
        functools.partial(kernel_fn, n_chunks=n_chunks, nq=nq, n_tiles=n_tiles),
        grid=(n_tiles + 2,),
        in_specs=in_specs,
        out_specs=pl.BlockSpec((Q_TILE, LANES), out_map),
        out_shape=jax.ShapeDtypeStruct((m, n_pairs * LANES), jnp.bfloat16),
        scratch_shapes=scratch,
        compiler_params=_cparams(1),
        name=name,
    )(*operands, q_arr, k_arr, v_arr, gate_arr, *tail)


def _attn_diff_call(rel_bias, lam_params, subln_g, qa, ka, va, sa, band, lam_init, batch, seq):
    n_chunks = seq // K_CHUNK
    assert n_chunks <= BIAS_SLOTS and n_chunks & (n_chunks - 1) == 0
    prefix = [pl.BlockSpec(memory_space=pltpu.SMEM),
              pl.BlockSpec((4, A_QK_HEAD), lambda t: (0, 0)),
              pl.BlockSpec((1, A_V_HEAD), lambda t: (0, 0))]
    return _attn_call(functools.partial(_attn_diff_kernel, lam_init=lam_init),
                      (rel_bias, lam_params, subln_g), prefix, qa, ka, va, sa, band,
                      n_pairs=A_HEADS, per_pair=True, batch=batch, seq=seq, name="attn_diff")


def _attn_gqa_call(qb, kb, vb, sb, batch, seq):
    return _attn_call(_attn_gqa_kernel, (), [], qb, kb, vb, sb, None,
                      n_pairs=B_Q_HEADS // 2, per_pair=False, batch=batch, seq=seq,
                      name="attn_gqa")


def _out_kernel(ya_ref, yb_ref, wa_ref, wb_ref, x_ref, g_ref, o_ref):
    f32 = jnp.float32
    y = (jnp.dot(ya_ref[...], wa_ref[...], preferred_element_type=f32)
         + jnp.dot(yb_ref[...], wb_ref[...], preferred_element_type=f32))
    ms = jnp.mean(y * y, axis=-1, keepdims=True)
    o_ref[...] = x_ref[...] + y * lax.rsqrt(ms + EPS) * g_ref[...]


def _out_call(ya, yb, wa, wb, xf, g_post, layer):
    m = xf.shape[0]
    row = lambda i: (i, 0)
    const = lambda i: (0, 0)
    return pl.pallas_call(
        _out_kernel,
        grid=(m // ROW_TILE,),
        in_specs=[pl.BlockSpec((ROW_TILE, A_WIDTH), row),
                  pl.BlockSpec((ROW_TILE, B_WIDTH), row),
                  pl.BlockSpec((None, A_WIDTH, D_MODEL), lambda i: (layer, 0, 0),
                               pipeline_mode=pl.Buffered(1)),
                  pl.BlockSpec((None, B_WIDTH, D_MODEL), lambda i: (layer, 0, 0),
                               pipeline_mode=pl.Buffered(1)),
                  pl.BlockSpec((ROW_TILE, D_MODEL), row),
                  pl.BlockSpec((1, D_MODEL), const)],
        out_specs=pl.BlockSpec((ROW_TILE, D_MODEL), row),
        out_shape=jax.ShapeDtypeStruct((m, D_MODEL), jnp.float32),
        compiler_params=_cparams(1),
        name="out_proj",
    )(ya, yb, wa, wb, xf, g_post)


def _layer_weights(w_in_l, w_out_l, q_norm_g_l, k_norm_g_l):
    bf16 = jnp.bfloat16
    d = w_in_l.shape[0]
    per_group = B_Q_HEADS // B_KV_HEADS
    half = B_HEAD // 2
    bq = w_in_l[:, _BQ:_BK].reshape(d, B_KV_HEADS, per_group, half, 2)
    bq = bq.transpose(0, 2, 1, 4, 3).reshape(d, B_WIDTH)
    bk = w_in_l[:, _BK:_BV].reshape(d, B_KV_HEADS, half, 2)
    bk = bk.transpose(0, 1, 3, 2).reshape(d, B_KV_HEADS * B_HEAD)
    bg = w_in_l[:, _BG:].reshape(d, B_KV_HEADS, per_group, B_HEAD)
    bg = bg.transpose(0, 2, 1, 3).reshape(d, B_WIDTH)
    wn = jnp.concatenate([w_in_l[:, _AK:_AV], w_in_l[:, _AG:_BQ], bg], axis=1).astype(bf16)
    wt = jnp.concatenate([w_in_l[:, _AQ:_AK], w_in_l[:, _AV:_AG], bq, bk, w_in_l[:, _BV:_BG]],
                         axis=1).astype(bf16).T
    wa = w_out_l[:A_WIDTH].astype(bf16)
    wb = w_out_l[A_WIDTH:].reshape(B_KV_HEADS, per_group, B_HEAD, -1)
    wb = wb.transpose(1, 0, 2, 3).reshape(B_WIDTH, -1).astype(bf16)
    qg = q_norm_g_l.reshape(half, 2).T.reshape(B_HEAD, 1)
    kg = k_norm_g_l.reshape(half, 2).T.reshape(B_HEAD, 1)
    return wn, wt, wa, wb, qg, kg


def kernel(x, rel_bias, pre_norm_g, w_in, diff_lambda, diff_subln_g, q_norm_g, k_norm_g,
           w_out, post_norm_g):
    batch, seq, d_model = x.shape
    xf = x.reshape(batch * seq, d_model)
    band = _band_call(rel_bias)
    cos_t, sin_t = _rope_tables_t(seq)
    wn, wt, wa, wb, qg, kg = jax.vmap(_layer_weights)(w_in, w_out, q_norm_g, k_norm_g)
    for l in range(DEPTH):
        lam_init = 0.8 - 0.6 * math.exp(-0.3 * l)
        qa, ka, va, sa, qb, kb, vb, sb = _proj_call(
            xf, pre_norm_g[l].reshape(1, d_model), wn, wt, cos_t, sin_t, qg, kg, seq, l)
        ya = _attn_diff_call(rel_bias, diff_lambda[l], diff_subln_g[l].reshape(1, A_V_HEAD),
                             qa, ka, va, sa, band, lam_init, batch, seq)
        yb = _attn_gqa_call(qb, kb, vb, sb, batch, seq)
        xf = _out_call(ya, yb, wa, wb, xf, post_norm_g[l].reshape(1, d_model), l)
    return xf.reshape(batch, seq, d_model)
```
